```python
import jax
import jax.numpy as jnp
from jax import lax
import numpy as np

D_MODEL = 4096
BATCH = 2
SEQ = 4096
DEPTH = 2

RET_HEADS = 16
RET_QK_DIM = 128
RET_V_DIM = D_MODEL // RET_HEADS
RET_CHUNK = 128
HGRN_HEADS = 16
HGRN_EXPAND = 128
HGRN_V_DIM = D_MODEL // HGRN_HEADS
HGRN_CHUNK = 32
MLA_HEADS = 32
MLA_Q_RANK = 1024
MLA_KV_RANK = 512
MLA_NOPE_DIM = 128
MLA_ROPE_DIM = 64
MLA_V_DIM = D_MODEL // MLA_HEADS
MLA_Q_BLOCK = 128
N_BRANCHES = 3
FFN_DIM = 14336
N_EXPERTS = 8
TOP_K = 2
EXPERT_DIM = 3584
N_DENSE = (DEPTH + 1) // 2
N_MOE = DEPTH // 2
ROPE_THETA = 10000.0
EPS = 1e-6

IN_WIDTHS = (
    RET_HEADS * RET_QK_DIM,
    RET_HEADS * RET_QK_DIM,
    RET_HEADS * RET_V_DIM,
    RET_HEADS * RET_V_DIM,
    HGRN_HEADS * HGRN_EXPAND,
    HGRN_HEADS * HGRN_EXPAND,
    HGRN_HEADS * HGRN_V_DIM,
    HGRN_HEADS * HGRN_V_DIM,
    MLA_Q_RANK,
    MLA_KV_RANK,
    MLA_ROPE_DIM,
    N_BRANCHES * D_MODEL,
)
D_IN = sum(IN_WIDTHS)

kernel_name = 'hybrid_retention_hgrn2_mla_moe'


def split_points():
    pts, acc = [], 0
    for w in IN_WIDTHS[:-1]:
        acc += w
        pts.append(acc)
    return pts


def rmsnorm(x, g):
    x32 = x.astype(jnp.float32)
    y = x32 * lax.rsqrt(jnp.mean(x32 * x32, axis=-1, keepdims=True) + EPS)
    return (y * g.astype(jnp.float32)).astype(x.dtype)


def head_rmsnorm(o, g):
    H, d = o.shape[-2], o.shape[-1]
    o32 = o.astype(jnp.float32)
    y = o32 * lax.rsqrt(jnp.mean(o32 * o32, axis=-1, keepdims=True) + EPS)
    return y * g.astype(jnp.float32).reshape(H, d)


def rope_cos_sin(positions, dim):
    inv = ROPE_THETA ** (-jnp.arange(0, dim, 2, dtype=jnp.float32) / dim)
    ang = positions.astype(jnp.float32)[..., None] * inv
    return jnp.cos(ang), jnp.sin(ang)


def apply_rope(x, cos, sin):
    x1, x2 = jnp.split(x, 2, axis=-1)
    return jnp.concatenate([x1 * cos - x2 * sin, x2 * cos + x1 * sin], axis=-1)


def to_chunks(x, c):
    B, S, H, d = x.shape
    return x.reshape(B, S // c, c, H, d).transpose(1, 0, 3, 2, 4)


def from_chunks(x):
    n, B, H, c, d = x.shape
    return x.transpose(1, 0, 3, 2, 4).reshape(B, n * c, H, d)


def retention(q, k, v, g, cos, sin, norm_g):
    B, S = q.shape[0], q.shape[1]
    C = RET_CHUNK
    q = apply_rope(q.reshape(B, S, RET_HEADS, RET_QK_DIM), cos, sin)
    k = apply_rope(k.reshape(B, S, RET_HEADS, RET_QK_DIM), cos, sin) * (RET_QK_DIM ** -0.5)
    v = v.reshape(B, S, RET_HEADS, RET_V_DIM)
    log_gamma = jnp.log(1.0 - jnp.exp2(-5.0 - jnp.arange(RET_HEADS, dtype=jnp.float32)))
    idx = jnp.arange(C, dtype=jnp.float32)
    rel = idx[:, None] - idx[None, :]
    decay = jnp.where(rel >= 0, jnp.exp(log_gamma[:, None, None] * jnp.maximum(rel, 0.0)), 0.0)
    q_dec = jnp.exp(log_gamma[:, None] * (idx + 1.0))[:, :, None]
    k_dec = jnp.exp(log_gamma[:, None] * (C - 1.0 - idx))[:, :, None]
    chunk_dec = jnp.exp(log_gamma * C)[:, None, None]
    qc, kc, vc = to_chunks(q, C), to_chunks(k, C), to_chunks(v, C)
    scores = jnp.einsum('nbhid,nbhjd->nbhij', qc, kc) * decay
    intra = jnp.einsum('nbhij,nbhjv->nbhiv', scores, vc)

    def step(state, inp):
        qi, ki, vi = inp
        cross = jnp.einsum('bhid,bhdv->bhiv', qi * q_dec, state)
        state = chunk_dec * state + jnp.einsum('bhjd,bhjv->bhdv', ki * k_dec, vi)
        return state, cross

    state0 = jnp.zeros((B, RET_HEADS, RET_QK_DIM, RET_V_DIM), qc.dtype)
    _, cross = lax.scan(step, state0, (qc, kc, vc))
    o = from_chunks(intra + cross)
    return head_rmsnorm(o, norm_g).reshape(B, S, -1) * jax.nn.silu(g)


def hgrn2(f_pre, q, i, og, lb, norm_g):
    B, S = f_pre.shape[0], f_pre.shape[1]
    C = HGRN_CHUNK
    f_pre = f_pre.reshape(B, S, HGRN_HEADS, HGRN_EXPAND)
    lb = lb.astype(jnp.float32).reshape(HGRN_HEADS, HGRN_EXPAND)
    log_f = jnp.logaddexp(jnp.log(lb), jnp.log1p(-lb) + jax.nn.log_sigmoid(f_pre))
    k = (1.0 - lb) * jax.nn.sigmoid(-f_pre)
    qc = to_chunks(q.reshape(B, S, HGRN_HEADS, HGRN_EXPAND), C)
    kc = to_chunks(k, C)
    vc = to_chunks(i.reshape(B, S, HGRN_HEADS, HGRN_V_DIM), C)
    b = jnp.cumsum(to_chunks(log_f, C), axis=3)
    b_end = b[:, :, :, -1:, :]
    b_mid = b[:, :, :, C // 2 - 1:C // 2, :]
    causal = jnp.tril(jnp.ones((C, C), dtype=bool))
    scores = jnp.einsum('nbhte,nbhse->nbhts', qc * jnp.exp(b - b_mid), kc * jnp.exp(b_mid - b))
    intra = jnp.einsum('nbhts,nbhsv->nbhtv', jnp.where(causal, scores, 0.0), vc)
    q_in = qc * jnp.exp(b)
    k_out = kc * jnp.exp(b_end - b)
    dec = jnp.exp(b_end[:, :, :, 0, :])[..., None]

    def step(state, inp):
        qi, ki, vi, di = inp
        inter = jnp.einsum('bhte,bhev->bhtv', qi, state)
        state = di * state + jnp.einsum('bhse,bhsv->bhev', ki, vi)
        return state, inter

    state0 = jnp.zeros((B, HGRN_HEADS, HGRN_EXPAND, HGRN_V_DIM), q_in.dtype)
    _, inter = lax.scan(step, state0, (q_in, k_out, vc, dec))
    o = from_chunks(intra + inter)
    return head_rmsnorm(o, norm_g).reshape(B, S, -1) * jax.nn.silu(og)


def mla(c_q, c_kv, k_rope, cos, sin, q_norm, w_uq, kv_norm, w_ukv):
    B, S = c_q.shape[0], c_q.shape[1]
    q = (rmsnorm(c_q, q_norm) @ w_uq).reshape(B, S, MLA_HEADS, MLA_NOPE_DIM + MLA_ROPE_DIM)
    q_nope, q_rope = q[..., :MLA_NOPE_DIM], q[..., MLA_NOPE_DIM:]
    q_rope = apply_rope(q_rope, cos[:, :, None, :], sin[:, :, None, :])
    kv = (rmsnorm(c_kv, kv_norm) @ w_ukv).reshape(B, S, MLA_HEADS, MLA_NOPE_DIM + MLA_V_DIM)
    k_nope, v = kv[..., :MLA_NOPE_DIM], kv[..., MLA_NOPE_DIM:]
    k_rope = apply_rope(k_rope, cos, sin)
    scale = (MLA_NOPE_DIM + MLA_ROPE_DIM) ** -0.5
    outs = []
    for blk in range(S // MLA_Q_BLOCK):
        s0 = blk * MLA_Q_BLOCK
        end = s0 + MLA_Q_BLOCK
        sc = (jnp.einsum('bqhd,bkhd->bhqk', q_nope[:, s0:end], k_nope[:, :end])
              + jnp.einsum('bqhr,bkr->bhqk', q_rope[:, s0:end], k_rope[:, :end]))
        sc = sc.astype(jnp.float32) * scale
        mask = (s0 + jnp.arange(MLA_Q_BLOCK))[:, None] >= jnp.arange(end)[None, :]
        p = jax.nn.softmax(jnp.where(mask, sc, -jnp.inf), axis=-1)
        outs.append(jnp.einsum('bhqk,bkhv->bqhv', p, v[:, :end]))
    return jnp.concatenate(outs, axis=1).reshape(B, S, MLA_HEADS * MLA_V_DIM)


def swiglu(h, w1, w3, w2):
    return (jax.nn.silu(h @ w1) * (h @ w3)) @ w2


def moe(h, router, w1, w3, w2):
    logits = (h @ router).astype(jnp.float32)
    top_vals, top_idx = lax.top_k(logits, TOP_K)
    top_w = jax.nn.softmax(top_vals, axis=-1)
    combine = jnp.sum(jax.nn.one_hot(top_idx, N_EXPERTS, dtype=jnp.float32) * top_w[..., None], axis=-2)
    out = jnp.zeros_like(h)
    for e in range(N_EXPERTS):
        out = out + combine[..., e:e + 1].astype(h.dtype) * swiglu(h, w1[e], w3[e], w2[e])
    return out


def setup_inputs(seed: int = 0) -> dict:
    key = jax.random.key(seed)
    ks = jax.random.split(key, 24)
    f32 = jnp.float32

    def nrm(k, shape, fan_in):
        return jax.random.normal(k, shape, f32) * (fan_in ** -0.5)

    def gain(k, shape):
        return 1.0 + 0.02 * jax.random.normal(k, shape, f32)

    x = jax.random.normal(ks[0], (BATCH, SEQ, D_MODEL), f32)
    offset = jax.random.randint(ks[1], (BATCH, 1), 0, 1024, dtype=jnp.int32)
    positions = offset + jnp.arange(SEQ, dtype=jnp.int32)[None, :]
    return {
        'x': x,
        'positions': positions,
        'ln_mix': gain(ks[2], (DEPTH, D_MODEL)),
        'w_in': nrm(ks[3], (DEPTH, D_MODEL, D_IN), D_MODEL),
        'ret_norm': gain(ks[4], (DEPTH, RET_HEADS * RET_V_DIM)),
        'hgrn_norm': gain(ks[5], (DEPTH, HGRN_HEADS * HGRN_V_DIM)),
        'hgrn_lb_logits': 0.1 * jax.random.normal(ks[6], (DEPTH, HGRN_HEADS * HGRN_EXPAND), f32),
        'mla_q_norm': gain(ks[7], (DEPTH, MLA_Q_RANK)),
        'mla_w_uq': nrm(ks[8], (DEPTH, MLA_Q_RANK, MLA_HEADS * (MLA_NOPE_DIM + MLA_ROPE_DIM)), MLA_Q_RANK),
        'mla_kv_norm': gain(ks[9], (DEPTH, MLA_KV_RANK)),
        'mla_w_ukv': nrm(ks[10], (DEPTH, MLA_KV_RANK, MLA_HEADS * (MLA_NOPE_DIM + MLA_V_DIM)), MLA_KV_RANK),
        'w_out': nrm(ks[11], (DEPTH, D_MODEL, D_MODEL), D_MODEL),
        'ln_ffn': gain(ks[12], (DEPTH, D_MODEL)),
        'ffn_w1': nrm(ks[13], (N_DENSE, D_MODEL, FFN_DIM), D_MODEL),
        'ffn_w3': nrm(ks[14], (N_DENSE, D_MODEL, FFN_DIM), D_MODEL),
        'ffn_w2': nrm(ks[15], (N_DENSE, FFN_DIM, D_MODEL), FFN_DIM),
        'moe_router': nrm(ks[16], (N_MOE, D_MODEL, N_EXPERTS), D_MODEL),
        'moe_w1': nrm(ks[17], (N_MOE, N_EXPERTS, D_MODEL, EXPERT_DIM), D_MODEL),
        'moe_w3': nrm(ks[18], (N_MOE, N_EXPERTS, D_MODEL, EXPERT_DIM), D_MODEL),
        'moe_w2': nrm(ks[19], (N_MOE, N_EXPERTS, EXPERT_DIM, D_MODEL), EXPERT_DIM),
        'final_norm': gain(ks[20], (D_MODEL,)),
    }


def reference(x, positions, ln_mix, w_in, ret_norm, hgrn_norm, hgrn_lb_logits, mla_q_norm, mla_w_uq,
              mla_kv_norm, mla_w_ukv, w_out, ln_ffn, ffn_w1, ffn_w3, ffn_w2, moe_router, moe_w1,
              moe_w3, moe_w2, final_norm):
    B, S = x.shape[0], x.shape[1]
    cos_r, sin_r = rope_cos_sin(positions, RET_QK_DIM)
    cos_r, sin_r = cos_r[:, :, None, :], sin_r[:, :, None, :]
    cos_m, sin_m = rope_cos_sin(positions, MLA_ROPE_DIM)
    lb_all = jnp.cumsum(jax.nn.softmax(hgrn_lb_logits.astype(jnp.float32), axis=0), axis=0)
    lb_all = lb_all - lb_all[:1]
    pts = split_points()
    for l in range(DEPTH):
        h = rmsnorm(x, ln_mix[l])
        p = (h @ w_in[l]).astype(jnp.float32)
        (rq, rk, rv, rg, hf, hq, hi, hog, cq, ckv, kr, gates) = jnp.split(p, pts, axis=-1)
        o_a = retention(rq, rk, rv, rg, cos_r, sin_r, ret_norm[l])
        o_b = hgrn2(hf, hq, hi, hog, lb_all[l], hgrn_norm[l])
        o_c = mla(cq, ckv, kr, cos_m, sin_m, mla_q_norm[l], mla_w_uq[l], mla_kv_norm[l], mla_w_ukv[l])
        gate = jax.nn.sigmoid(gates.reshape(B, S, N_BRANCHES, D_MODEL))
        y = gate[:, :, 0] * o_a + gate[:, :, 1] * o_b + gate[:, :, 2] * o_c
        x = x + y.astype(x.dtype) @ w_out[l]
        h2 = rmsnorm(x, ln_ffn[l])
        if l % 2 == 0:
            j = l // 2
            x = x + swiglu(h2, ffn_w1[j], ffn_w3[j], ffn_w2[j])
        else:
            j = l // 2
            x = x + moe(h2, moe_router[j], moe_w1[j], moe_w3[j], moe_w2[j])
    return rmsnorm(x, final_norm)
```

```python
import functools

import jax
import jax.numpy as jnp
from jax import lax
from jax.experimental import pallas as pl
from jax.experimental.pallas import tpu as pltpu

F32 = jnp.float32
BF16 = jnp.bfloat16

RET_HEADS = 16
RET_QK_DIM = 128
HGRN_HEADS = 16
HGRN_EXPAND = 128
HGRN_CHUNK = 32
MLA_HEADS = 32
MLA_NOPE_DIM = 128
MLA_ROPE_DIM = 64
MLA_V_DIM = 128
N_BRANCHES = 3
N_EXPERTS = 8
TOP_K = 2
ROPE_THETA = 10000.0
EPS = 1e-6

LANES = 128
VMEM_LIMIT_BYTES = 56 * 2 ** 20

ROW_TILE = 512
MM_BM = 1024
MM_BN = 512
RET_BLOCK = 256
HGRN_BLOCK = 256
ATTN_BLOCK = 512
MOE_TILE = 256
GATHER_TILE = 256


def _cparams(*sem):
    return pltpu.CompilerParams(dimension_semantics=sem, vmem_limit_bytes=VMEM_LIMIT_BYTES)


def _tile(n, preferred):
    for t in range(min(preferred, n) // LANES * LANES, 0, -LANES):
        if n % t == 0:
            return t
    return n


def _dot(a, b):
    return jnp.dot(a, b, preferred_element_type=F32)


def _dot_nt(a, b):
    return lax.dot_general(a, b, (((1,), (1,)), ((), ())), preferred_element_type=F32)


def _sigmoid(x):
    return 1.0 / (1.0 + jnp.exp(-x))


def _silu(x):
    return x * _sigmoid(x)


def _rmsnorm_kernel(x_ref, g_ref, o_ref):
    x = x_ref[...].astype(F32)
    ms = jnp.mean(x * x, axis=-1, keepdims=True)
    o_ref[...] = (x * lax.rsqrt(ms + EPS) * g_ref[...]).astype(o_ref.dtype)


def rmsnorm(x, g, out_dtype):
    m, d = x.shape
    rows = min(ROW_TILE, m)
    return pl.pallas_call(
        _rmsnorm_kernel,
        grid=(m // rows,),
        in_specs=[pl.BlockSpec((rows, d), lambda i: (i, 0)),
                  pl.BlockSpec((1, d), lambda i: (0, 0))],
        out_specs=pl.BlockSpec((rows, d), lambda i: (i, 0)),
        out_shape=jax.ShapeDtypeStruct((m, d), out_dtype),
        compiler_params=_cparams("parallel"),
        name="rmsnorm",
    )(x, g.reshape(1, d).astype(F32))


def _mm_kernel(*refs, act, has_res):
    a_ref, w_ref = refs[0], refs[1]
    o_ref = refs[-1]
    acc = _dot(a_ref[...], w_ref[...])
    if act == "sigmoid":
        acc = _sigmoid(acc)
    if has_res:
        acc = refs[2][...] + acc
    o_ref[...] = acc.astype(o_ref.dtype)


def matmul(a, w, out_dtype, act=None, res=None, bm=MM_BM, bn=MM_BN):
    m, k = a.shape
    n = w.shape[1]
    bm = _tile(m, bm)
    bn = _tile(n, bn)
    in_specs = [pl.BlockSpec((bm, k), lambda j, i: (i, 0)),
                pl.BlockSpec((k, bn), lambda j, i: (0, j))]
    args = [a, w]
    if res is not None:
        in_specs.append(pl.BlockSpec((bm, bn), lambda j, i: (i, j)))
        args.append(res)
    return pl.pallas_call(
        functools.partial(_mm_kernel, act=act, has_res=res is not None),
        grid=(n // bn, m // bm),
        in_specs=in_specs,
        out_specs=pl.BlockSpec((bm, bn), lambda j, i: (i, j)),
        out_shape=jax.ShapeDtypeStruct((m, n), out_dtype),
        compiler_params=_cparams("parallel", "parallel"),
        name="matmul",
    )(*args)


def _mm_acc_kernel(a_ref, w_ref, res_ref, o_ref, acc_ref):
    kk = pl.program_id(2)

    @pl.when(kk == 0)
    def _():
        acc_ref[...] = jnp.zeros_like(acc_ref)

    acc_ref[...] += _dot(a_ref[...], w_ref[...])

    @pl.when(kk == pl.num_programs(2) - 1)
    def _():
        o_ref[...] = (res_ref[...] + acc_ref[...]).astype(o_ref.dtype)


def matmul_acc_res(a, w, res, bk, bm=MM_BM, bn=MM_BN):
    m, k = a.shape
    n = w.shape[1]
    bm = _tile(m, bm)
    bn = _tile(n, bn)
    return pl.pallas_call(
        _mm_acc_kernel,
        grid=(n // bn, m // bm, k // bk),
        in_specs=[pl.BlockSpec((bm, bk), lambda j, i, kk: (i, kk)),
                  pl.BlockSpec((bk, bn), lambda j, i, kk: (kk, j)),
                  pl.BlockSpec((bm, bn), lambda j, i, kk: (i, j))],
        out_specs=pl.BlockSpec((bm, bn), lambda j, i, kk: (i, j)),
        out_shape=jax.ShapeDtypeStruct((m, n), res.dtype),
        scratch_shapes=[pltpu.VMEM((bm, bn), F32)],
        compiler_params=_cparams("parallel", "parallel", "arbitrary"),
        name="matmul_acc_res",
    )(a, w, res)


def _swiglu_up_kernel(a_ref, w1_ref, w3_ref, o_ref):
    a = a_ref[...]
    u = _dot(a, w1_ref[...])
    v = _dot(a, w3_ref[...])
    o_ref[...] = (_silu(u) * v).astype(o_ref.dtype)


def swiglu_up(a, w1, w3, bm=MM_BM, bn=MM_BN):
    m, k = a.shape
    n = w1.shape[1]
    bm = _tile(m, bm)
    bn = _tile(n, bn)
    return pl.pallas_call(
        _swiglu_up_kernel,
        grid=(n // bn, m // bm),
        in_specs=[pl.BlockSpec((bm, k), lambda j, i: (i, 0)),
                  pl.BlockSpec((k, bn), lambda j, i: (0, j)),
                  pl.BlockSpec((k, bn), lambda j, i: (0, j))],
        out_specs=pl.BlockSpec((bm, bn), lambda j, i: (i, j)),
        out_shape=jax.ShapeDtypeStruct((m, n), BF16),
        compiler_params=_cparams("parallel", "parallel"),
        name="swiglu_up",
    )(a, w1, w3)


def _head_norm_gate(o, gain, g):
    y = o * lax.rsqrt(jnp.mean(o * o, axis=-1, keepdims=True) + EPS) * gain
    return y * _silu(g.astype(F32))


def _retention_kernel(q_ref, k_ref, v_ref, g_ref, cos_ref, sin_ref, dmat_ref, qdec_ref, kdec_ref,
                      cdec_ref, gain_ref, o_ref, st_ref):
    @pl.when(pl.program_id(2) == 0)
    def _():
        st_ref[...] = jnp.zeros_like(st_ref)

    cos = cos_ref[...]
    sin = sin_ref[...]
    half = RET_QK_DIM // 2
    q = q_ref[...].astype(F32)
    k = k_ref[...].astype(F32)
    qr = q * cos + pltpu.roll(q, half, 1) * sin
    kr = (k * cos + pltpu.roll(k, half, 1) * sin) * (RET_QK_DIM ** -0.5)
    v = v_ref[...]
    scores = _dot_nt(qr.astype(BF16), kr.astype(BF16)) * dmat_ref[0]
    intra = _dot(scores.astype(BF16), v)
    st = st_ref[...]
    cross = _dot_nt((qr * qdec_ref[0]).astype(BF16), st.astype(BF16))
    v_t = v.astype(F32).T.astype(BF16)
    st_ref[...] = cdec_ref[0] * st + _dot(v_t, (kr * kdec_ref[0]).astype(BF16))
    o_ref[...] = _head_norm_gate(intra + cross, gain_ref[...], g_ref[...]).astype(o_ref.dtype)


def retention(r, cos2, sin2, norm_g, batch, seq):
    heads, dk = RET_HEADS, RET_QK_DIM
    dv = (r.shape[1] - 2 * heads * dk) // (2 * heads)
    blk = min(RET_BLOCK, seq)
    nblk = seq // blk
    log_gamma = jnp.log(1.0 - jnp.exp2(-5.0 - jnp.arange(heads, dtype=F32)))
    idx = jnp.arange(blk, dtype=F32)
    rel = idx[:, None] - idx[None, :]
    dmat = jnp.where(rel >= 0, jnp.exp(log_gamma[:, None, None] * jnp.maximum(rel, 0.0)), 0.0)
    qdec = jnp.broadcast_to(jnp.exp(log_gamma[:, None] * (idx + 1.0))[:, :, None], (heads, blk, dk))
    kdec = jnp.broadcast_to(jnp.exp(log_gamma[:, None] * (blk - 1.0 - idx))[:, :, None], (heads, blk, dk))
    cdec = jnp.broadcast_to(jnp.exp(log_gamma * blk)[:, None, None], (heads, 1, dk))
    v_off = 2 * heads * dk // dv
    g_off = v_off + heads
    row = lambda b, h, t: b * nblk + t
    return pl.pallas_call(
        _retention_kernel,
        grid=(batch, heads, nblk),
        in_specs=[pl.BlockSpec((blk, dk), lambda b, h, t: (row(b, h, t), h)),
                  pl.BlockSpec((blk, dk), lambda b, h, t: (row(b, h, t), heads + h)),
                  pl.BlockSpec((blk, dv), lambda b, h, t: (row(b, h, t), v_off + h)),
                  pl.BlockSpec((blk, dv), lambda b, h, t: (row(b, h, t), g_off + h)),
                  pl.BlockSpec((blk, dk), lambda b, h, t: (row(b, h, t), 0)),
                  pl.BlockSpec((blk, dk), lambda b, h, t: (row(b, h, t), 0)),
                  pl.BlockSpec((1, blk, blk), lambda b, h, t: (h, 0, 0)),
                  pl.BlockSpec((1, blk, dk), lambda b, h, t: (h, 0, 0)),
                  pl.BlockSpec((1, blk, dk), lambda b, h, t: (h, 0, 0)),
                  pl.BlockSpec((1, 1, dk), lambda b, h, t: (h, 0, 0)),
                  pl.BlockSpec((1, dv), lambda b, h, t: (0, h))],
        out_specs=pl.BlockSpec((blk, dv), lambda b, h, t: (row(b, h, t), h)),
        out_shape=jax.ShapeDtypeStruct((batch * seq, heads * dv), BF16),
        scratch_shapes=[pltpu.VMEM((dv, dk), F32)],
        compiler_params=_cparams("parallel", "parallel", "arbitrary"),
        name="retention",
    )(r, r, r, r, cos2, sin2, dmat, qdec, kdec, cdec, norm_g.reshape(1, -1).astype(F32))


def _hgrn_kernel(f_ref, q_ref, i_ref, og_ref, la_ref, l1_ref, oml_ref, gain_ref, o_ref, st_ref):
    @pl.when(pl.program_id(2) == 0)
    def _():
        st_ref[...] = jnp.zeros_like(st_ref)

    blk, e = f_ref.shape
    c = HGRN_CHUNK
    nch = blk // c
    z = f_ref[...]
    log_sig = jnp.minimum(z, 0.0) - jnp.log1p(jnp.exp(-jnp.abs(z)))
    la = la_ref[...]
    u = l1_ref[...] + log_sig
    log_f = jnp.maximum(la, u) + jnp.log1p(jnp.exp(-jnp.abs(la - u)))
    key = oml_ref[...] * (1.0 / (1.0 + jnp.exp(z)))

    row = lax.broadcasted_iota(jnp.int32, (blk, e), 0)
    row_in_chunk = row % c
    b = log_f
    shift = 1
    while shift < c:
        b = b + jnp.where(row_in_chunk >= shift, pltpu.roll(b, shift, 0), 0.0)
        shift *= 2
    b3 = b.reshape(nch, c, e)
    b_mid = jnp.broadcast_to(b3[:, c // 2 - 1:c // 2, :], (nch, c, e)).reshape(blk, e)
    b_end_rows = b3[:, c - 1:c, :]
    b_end = jnp.broadcast_to(b_end_rows, (nch, c, e)).reshape(blk, e)

    q = q_ref[...].astype(F32)
    v = i_ref[...]
    qf = (q * jnp.exp(b - b_mid)).astype(BF16)
    kf = (key * jnp.exp(b_mid - b)).astype(BF16)
    scores = _dot_nt(qf, kf)
    r2 = lax.broadcasted_iota(jnp.int32, (blk, blk), 0)
    c2 = lax.broadcasted_iota(jnp.int32, (blk, blk), 1)
    keep = (r2 // c == c2 // c) & (c2 <= r2)
    intra = _dot(jnp.where(keep, scores, 0.0).astype(BF16), v)

    q_in = (q * jnp.exp(b)).astype(BF16)
    k_out = (key * jnp.exp(b_end - b)).astype(BF16)
    chunk_of_row = row // c
    zero = jnp.zeros_like(q_in)
    k_wide = jnp.concatenate([jnp.where(chunk_of_row == j, k_out, zero) for j in range(nch)], axis=1)
    q_wide = jnp.concatenate([jnp.where(chunk_of_row == j, q_in, zero) for j in range(nch)], axis=1)
    v_t = v.astype(F32).T.astype(BF16)
    upd = _dot(v_t, k_wide)
    dec = jnp.exp(b_end_rows)
    st = st_ref[...]
    states = []
    for j in range(nch):
        states.append(st.astype(BF16))
        st = st * dec[j] + upd[:, j * e:(j + 1) * e]
    st_ref[...] = st
    inter = _dot_nt(q_wide, jnp.concatenate(states, axis=1))
    o_ref[...] = _head_norm_gate(intra + inter, gain_ref[...], og_ref[...]).astype(o_ref.dtype)


def hgrn2(hf, hr, lb, norm_g, batch, seq):
    heads, e = HGRN_HEADS, HGRN_EXPAND
    dv = (hr.shape[1] - heads * e) // (2 * heads)
    blk = min(HGRN_BLOCK, seq)
    nblk = seq // blk
    lb = lb.astype(F32).reshape(1, -1)
    la, l1, oml = jnp.log(lb), jnp.log1p(-lb), 1.0 - lb
    i_off = heads * e // dv
    og_off = i_off + heads
    row = lambda b, h, t: b * nblk + t
    vec = pl.BlockSpec((1, e), lambda b, h, t: (0, h))
    return pl.pallas_call(
        _hgrn_kernel,
        grid=(batch, heads, nblk),
        in_specs=[pl.BlockSpec((blk, e), lambda b, h, t: (row(b, h, t), h)),
                  pl.BlockSpec((blk, e), lambda b, h, t: (row(b, h, t), h)),
                  pl.BlockSpec((blk, dv), lambda b, h, t: (row(b, h, t), i_off + h)),
                  pl.BlockSpec((blk, dv), lambda b, h, t: (row(b, h, t), og_off + h)),
                  vec, vec, vec,
                  pl.BlockSpec((1, dv), lambda b, h, t: (0, h))],
        out_specs=pl.BlockSpec((blk, dv), lambda b, h, t: (row(b, h, t), h)),
        out_shape=jax.ShapeDtypeStruct((batch * seq, heads * dv), BF16),
        scratch_shapes=[pltpu.VMEM((dv, e), F32)],
        compiler_params=_cparams("parallel", "parallel", "arbitrary"),
        name="hgrn2",
    )(hf, hr, hr, hr, la, l1, oml, norm_g.reshape(1, -1).astype(F32))


def _mla_prep_kernel(p_ref, qg_ref, kvg_ref, cm_ref, sm_ref, cq_ref, ckv_ref, kr_ref, *, q_rank, kv_rank):
    p = p_ref[...]

    def norm(x, g):
        return x * lax.rsqrt(jnp.mean(x * x, axis=-1, keepdims=True) + EPS) * g

    cq_ref[...] = norm(p[:, :q_rank], qg_ref[...]).astype(cq_ref.dtype)
    ckv_ref[...] = norm(p[:, q_rank:q_rank + kv_rank], kvg_ref[...]).astype(ckv_ref.dtype)
    kr = p[:, q_rank + kv_rank:]
    kr_ref[...] = (kr * cm_ref[...] + pltpu.roll(kr, MLA_ROPE_DIM, 1) * sm_ref[...]).astype(kr_ref.dtype)


def mla_prep(pm, q_norm, kv_norm, cm, sm):
    m = pm.shape[0]
    q_rank, kv_rank = q_norm.shape[0], kv_norm.shape[0]
    rows = min(ROW_TILE, m)
    full = lambda w: pl.BlockSpec((rows, w), lambda i: (i, 0))
    return pl.pallas_call(
        functools.partial(_mla_prep_kernel, q_rank=q_rank, kv_rank=kv_rank),
        grid=(m // rows,),
        in_specs=[full(pm.shape[1]),
                  pl.BlockSpec((1, q_rank), lambda i: (0, 0)),
                  pl.BlockSpec((1, kv_rank), lambda i: (0, 0)),
                  full(LANES), full(LANES)],
        out_specs=[full(q_rank), full(kv_rank), full(LANES)],
        out_shape=[jax.ShapeDtypeStruct((m, q_rank), BF16),
                   jax.ShapeDtypeStruct((m, kv_rank), BF16),
                   jax.ShapeDtypeStruct((m, LANES), BF16)],
        compiler_params=_cparams("parallel"),
        name="mla_prep",
    )(pm, q_norm.reshape(1, -1).astype(F32), kv_norm.reshape(1, -1).astype(F32), cm, sm)


def _attn_kernel(q_ref, k_ref, v_ref, kr_ref, cm_ref, sm_ref, o_ref, *, scale):
    i = pl.program_id(2)
    tq = q_ref.shape[0]
    qb = q_ref[...].astype(F32)
    qn = qb[:, :MLA_NOPE_DIM]
    qr = qb[:, MLA_NOPE_DIM:]
    qr = qr * cm_ref[...] + pltpu.roll(qr, MLA_ROPE_DIM, 1) * sm_ref[...]
    qf = (jnp.concatenate([qn, qr], axis=1) * scale).astype(BF16)

    def step(j, carry, masked):
        m, l, acc = carry
        off = pl.multiple_of(j * tq, tq)
        kb = jnp.concatenate([k_ref[pl.ds(off, tq), :], kr_ref[pl.ds(off, tq), :]], axis=1)
        s = _dot_nt(qf, kb)
        if masked:
            r2 = lax.broadcasted_iota(jnp.int32, s.shape, 0)
            c2 = lax.broadcasted_iota(jnp.int32, s.shape, 1)
            s = jnp.where(c2 <= r2, s, -jnp.inf)
        m_new = jnp.maximum(m, jnp.max(s, axis=-1, keepdims=True))
        p = jnp.exp(s - m_new)
        alpha = jnp.exp(m - m_new)
        l = alpha * l + jnp.sum(p, axis=-1, keepdims=True)
        acc = alpha * acc + _dot(p.astype(BF16), v_ref[pl.ds(off, tq), :])
        return m_new, l, acc

    init = (jnp.full((tq, 1), -jnp.inf, F32), jnp.zeros((tq, 1), F32), jnp.zeros((tq, MLA_V_DIM), F32))
    carry = lax.fori_loop(0, i, lambda j, cr: step(j, cr, False), init)
    _, l, acc = step(i, carry, True)
    o_ref[...] = (acc / l).astype(o_ref.dtype)


def mla_attention(q, kv, kr, cm, sm, batch, seq):
    heads = MLA_HEADS
    tq = min(ATTN_BLOCK, seq)
    nq = seq // tq
    scale = (MLA_NOPE_DIM + MLA_ROPE_DIM) ** -0.5
    qw = MLA_NOPE_DIM + 2 * MLA_ROPE_DIM
    return pl.pallas_call(
        functools.partial(_attn_kernel, scale=scale),
        grid=(batch, heads, nq),
        in_specs=[pl.BlockSpec((tq, qw), lambda b, h, i: (b * nq + i, h)),
                  pl.BlockSpec((seq, MLA_NOPE_DIM), lambda b, h, i: (b, 2 * h)),
                  pl.BlockSpec((seq, MLA_V_DIM), lambda b, h, i: (b, 2 * h + 1)),
                  pl.BlockSpec((seq, LANES), lambda b, h, i: (b, 0)),
                  pl.BlockSpec((tq, LANES), lambda b, h, i: (b * nq + i, 0)),
                  pl.BlockSpec((tq, LANES), lambda b, h, i: (b * nq + i, 0))],
        out_specs=pl.BlockSpec((tq, MLA_V_DIM), lambda b, h, i: (b * nq + i, h)),
        out_shape=jax.ShapeDtypeStruct((batch * seq, heads * MLA_V_DIM), BF16),
        compiler_params=_cparams("parallel", "parallel", "arbitrary"),
        name="mla_attention",
    )(q, kv, kv, kr, cm, sm)


def _merge_kernel(ga_ref, gb_ref, gc_ref, a_ref, b_ref, c_ref, o_ref):
    y = (ga_ref[...].astype(F32) * a_ref[...].astype(F32)
         + gb_ref[...].astype(F32) * b_ref[...].astype(F32)
         + gc_ref[...].astype(F32) * c_ref[...].astype(F32))
    o_ref[...] = y.astype(o_ref.dtype)


def merge(gates, oa, ob, oc):
    m, d = oa.shape
    rows = min(ROW_TILE, m)
    cols = min(1024, d)
    nc = d // cols
    blk = lambda off: pl.BlockSpec((rows, cols), lambda i, j: (i, off * nc + j))
    return pl.pallas_call(
        _merge_kernel,
        grid=(m // rows, nc),
        in_specs=[blk(0), blk(1), blk(2), blk(0), blk(0), blk(0)],
        out_specs=blk(0),
        out_shape=jax.ShapeDtypeStruct((m, d), BF16),
        compiler_params=_cparams("parallel", "parallel"),
        name="merge",
    )(gates, gates, gates, oa, ob, oc)


def _router_kernel(x_ref, g_ref, r_ref, h_ref, idx_ref, w_ref):
    x = x_ref[...]
    h = x * lax.rsqrt(jnp.mean(x * x, axis=-1, keepdims=True) + EPS) * g_ref[...]
    h_ref[...] = h
    logits = jnp.dot(h, r_ref[...], preferred_element_type=F32, precision=lax.Precision.HIGHEST)
    lane = lax.broadcasted_iota(jnp.int32, logits.shape, 1)
    lg = jnp.where(lane < N_EXPERTS, logits, -jnp.inf)
    m1 = jnp.max(lg, axis=-1, keepdims=True)
    i1 = jnp.min(jnp.where(lg == m1, lane, LANES), axis=-1, keepdims=True)
    lg2 = jnp.where(lane == i1, -jnp.inf, lg)
    m2 = jnp.max(lg2, axis=-1, keepdims=True)
    i2 = jnp.min(jnp.where(lg2 == m2, lane, LANES), axis=-1, keepdims=True)
    e2 = jnp.exp(m2 - m1)
    w1 = 1.0 / (1.0 + e2)
    w2 = e2 / (1.0 + e2)
    idx_ref[...] = jnp.where(lane == 0, i1, i2)
    w_ref[...] = jnp.where(lane == 0, w1, w2)


def moe_router(x, g, router):
    m, d = x.shape
    rows = min(ROW_TILE, m)
    r_pad = jnp.zeros((d, LANES), F32).at[:, :N_EXPERTS].set(router.astype(F32))
    h, idx, w = pl.pallas_call(
        _router_kernel,
        grid=(m // rows,),
        in_specs=[pl.BlockSpec((rows, d), lambda i: (i, 0)),
                  pl.BlockSpec((1, d), lambda i: (0, 0)),
                  pl.BlockSpec((d, LANES), lambda i: (0, 0))],
        out_specs=[pl.BlockSpec((rows, d), lambda i: (i, 0)),
                   pl.BlockSpec((rows, LANES), lambda i: (i, 0)),
                   pl.BlockSpec((rows, LANES), lambda i: (i, 0))],
        out_shape=[jax.ShapeDtypeStruct((m, d), F32),
                   jax.ShapeDtypeStruct((m, LANES), jnp.int32),
                   jax.ShapeDtypeStruct((m, LANES), F32)],
        compiler_params=_cparams("parallel"),
        name="moe_router",
    )(x, g.reshape(1, d).astype(F32), r_pad)
    return h, idx[:, :TOP_K], w[:, :TOP_K]


def _routing_tables(idx, wts, tile):
    t = idx.shape[0]
    pairs = t * TOP_K
    e = idx.reshape(pairs)
    onehot = (e[:, None] == jnp.arange(N_EXPERTS, dtype=jnp.int32)[None, :]).astype(jnp.int32)
    csum = jnp.cumsum(onehot, axis=0)
    rank = jnp.sum(csum * onehot, axis=1) - 1
    counts = csum[-1]
    padded = ((counts + tile - 1) // tile) * tile
    ends = jnp.cumsum(padded)
    starts = ends - padded
    pos = (jnp.sum(onehot * starts[None, :], axis=1) + rank).astype(jnp.int32)
    rows = pairs + N_EXPERTS * tile
    row_tok = jnp.zeros((rows,), jnp.int32).at[pos].set(jnp.arange(pairs, dtype=jnp.int32) // TOP_K)
    row_w = jnp.zeros((rows,), F32).at[pos].set(wts.reshape(pairs))
    tile_start = jnp.arange(rows // tile, dtype=jnp.int32) * tile
    tile_e = jnp.minimum(jnp.sum((tile_start[:, None] >= ends[None, :]).astype(jnp.int32), axis=1),
                         N_EXPERTS - 1).astype(jnp.int32)
    n_used = (ends[-1] // tile).astype(jnp.int32).reshape(1)
    return pos, row_tok, row_w.reshape(rows, 1), tile_e, n_used


def _row_copy(src_hbm, src_row, dst, dst_row, sem):
    return pltpu.make_async_copy(src_hbm.at[pl.ds(src_row, 1), :], dst.at[pl.ds(dst_row, 1), :], sem)


def _gather_kernel(tok_ref, h_hbm, o_ref, buf, sem):
    tg = buf.shape[0]
    base = pl.program_id(0) * tg

    def start(r, carry):
        _row_copy(h_hbm, tok_ref[base + r], buf, r, sem).start()
        return carry

    def wait(r, carry):
        _row_copy(h_hbm, 0, buf, r, sem).wait()
        return carry

    lax.fori_loop(0, tg, start, 0)
    lax.fori_loop(0, tg, wait, 0)
    o_ref[...] = buf[...].astype(o_ref.dtype)


def moe_gather(h, row_tok):
    rows = row_tok.shape[0]
    d = h.shape[1]
    tg = GATHER_TILE
    return pl.pallas_call(
        _gather_kernel,
        grid_spec=pltpu.PrefetchScalarGridSpec(
            num_scalar_prefetch=1,
            grid=(rows // tg,),
            in_specs=[pl.BlockSpec(memory_space=pl.ANY)],
            out_specs=pl.BlockSpec((tg, d), lambda i, tok: (i, 0)),
            scratch_shapes=[pltpu.VMEM((tg, d), h.dtype), pltpu.SemaphoreType.DMA(())]),
        out_shape=jax.ShapeDtypeStruct((rows, d), BF16),
        compiler_params=_cparams("arbitrary"),
        name="moe_gather",
    )(row_tok, h)


def _moe_up_kernel(te_ref, nu_ref, a_ref, w1_ref, w3_ref, o_ref):
    used = pl.program_id(1) < nu_ref[0]

    @pl.when(used)
    def _():
        a = a_ref[...]
        o_ref[...] = (_silu(_dot(a, w1_ref[...])) * _dot(a, w3_ref[...])).astype(o_ref.dtype)

    @pl.when(jnp.logical_not(used))
    def _():
        o_ref[...] = jnp.zeros_like(o_ref)


def moe_up(xs, w1, w3, tile_e, n_used, bn):
    rows, d = xs.shape
    n = w1.shape[2]
    tm = MOE_TILE
    return pl.pallas_call(
        _moe_up_kernel,
        grid_spec=pltpu.PrefetchScalarGridSpec(
            num_scalar_prefetch=2,
            grid=(n // bn, rows // tm),
            in_specs=[pl.BlockSpec((tm, d), lambda j, i, te, nu: (i, 0)),
                      pl.BlockSpec((None, d, bn), lambda j, i, te, nu: (te[i], 0, j)),
                      pl.BlockSpec((None, d, bn), lambda j, i, te, nu: (te[i], 0, j))],
            out_specs=pl.BlockSpec((tm, bn), lambda j, i, te, nu: (i, j))),
        out_shape=jax.ShapeDtypeStruct((rows, n), BF16),
        compiler_params=_cparams("parallel", "arbitrary"),
        name="moe_up",
    )(tile_e, n_used, xs, w1, w3)


def _moe_down_kernel(te_ref, nu_ref, a_ref, w_ref, rw_ref, o_ref):
    used = pl.program_id(1) < nu_ref[0]

    @pl.when(used)
    def _():
        o_ref[...] = (rw_ref[...] * _dot(a_ref[...], w_ref[...])).astype(o_ref.dtype)

    @pl.when(jnp.logical_not(used))
    def _():
        o_ref[...] = jnp.zeros_like(o_ref)


def moe_down(g, w2, row_w, tile_e, n_used, bn):
    rows, k = g.shape
    n = w2.shape[2]
    tm = MOE_TILE
    return pl.pallas_call(
        _moe_down_kernel,
        grid_spec=pltpu.PrefetchScalarGridSpec(
            num_scalar_prefetch=2,
            grid=(n // bn, rows // tm),
            in_specs=[pl.BlockSpec((tm, k), lambda j, i, te, nu: (i, 0)),
                      pl.BlockSpec((None, k, bn), lambda j, i, te, nu: (te[i], 0, j)),
                      pl.BlockSpec((tm, 1), lambda j, i, te, nu: (i, 0))],
            out_specs=pl.BlockSpec((tm, bn), lambda j, i, te, nu: (i, j))),
        out_shape=jax.ShapeDtypeStruct((rows, n), F32),
        compiler_params=_cparams("parallel", "arbitrary"),
        name="moe_down",
    )(tile_e, n_used, g, w2, row_w)


def _combine_kernel(pos_ref, x_ref, y_hbm, g_ref, o_ref, buf, sem, *, apply_norm):
    tb = x_ref.shape[0]
    base = pl.program_id(0) * tb

    def start(r, carry):
        for s in range(TOP_K):
            _row_copy(y_hbm, pos_ref[(base + r) * TOP_K + s], buf.at[s], r, sem).start()
        return carry

    def wait(r, carry):
        for s in range(TOP_K):
            _row_copy(y_hbm, 0, buf.at[s], r, sem).wait()
        return carry

    lax.fori_loop(0, tb, start, 0)
    lax.fori_loop(0, tb, wait, 0)
    x = x_ref[...]
    for s in range(TOP_K):
        x = x + buf[s]
    if apply_norm:
        x = x * lax.rsqrt(jnp.mean(x * x, axis=-1, keepdims=True) + EPS) * g_ref[...]
    o_ref[...] = x.astype(o_ref.dtype)


def moe_combine(x, y, pos, final_g, apply_norm):
    m, d = x.shape
    tb = GATHER_TILE
    return pl.pallas_call(
        functools.partial(_combine_kernel, apply_norm=apply_norm),
        grid_spec=pltpu.PrefetchScalarGridSpec(
            num_scalar_prefetch=1,
            grid=(m // tb,),
            in_specs=[pl.BlockSpec((tb, d), lambda i, p: (i, 0)),
                      pl.BlockSpec(memory_space=pl.ANY),
                      pl.BlockSpec((1, d), lambda i, p: (0, 0))],
            out_specs=pl.BlockSpec((tb, d), lambda i, p: (i, 0)),
            scratch_shapes=[pltpu.VMEM((TOP_K, tb, d), y.dtype), pltpu.SemaphoreType.DMA(())]),
        out_shape=jax.ShapeDtypeStruct((m, d), x.dtype),
        compiler_params=_cparams("arbitrary"),
        name="moe_combine_norm",
    )(pos, x, y, final_g.reshape(1, d).astype(F32))


def _rope_tables(positions, dim, pad_to):
    inv = ROPE_THETA ** (-jnp.arange(0, dim, 2, dtype=F32) / dim)
    ang = positions.astype(F32).reshape(-1)[:, None] * inv
    cos, sin = jnp.cos(ang), jnp.sin(ang)
    pad = jnp.zeros((ang.shape[0], pad_to - dim), F32)
    return (jnp.concatenate([cos, cos, pad], axis=1), jnp.concatenate([-sin, sin, pad], axis=1))


def _swap_halves(w):
    half = w.shape[-1] // 2
    return jnp.concatenate([w[..., half:], w[..., :half]], axis=-1)


def kernel(x, positions, ln_mix, w_in, ret_norm, hgrn_norm, hgrn_lb_logits, mla_q_norm, mla_w_uq,
           mla_kv_norm, mla_w_ukv, w_out, ln_ffn, ffn_w1, ffn_w3, ffn_w2, moe_router_w, moe_w1,
           moe_w3, moe_w2, final_norm):
    batch, seq, d_model = x.shape
    depth = w_in.shape[0]
    q_rank, kv_rank = mla_q_norm.shape[1], mla_kv_norm.shape[1]
    ret_w = 2 * RET_HEADS * RET_QK_DIM + 2 * d_model
    hg_e = HGRN_HEADS * HGRN_EXPAND
    hg_w = 2 * hg_e + 2 * d_model
    mla_w = q_rank + kv_rank + MLA_ROPE_DIM

    cos_r, sin_r = _rope_tables(positions, RET_QK_DIM, RET_QK_DIM)
    cm, sm = _rope_tables(positions, MLA_ROPE_DIM, LANES)
    lb_all = jnp.cumsum(jax.nn.softmax(hgrn_lb_logits.astype(F32), axis=0), axis=0)
    lb_all = lb_all - lb_all[:1]

    xf = x.reshape(batch * seq, d_model)
    for l in range(depth):
        last = l == depth - 1
        wl = w_in[l]
        w_ret = wl[:, :ret_w].astype(BF16)
        w_hf = wl[:, ret_w:ret_w + hg_e].astype(BF16)
        w_hr = wl[:, ret_w + hg_e:ret_w + hg_w].astype(BF16)
        o = ret_w + hg_w
        w_kr = wl[:, o + q_rank + kv_rank:o + mla_w]
        w_mla = jnp.concatenate([wl[:, o:o + mla_w], _swap_halves(w_kr)], axis=1).astype(BF16)
        w_gate = wl[:, o + mla_w:].astype(BF16)

        h = rmsnorm(xf, ln_mix[l], BF16)
        r = matmul(h, w_ret, BF16)
        hf = matmul(h, w_hf, F32)
        hr = matmul(h, w_hr, BF16)
        pm = matmul(h, w_mla, F32, bn=w_mla.shape[1])
        gates = matmul(h, w_gate, BF16, act="sigmoid")

        o_a = retention(r, cos_r, sin_r, ret_norm[l], batch, seq)
        o_b = hgrn2(hf, hr, lb_all[l], hgrn_norm[l], batch, seq)

        cqn, ckvn, kr = mla_prep(pm, mla_q_norm[l], mla_kv_norm[l], cm, sm)
        wq = mla_w_uq[l].reshape(q_rank, MLA_HEADS, MLA_NOPE_DIM + MLA_ROPE_DIM)
        wq_rope = wq[..., MLA_NOPE_DIM:]
        wq = jnp.concatenate([wq, _swap_halves(wq_rope)], axis=-1).reshape(q_rank, -1).astype(BF16)
        q = matmul(cqn, wq, BF16)
        kv = matmul(ckvn, mla_w_ukv[l].astype(BF16), BF16)
        o_c = mla_attention(q, kv, kr, cm, sm, batch, seq)

        y = merge(gates, o_a, o_b, o_c)
        xf = matmul(y, w_out[l].astype(BF16), F32, res=xf)

        if l % 2 == 0:
            j = l // 2
            h2 = rmsnorm(xf, ln_ffn[l], BF16)
            g = swiglu_up(h2, ffn_w1[j].astype(BF16), ffn_w3[j].astype(BF16))
            ffn_dim = g.shape[1]
            bk = ffn_dim // 7 if ffn_dim % (7 * 256) == 0 else ffn_dim
            xf = matmul_acc_res(g, ffn_w2[j].astype(BF16), xf, bk)
            if last:
                xf = rmsnorm(xf, final_norm, x.dtype)
        else:
            j = l // 2
            h2, idx, wts = moe_router(xf, ln_ffn[l], moe_router_w[j])
            pos, row_tok, row_w, tile_e, n_used = _routing_tables(idx, wts, MOE_TILE)
            xs = moe_gather(h2, row_tok)
            ed = moe_w1.shape[-1]
            bn_up = ed // 4 if ed % (4 * LANES) == 0 else ed
            g = moe_up(xs, moe_w1[j].astype(BF16), moe_w3[j].astype(BF16), tile_e, n_used, bn_up)
            yrows = moe_down(g, moe_w2[j].astype(BF16), row_w, tile_e, n_used, min(1024, d_model))
            xf = moe_combine(xf, yrows, pos, final_norm, apply_norm=last)
    return xf.reshape(batch, seq, d_model)
```

```python
import functools

import jax
import jax.numpy as jnp
from jax import lax
from jax.experimental import pallas as pl
from jax.experimental.pallas import tpu as pltpu

F32 = jnp.float32
BF16 = jnp.bfloat16

RET_HEADS = 16
RET_QK_DIM = 128
HGRN_HEADS = 16
HGRN_EXPAND = 128
HGRN_CHUNK = 32
MLA_HEADS = 32
MLA_NOPE_DIM = 128
MLA_ROPE_DIM = 64
MLA_V_DIM = 128
N_BRANCHES = 3
N_EXPERTS = 8
TOP_K = 2
ROPE_THETA = 10000.0
EPS = 1e-6
LOG2_E = 1.4426950408889634

LANES = 128
VMEM_LIMIT_BYTES = 56 * 2 ** 20

ROW_TILE = 512
MM_BM = 1024
MM_BN = 512
RET_BLOCK = 512
HGRN_BLOCK = 256
ATTN_BLOCK = 512
ATTN_KV_BLOCK = 512
ATTN_HEADS_PER_STEP = 2
MOE_TILE = 256
MOE_UP_BN = 512
MOE_DOWN_BN = 1024
GATHER_TILE = 256


def _cparams(*sem):
    return pltpu.CompilerParams(dimension_semantics=sem, vmem_limit_bytes=VMEM_LIMIT_BYTES)


def _tile(n, preferred):
    for t in range(min(preferred, n) // LANES * LANES, 0, -LANES):
        if n % t == 0:
            return t
    return n


def _dot(a, b):
    return jnp.dot(a, b, preferred_element_type=F32)


def _dot_nt(a, b):
    return lax.dot_general(a, b, (((1,), (1,)), ((), ())), preferred_element_type=F32)


def _sigmoid(x):
    return 1.0 / (1.0 + jnp.exp(-x))


def _silu(x):
    return x * _sigmoid(x)


def _rmsnorm_kernel(x_ref, g_ref, o_ref):
    x = x_ref[...].astype(F32)
    ms = jnp.mean(x * x, axis=-1, keepdims=True)
    o_ref[...] = (x * lax.rsqrt(ms + EPS) * g_ref[...]).astype(o_ref.dtype)


def rmsnorm(x, g, out_dtype):
    m, d = x.shape
    rows = min(ROW_TILE, m)
    return pl.pallas_call(
        _rmsnorm_kernel,
        grid=(m // rows,),
        in_specs=[pl.BlockSpec((rows, d), lambda i: (i, 0)),
                  pl.BlockSpec((1, d), lambda i: (0, 0))],
        out_specs=pl.BlockSpec((rows, d), lambda i: (i, 0)),
        out_shape=jax.ShapeDtypeStruct((m, d), out_dtype),
        compiler_params=_cparams("parallel"),
        name="rmsnorm",
    )(x, g.reshape(1, d).astype(F32))


def _mm_kernel(*refs, act, has_res):
    a_ref, w_ref = refs[0], refs[1]
    o_ref = refs[-1]
    acc = _dot(a_ref[...], w_ref[...])
    if act == "sigmoid":
        acc = _sigmoid(acc)
    if has_res:
        acc = refs[2][...] + acc
    o_ref[...] = acc.astype(o_ref.dtype)


def matmul(a, w, out_dtype, act=None, res=None, bm=MM_BM, bn=MM_BN):
    m, k = a.shape
    n = w.shape[1]
    bm = _tile(m, bm)
    bn = _tile(n, bn)
    in_specs = [pl.BlockSpec((bm, k), lambda j, i: (i, 0)),
                pl.BlockSpec((k, bn), lambda j, i: (0, j))]
    args = [a, w]
    if res is not None:
        in_specs.append(pl.BlockSpec((bm, bn), lambda j, i: (i, j)))
        args.append(res)
    return pl.pallas_call(
        functools.partial(_mm_kernel, act=act, has_res=res is not None),
        grid=(n // bn, m // bm),
        in_specs=in_specs,
        out_specs=pl.BlockSpec((bm, bn), lambda j, i: (i, j)),
        out_shape=jax.ShapeDtypeStruct((m, n), out_dtype),
        compiler_params=_cparams("parallel", "parallel"),
        name="matmul",
    )(*args)


def _staged_mm_kernel(*refs, n_w, layer, col0, bn, epilogue, has_res):
    a_ref = refs[0]
    w_hbm = refs[1:1 + n_w]
    res_ref = refs[1 + n_w] if has_res else None
    o_ref = refs[1 + n_w + int(has_res)]
    stage = refs[-1 - 2 * n_w:-1 - n_w]
    wb = refs[-1 - n_w:-1]
    sem = refs[-1]
    j = pl.program_id(0)
    i = pl.program_id(1)

    def copy(jj, t):
        cols = pl.ds(pl.multiple_of(col0 + jj * bn, LANES), bn)
        return pltpu.make_async_copy(w_hbm[t].at[layer, :, cols], stage[t], sem.at[t])

    @pl.when(i == 0)
    def _():
        @pl.when(j == 0)
        def _():
            for t in range(n_w):
                copy(0, t).start()

        for t in range(n_w):
            copy(j, t).wait()
            wb[t][...] = stage[t][...].astype(BF16)

        @pl.when(j + 1 < pl.num_programs(0))
        def _():
            for t in range(n_w):
                copy(j + 1, t).start()

    a = a_ref[...]
    acc = _dot(a, wb[0][...])
    if epilogue == "swiglu":
        acc = _silu(acc) * _dot(a, wb[1][...])
    elif epilogue == "sigmoid":
        acc = _sigmoid(acc)
    if has_res:
        acc = res_ref[...] + acc
    o_ref[...] = acc.astype(o_ref.dtype)


def staged_matmul(a, ws, layer, col0, n, out_dtype, epilogue=None, res=None, bm=MM_BM, bn=MM_BN):
    m, k = a.shape
    n_w = len(ws)
    bm = _tile(m, bm)
    bn = _tile(n, bn)
    assert col0 % LANES == 0 and n % bn == 0
    in_specs = [pl.BlockSpec((bm, k), lambda j, i: (i, 0))]
    in_specs += [pl.BlockSpec(memory_space=pl.ANY)] * n_w
    args = [a, *ws]
    if res is not None:
        in_specs.append(pl.BlockSpec((bm, bn), lambda j, i: (i, j)))
        args.append(res)
    scratch = ([pltpu.VMEM((k, bn), F32)] * n_w + [pltpu.VMEM((k, bn), BF16)] * n_w
               + [pltpu.SemaphoreType.DMA((n_w,))])
    return pl.pallas_call(
        functools.partial(_staged_mm_kernel, n_w=n_w, layer=layer, col0=col0, bn=bn,
                          epilogue=epilogue, has_res=res is not None),
        grid=(n // bn, m // bm),
        in_specs=in_specs,
        out_specs=pl.BlockSpec((bm, bn), lambda j, i: (i, j)),
        out_shape=jax.ShapeDtypeStruct((m, n), out_dtype),
        scratch_shapes=scratch,
        compiler_params=_cparams("arbitrary", "arbitrary"),
        name="staged_matmul",
    )(*args)


def _mm_acc_kernel(a_ref, w_ref, res_ref, o_ref, acc_ref):
    kk = pl.program_id(2)

    @pl.when(kk == 0)
    def _():
        acc_ref[...] = jnp.zeros_like(acc_ref)

    acc_ref[...] += _dot(a_ref[...], w_ref[...])

    @pl.when(kk == pl.num_programs(2) - 1)
    def _():
        o_ref[...] = (res_ref[...] + acc_ref[...]).astype(o_ref.dtype)


def matmul_acc_res(a, w, res, bk, bm=MM_BM, bn=MM_BN):
    m, k = a.shape
    n = w.shape[1]
    bm = _tile(m, bm)
    bn = _tile(n, bn)
    return pl.pallas_call(
        _mm_acc_kernel,
        grid=(n // bn, m // bm, k // bk),
        in_specs=[pl.BlockSpec((bm, bk), lambda j, i, kk: (i, kk)),
                  pl.BlockSpec((bk, bn), lambda j, i, kk: (kk, j)),
                  pl.BlockSpec((bm, bn), lambda j, i, kk: (i, j))],
        out_specs=pl.BlockSpec((bm, bn), lambda j, i, kk: (i, j)),
        out_shape=jax.ShapeDtypeStruct((m, n), res.dtype),
        scratch_shapes=[pltpu.VMEM((bm, bn), F32)],
        compiler_params=_cparams("parallel", "parallel", "arbitrary"),
        name="matmul_acc_res",
    )(a, w, res)


def _head_norm_gate(o, gain, g):
    y = o * lax.rsqrt(jnp.mean(o * o, axis=-1, keepdims=True) + EPS) * gain
    return y * _silu(g.astype(F32))


def _retention_kernel(q_ref, k_ref, v_ref, g_ref, cos_ref, sin_ref, dmat_ref, qdec_ref, kdec_ref,
                      cdec_ref, gain_ref, o_ref, st_ref):
    @pl.when(pl.program_id(2) == 0)
    def _():
        st_ref[...] = jnp.zeros_like(st_ref)

    cos = cos_ref[...]
    sin = sin_ref[...]
    half = RET_QK_DIM // 2
    q = q_ref[...].astype(F32)
    k = k_ref[...].astype(F32)
    qr = q * cos + pltpu.roll(q, half, 1) * sin
    kr = (k * cos + pltpu.roll(k, half, 1) * sin) * (RET_QK_DIM ** -0.5)
    v = v_ref[...]
    scores = _dot_nt(qr.astype(BF16), kr.astype(BF16)) * dmat_ref[0]
    intra = _dot(scores.astype(BF16), v)
    st = st_ref[...]
    cross = _dot_nt((qr * qdec_ref[0]).astype(BF16), st.astype(BF16))
    v_t = v.astype(F32).T.astype(BF16)
    st_ref[...] = cdec_ref[0] * st + _dot(v_t, (kr * kdec_ref[0]).astype(BF16))
    o_ref[...] = _head_norm_gate(intra + cross, gain_ref[...], g_ref[...]).astype(o_ref.dtype)


def retention(r, cos2, sin2, norm_g, batch, seq):
    heads, dk = RET_HEADS, RET_QK_DIM
    dv = (r.shape[1] - 2 * heads * dk) // (2 * heads)
    blk = min(RET_BLOCK, seq)
    nblk = seq // blk
    log_gamma = jnp.log(1.0 - jnp.exp2(-5.0 - jnp.arange(heads, dtype=F32)))
    idx = jnp.arange(blk, dtype=F32)
    rel = idx[:, None] - idx[None, :]
    dmat = jnp.where(rel >= 0, jnp.exp(log_gamma[:, None, None] * jnp.maximum(rel, 0.0)), 0.0)
    qdec = jnp.broadcast_to(jnp.exp(log_gamma[:, None] * (idx + 1.0))[:, :, None], (heads, blk, dk))
    kdec = jnp.broadcast_to(jnp.exp(log_gamma[:, None] * (blk - 1.0 - idx))[:, :, None], (heads, blk, dk))
    cdec = jnp.broadcast_to(jnp.exp(log_gamma * blk)[:, None, None], (heads, 1, dk))
    v_off = 2 * heads * dk // dv
    g_off = v_off + heads
    row = lambda b, h, t: b * nblk + t
    return pl.pallas_call(
        _retention_kernel,
        grid=(batch, heads, nblk),
        in_specs=[pl.BlockSpec((blk, dk), lambda b, h, t: (row(b, h, t), h)),
                  pl.BlockSpec((blk, dk), lambda b, h, t: (row(b, h, t), heads + h)),
                  pl.BlockSpec((blk, dv), lambda b, h, t: (row(b, h, t), v_off + h)),
                  pl.BlockSpec((blk, dv), lambda b, h, t: (row(b, h, t), g_off + h)),
                  pl.BlockSpec((blk, dk), lambda b, h, t: (row(b, h, t), 0)),
                  pl.BlockSpec((blk, dk), lambda b, h, t: (row(b, h, t), 0)),
                  pl.BlockSpec((1, blk, blk), lambda b, h, t: (h, 0, 0)),
                  pl.BlockSpec((1, blk, dk), lambda b, h, t: (h, 0, 0)),
                  pl.BlockSpec((1, blk, dk), lambda b, h, t: (h, 0, 0)),
                  pl.BlockSpec((1, 1, dk), lambda b, h, t: (h, 0, 0)),
                  pl.BlockSpec((1, dv), lambda b, h, t: (0, h))],
        out_specs=pl.BlockSpec((blk, dv), lambda b, h, t: (row(b, h, t), h)),
        out_shape=jax.ShapeDtypeStruct((batch * seq, heads * dv), BF16),
        scratch_shapes=[pltpu.VMEM((dv, dk), F32)],
        compiler_params=_cparams("parallel", "parallel", "arbitrary"),
        name="retention",
    )(r, r, r, r, cos2, sin2, dmat, qdec, kdec, cdec, norm_g.reshape(1, -1).astype(F32))


def _hgrn_kernel(f_ref, q_ref, i_ref, og_ref, la_ref, l1_ref, oml_ref, gain_ref, o_ref, st_ref):
    @pl.when(pl.program_id(2) == 0)
    def _():
        st_ref[...] = jnp.zeros_like(st_ref)

    blk, e = f_ref.shape
    c = HGRN_CHUNK
    nch = blk // c
    z = f_ref[...]
    log_sig = jnp.minimum(z, 0.0) - jnp.log1p(jnp.exp(-jnp.abs(z)))
    la = la_ref[...]
    u = l1_ref[...] + log_sig
    log_f = jnp.maximum(la, u) + jnp.log1p(jnp.exp(-jnp.abs(la - u)))
    key = oml_ref[...] * (1.0 / (1.0 + jnp.exp(z)))

    row = lax.broadcasted_iota(jnp.int32, (blk, e), 0)
    row_in_chunk = row % c
    b = log_f
    shift = 1
    while shift < c:
        b = b + jnp.where(row_in_chunk >= shift, pltpu.roll(b, shift, 0), 0.0)
        shift *= 2
    b3 = b.reshape(nch, c, e)
    b_mid = jnp.broadcast_to(b3[:, c // 2 - 1:c // 2, :], (nch, c, e)).reshape(blk, e)
    b_end_rows = b3[:, c - 1:c, :]
    b_end = jnp.broadcast_to(b_end_rows, (nch, c, e)).reshape(blk, e)

    q = q_ref[...].astype(F32)
    v = i_ref[...]
    qf = (q * jnp.exp(b - b_mid)).astype(BF16)
    kf = (key * jnp.exp(b_mid - b)).astype(BF16)
    scores = _dot_nt(qf, kf)
    r2 = lax.broadcasted_iota(jnp.int32, (blk, blk), 0)
    c2 = lax.broadcasted_iota(jnp.int32, (blk, blk), 1)
    keep = (r2 // c == c2 // c) & (c2 <= r2)
    intra = _dot(jnp.where(keep, scores, 0.0).astype(BF16), v)

    q_in = (q * jnp.exp(b)).astype(BF16)
    k_out = (key * jnp.exp(b_end - b)).astype(BF16)
    chunk_of_row = row // c
    zero = jnp.zeros_like(q_in)
    k_wide = jnp.concatenate([jnp.where(chunk_of_row == j, k_out, zero) for j in range(nch)], axis=1)
    q_wide = jnp.concatenate([jnp.where(chunk_of_row == j, q_in, zero) for j in range(nch)], axis=1)
    v_t = v.astype(F32).T.astype(BF16)
    upd = _dot(v_t, k_wide)
    dec = jnp.exp(b_end_rows)
    st = st_ref[...]
    states = []
    for j in range(nch):
        states.append(st.astype(BF16))
        st = st * dec[j] + upd[:, j * e:(j + 1) * e]
    st_ref[...] = st
    inter = _dot_nt(q_wide, jnp.concatenate(states, axis=1))
    o_ref[...] = _head_norm_gate(intra + inter, gain_ref[...], og_ref[...]).astype(o_ref.dtype)


def hgrn2(hf, hr, lb, norm_g, batch, seq):
    heads, e = HGRN_HEADS, HGRN_EXPAND
    dv = (hr.shape[1] - heads * e) // (2 * heads)
    blk = min(HGRN_BLOCK, seq)
    nblk = seq // blk
    lb = lb.astype(F32).reshape(1, -1)
    la, l1, oml = jnp.log(lb), jnp.log1p(-lb), 1.0 - lb
    i_off = heads * e // dv
    og_off = i_off + heads
    row = lambda b, h, t: b * nblk + t
    vec = pl.BlockSpec((1, e), lambda b, h, t: (0, h))
    return pl.pallas_call(
        _hgrn_kernel,
        grid=(batch, heads, nblk),
        in_specs=[pl.BlockSpec((blk, e), lambda b, h, t: (row(b, h, t), h)),
                  pl.BlockSpec((blk, e), lambda b, h, t: (row(b, h, t), h)),
                  pl.BlockSpec((blk, dv), lambda b, h, t: (row(b, h, t), i_off + h)),
                  pl.BlockSpec((blk, dv), lambda b, h, t: (row(b, h, t), og_off + h)),
                  vec, vec, vec,
                  pl.BlockSpec((1, dv), lambda b, h, t: (0, h))],
        out_specs=pl.BlockSpec((blk, dv), lambda b, h, t: (row(b, h, t), h)),
        out_shape=jax.ShapeDtypeStruct((batch * seq, heads * dv), BF16),
        scratch_shapes=[pltpu.VMEM((dv, e), F32)],
        compiler_params=_cparams("parallel", "parallel", "arbitrary"),
        name="hgrn2",
    )(hf, hr, hr, hr, la, l1, oml, norm_g.reshape(1, -1).astype(F32))


def _mla_prep_kernel(p_ref, qg_ref, kvg_ref, cm_ref, sm_ref, cq_ref, ckv_ref, kr_ref, *, q_rank, kv_rank):
    p = p_ref[...]

    def norm(x, g):
        return x * lax.rsqrt(jnp.mean(x * x, axis=-1, keepdims=True) + EPS) * g

    cq_ref[...] = norm(p[:, :q_rank], qg_ref[...]).astype(cq_ref.dtype)
    ckv_ref[...] = norm(p[:, q_rank:q_rank + kv_rank], kvg_ref[...]).astype(ckv_ref.dtype)
    kr = p[:, q_rank + kv_rank:]
    kr_ref[...] = (kr * cm_ref[...] + pltpu.roll(kr, MLA_ROPE_DIM, 1) * sm_ref[...]).astype(kr_ref.dtype)


def mla_prep(pm, q_norm, kv_norm, cm, sm):
    m = pm.shape[0]
    q_rank, kv_rank = q_norm.shape[0], kv_norm.shape[0]
    rows = min(ROW_TILE, m)
    full = lambda w: pl.BlockSpec((rows, w), lambda i: (i, 0))
    return pl.pallas_call(
        functools.partial(_mla_prep_kernel, q_rank=q_rank, kv_rank=kv_rank),
        grid=(m // rows,),
        in_specs=[full(pm.shape[1]),
                  pl.BlockSpec((1, q_rank), lambda i: (0, 0)),
                  pl.BlockSpec((1, kv_rank), lambda i: (0, 0)),
                  full(LANES), full(LANES)],
        out_specs=[full(q_rank), full(kv_rank), full(LANES)],
        out_shape=[jax.ShapeDtypeStruct((m, q_rank), BF16),
                   jax.ShapeDtypeStruct((m, kv_rank), BF16),
                   jax.ShapeDtypeStruct((m, LANES), BF16)],
        compiler_params=_cparams("parallel"),
        name="mla_prep",
    )(pm, q_norm.reshape(1, -1).astype(F32), kv_norm.reshape(1, -1).astype(F32), cm, sm)


def _attn_kernel(q_ref, kv_ref, kr_ref, cm_ref, sm_ref, o_ref, kcat_ref, *, scale, heads_per_step, tk):
    i = pl.program_id(2)
    tq = q_ref.shape[0]
    kv_w = MLA_NOPE_DIM + MLA_V_DIM
    q_w = MLA_NOPE_DIM + 2 * MLA_ROPE_DIM

    @pl.when(i == 0)
    def _():
        for hh in range(heads_per_step):
            kcat_ref[hh, :, :MLA_NOPE_DIM] = kv_ref[:, hh * kv_w:hh * kv_w + MLA_NOPE_DIM]
            kcat_ref[hh, :, MLA_NOPE_DIM:] = kr_ref[...]

    qfs = []
    for hh in range(heads_per_step):
        qb = q_ref[:, hh * q_w:(hh + 1) * q_w].astype(F32)
        qr = qb[:, MLA_NOPE_DIM:]
        qr = qr * cm_ref[...] + pltpu.roll(qr, MLA_ROPE_DIM, 1) * sm_ref[...]
        qfs.append((jnp.concatenate([qb[:, :MLA_NOPE_DIM], qr], axis=1) * (scale * LOG2_E)).astype(BF16))

    def step(j, carry, masked):
        rows = pl.ds(pl.multiple_of(j * tk, tk), tk)
        out = []
        for hh in range(heads_per_step):
            m, l, acc = carry[hh]
            s = _dot_nt(qfs[hh], kcat_ref[hh, rows, :])
            if masked:
                r2 = lax.broadcasted_iota(jnp.int32, s.shape, 0) + i * tq
                c2 = lax.broadcasted_iota(jnp.int32, s.shape, 1) + j * tk
                s = jnp.where(c2 <= r2, s, -jnp.inf)
            m_new = jnp.maximum(m, jnp.max(s, axis=-1, keepdims=True))
            p = jnp.exp2(s - m_new)
            alpha = jnp.exp2(m - m_new)
            l = alpha * l + jnp.sum(p, axis=-1, keepdims=True)
            v = kv_ref[rows, hh * kv_w + MLA_NOPE_DIM:(hh + 1) * kv_w]
            acc = alpha * acc + _dot(p.astype(BF16), v)
            out.append((m_new, l, acc))
        return tuple(out)

    init = tuple((jnp.full((tq, 1), -jnp.inf, F32), jnp.zeros((tq, 1), F32), jnp.zeros((tq, MLA_V_DIM), F32))
                 for _ in range(heads_per_step))
    per_q = tq // tk
    carry = lax.fori_loop(0, i * per_q, lambda j, cr: step(j, cr, False), init)
    for d in range(per_q):
        carry = step(i * per_q + d, carry, True)
    for hh in range(heads_per_step):
        _, l, acc = carry[hh]
        o_ref[:, hh * MLA_V_DIM:(hh + 1) * MLA_V_DIM] = (acc / l).astype(o_ref.dtype)


def mla_attention(q, kv, kr, cm, sm, batch, seq):
    heads = MLA_HEADS
    hp = ATTN_HEADS_PER_STEP
    tq = min(ATTN_BLOCK, seq)
    nq = seq // tq
    scale = (MLA_NOPE_DIM + MLA_ROPE_DIM) ** -0.5
    qw = MLA_NOPE_DIM + 2 * MLA_ROPE_DIM
    kvw = MLA_NOPE_DIM + MLA_V_DIM
    return pl.pallas_call(
        functools.partial(_attn_kernel, scale=scale, heads_per_step=hp, tk=min(ATTN_KV_BLOCK, tq)),
        grid=(batch, heads // hp, nq),
        in_specs=[pl.BlockSpec((tq, hp * qw), lambda b, h, i: (b * nq + i, h)),
                  pl.BlockSpec((seq, hp * kvw), lambda b, h, i: (b, h)),
                  pl.BlockSpec((seq, LANES), lambda b, h, i: (b, 0)),
                  pl.BlockSpec((tq, LANES), lambda b, h, i: (b * nq + i, 0)),
                  pl.BlockSpec((tq, LANES), lambda b, h, i: (b * nq + i, 0))],
        out_specs=pl.BlockSpec((tq, hp * MLA_V_DIM), lambda b, h, i: (b * nq + i, h)),
        out_shape=jax.ShapeDtypeStruct((batch * seq, heads * MLA_V_DIM), BF16),
        scratch_shapes=[pltpu.VMEM((hp, seq, MLA_NOPE_DIM + LANES), BF16)],
        compiler_params=_cparams("parallel", "parallel", "arbitrary"),
        name="mla_attention",
    )(q, kv, kr, cm, sm)


def _merge_kernel(ga_ref, gb_ref, gc_ref, a_ref, b_ref, c_ref, o_ref):
    y = (ga_ref[...].astype(F32) * a_ref[...].astype(F32)
         + gb_ref[...].astype(F32) * b_ref[...].astype(F32)
         + gc_ref[...].astype(F32) * c_ref[...].astype(F32))
    o_ref[...] = y.astype(o_ref.dtype)


def merge(gates, oa, ob, oc):
    m, d = oa.shape
    rows = min(ROW_TILE, m)
    cols = min(1024, d)
    nc = d // cols
    blk = lambda off: pl.BlockSpec((rows, cols), lambda i, j: (i, off * nc + j))
    return pl.pallas_call(
        _merge_kernel,
        grid=(m // rows, nc),
        in_specs=[blk(0), blk(1), blk(2), blk(0), blk(0), blk(0)],
        out_specs=blk(0),
        out_shape=jax.ShapeDtypeStruct((m, d), BF16),
        compiler_params=_cparams("parallel", "parallel"),
        name="merge",
    )(gates, gates, gates, oa, ob, oc)


def _router_kernel(x_ref, g_ref, r_ref, h_ref, idx_ref, w_ref):
    x = x_ref[...]
    h = x * lax.rsqrt(jnp.mean(x * x, axis=-1, keepdims=True) + EPS) * g_ref[...]
    h_ref[...] = h
    logits = jnp.dot(h, r_ref[...], preferred_element_type=F32, precision=lax.Precision.HIGHEST)
    lane = lax.broadcasted_iota(jnp.int32, logits.shape, 1)
    lg = jnp.where(lane < N_EXPERTS, logits, -jnp.inf)
    m1 = jnp.max(lg, axis=-1, keepdims=True)
    i1 = jnp.min(jnp.where(lg == m1, lane, LANES), axis=-1, keepdims=True)
    lg2 = jnp.where(lane == i1, -jnp.inf, lg)
    m2 = jnp.max(lg2, axis=-1, keepdims=True)
    i2 = jnp.min(jnp.where(lg2 == m2, lane, LANES), axis=-1, keepdims=True)
    e2 = jnp.exp(m2 - m1)
    w1 = 1.0 / (1.0 + e2)
    w2 = e2 / (1.0 + e2)
    idx_ref[...] = jnp.where(lane == 0, i1, i2)
    w_ref[...] = jnp.where(lane == 0, w1, w2)


def moe_router(x, g, router):
    m, d = x.shape
    rows = min(ROW_TILE, m)
    r_pad = jnp.zeros((d, LANES), F32).at[:, :N_EXPERTS].set(router.astype(F32))
    h, idx, w = pl.pallas_call(
        _router_kernel,
        grid=(m // rows,),
        in_specs=[pl.BlockSpec((rows, d), lambda i: (i, 0)),
                  pl.BlockSpec((1, d), lambda i: (0, 0)),
                  pl.BlockSpec((d, LANES), lambda i: (0, 0))],
        out_specs=[pl.BlockSpec((rows, d), lambda i: (i, 0)),
                   pl.BlockSpec((rows, LANES), lambda i: (i, 0)),
                   pl.BlockSpec((rows, LANES), lambda i: (i, 0))],
        out_shape=[jax.ShapeDtypeStruct((m, d), F32),
                   jax.ShapeDtypeStruct((m, LANES), jnp.int32),
                   jax.ShapeDtypeStruct((m, LANES), F32)],
        compiler_params=_cparams("parallel"),
        name="moe_router",
    )(x, g.reshape(1, d).astype(F32), r_pad)
    return h, idx[:, :TOP_K], w[:, :TOP_K]


def _routing_tables(idx, wts, tile):
    t = idx.shape[0]
    pairs = t * TOP_K
    e = idx.reshape(pairs)
    onehot = (e[:, None] == jnp.arange(N_EXPERTS, dtype=jnp.int32)[None, :]).astype(jnp.int32)
    csum = jnp.cumsum(onehot, axis=0)
    rank = jnp.sum(csum * onehot, axis=1) - 1
    counts = csum[-1]
    padded = ((counts + tile - 1) // tile) * tile
    ends = jnp.cumsum(padded)
    starts = ends - padded
    pos = (jnp.sum(onehot * starts[None, :], axis=1) + rank).astype(jnp.int32)
    rows = pairs + N_EXPERTS * tile
    row_tok = jnp.zeros((rows,), jnp.int32).at[pos].set(jnp.arange(pairs, dtype=jnp.int32) // TOP_K)
    row_w = jnp.zeros((rows,), F32).at[pos].set(wts.reshape(pairs))
    tile_start = jnp.arange(rows // tile, dtype=jnp.int32) * tile
    tile_e = jnp.minimum(jnp.sum((tile_start[:, None] >= ends[None, :]).astype(jnp.int32), axis=1),
                         N_EXPERTS - 1).astype(jnp.int32)
    n_used = (ends[-1] // tile).astype(jnp.int32).reshape(1)
    n_tiles = rows // tile
    t_idx = jnp.arange(n_tiles, dtype=jnp.int32)
    prev_e = jnp.concatenate([jnp.full((1,), -1, jnp.int32), tile_e[:-1]])
    first = ((tile_e != prev_e) & (t_idx < n_used[0])).astype(jnp.int32)
    later_first = (t_idx[None, :] > t_idx[:, None]) & (first[None, :] == 1)
    next_idx = jnp.min(jnp.where(later_first, t_idx[None, :], n_tiles), axis=1)
    nxt = jnp.where(next_idx < n_tiles, tile_e[jnp.minimum(next_idx, n_tiles - 1)], -1).astype(jnp.int32)
    return pos, row_tok, row_w.reshape(rows, 1), (tile_e, first, nxt, n_used)


def _row_copy(src_hbm, src_row, dst, dst_row, sem):
    return pltpu.make_async_copy(src_hbm.at[pl.ds(src_row, 1), :], dst.at[pl.ds(dst_row, 1), :], sem)


def _gather_kernel(tok_ref, h_hbm, o_ref, buf, sem):
    tg = buf.shape[0]
    base = pl.program_id(0) * tg

    def start(r, carry):
        _row_copy(h_hbm, tok_ref[base + r], buf, r, sem).start()
        return carry

    def wait(r, carry):
        _row_copy(h_hbm, 0, buf, r, sem).wait()
        return carry

    lax.fori_loop(0, tg, start, 0)
    lax.fori_loop(0, tg, wait, 0)
    o_ref[...] = buf[...].astype(o_ref.dtype)


def moe_gather(h, row_tok):
    rows = row_tok.shape[0]
    d = h.shape[1]
    tg = GATHER_TILE
    return pl.pallas_call(
        _gather_kernel,
        grid_spec=pltpu.PrefetchScalarGridSpec(
            num_scalar_prefetch=1,
            grid=(rows // tg,),
            in_specs=[pl.BlockSpec(memory_space=pl.ANY)],
            out_specs=pl.BlockSpec((tg, d), lambda i, tok: (i, 0)),
            scratch_shapes=[pltpu.VMEM((tg, d), h.dtype), pltpu.SemaphoreType.DMA(())]),
        out_shape=jax.ShapeDtypeStruct((rows, d), BF16),
        compiler_params=_cparams("arbitrary"),
        name="moe_gather",
    )(row_tok, h)


def _moe_mm_kernel(*refs, n_w, layer, bn, has_scale):
    te_ref, first_ref, nxt_ref, nu_ref, a_ref = refs[:5]
    w_hbm = refs[5:5 + n_w]
    rw_ref = refs[5 + n_w] if has_scale else None
    o_ref = refs[5 + n_w + int(has_scale)]
    stage = refs[-1 - 2 * n_w:-1 - n_w]
    wb = refs[-1 - n_w:-1]
    sem = refs[-1]
    j = pl.program_id(0)
    i = pl.program_id(1)

    def copy(e, jj, t):
        cols = pl.ds(pl.multiple_of(jj * bn, LANES), bn)
        return pltpu.make_async_copy(w_hbm[t].at[layer, e, :, cols], stage[t], sem.at[t])

    @pl.when(first_ref[i] == 1)
    def _():
        @pl.when((j == 0) & (i == 0))
        def _():
            for t in range(n_w):
                copy(te_ref[0], 0, t).start()

        for t in range(n_w):
            copy(te_ref[i], j, t).wait()
            wb[t][...] = stage[t][...].astype(BF16)

        nxt = nxt_ref[i]

        @pl.when(nxt >= 0)
        def _():
            for t in range(n_w):
                copy(nxt, j, t).start()

        @pl.when((nxt < 0) & (j + 1 < pl.num_programs(0)))
        def _():
            for t in range(n_w):
                copy(te_ref[0], j + 1, t).start()

    used = i < nu_ref[0]

    @pl.when(used)
    def _():
        a = a_ref[...]
        acc = _dot(a, wb[0][...])
        if n_w == 2:
            acc = _silu(acc) * _dot(a, wb[1][...])
        if has_scale:
            acc = rw_ref[...] * acc
        o_ref[...] = acc.astype(o_ref.dtype)

    @pl.when(jnp.logical_not(used))
    def _():
        o_ref[...] = jnp.zeros_like(o_ref)


def moe_matmul(a, ws, layer, tables, out_dtype, bn, row_scale=None):
    tile_e, first, nxt, n_used = tables
    rows, k = a.shape
    n = ws[0].shape[-1]
    n_w = len(ws)
    tm = MOE_TILE
    bn = _tile(n, bn)
    idx = lambda j, i, *_: (i, 0)
    in_specs = [pl.BlockSpec((tm, k), idx)] + [pl.BlockSpec(memory_space=pl.ANY)] * n_w
    args = [a, *ws]
    if row_scale is not None:
        in_specs.append(pl.BlockSpec((tm, 1), idx))
        args.append(row_scale)
    scratch = ([pltpu.VMEM((k, bn), F32)] * n_w + [pltpu.VMEM((k, bn), BF16)] * n_w
               + [pltpu.SemaphoreType.DMA((n_w,))])
    return pl.pallas_call(
        functools.partial(_moe_mm_kernel, n_w=n_w, layer=layer, bn=bn, has_scale=row_scale is not None),
        grid_spec=pltpu.PrefetchScalarGridSpec(
            num_scalar_prefetch=4,
            grid=(n // bn, rows // tm),
            in_specs=in_specs,
            out_specs=pl.BlockSpec((tm, bn), lambda j, i, *_: (i, j)),
            scratch_shapes=scratch),
        out_shape=jax.ShapeDtypeStruct((rows, n), out_dtype),
        compiler_params=_cparams("arbitrary", "arbitrary"),
        name="moe_matmul",
    )(tile_e, first, nxt, n_used, *args)


def _combine_kernel(pos_ref, x_ref, y_hbm, g_ref, o_ref, buf, sem, *, apply_norm):
    tb = x_ref.shape[0]
    base = pl.program_id(0) * tb

    def start(r, carry):
        for s in range(TOP_K):
            _row_copy(y_hbm, pos_ref[(base + r) * TOP_K + s], buf.at[s], r, sem).start()
        return carry

    def wait(r, carry):
        for s in range(TOP_K):
            _row_copy(y_hbm, 0, buf.at[s], r, sem).wait()
        return carry

    lax.fori_loop(0, tb, start, 0)
    lax.fori_loop(0, tb, wait, 0)
    x = x_ref[...]
    for s in range(TOP_K):
        x = x + buf[s]
    if apply_norm:
        x = x * lax.rsqrt(jnp.mean(x * x, axis=-1, keepdims=True) + EPS) * g_ref[...]
    o_ref[...] = x.astype(o_ref.dtype)


def moe_combine(x, y, pos, final_g, apply_norm):
    m, d = x.shape
    tb = GATHER_TILE
    return pl.pallas_call(
        functools.partial(_combine_kernel, apply_norm=apply_norm),
        grid_spec=pltpu.PrefetchScalarGridSpec(
            num_scalar_prefetch=1,
            grid=(m // tb,),
            in_specs=[pl.BlockSpec((tb, d), lambda i, p: (i, 0)),
                      pl.BlockSpec(memory_space=pl.ANY),
                      pl.BlockSpec((1, d), lambda i, p: (0, 0))],
            out_specs=pl.BlockSpec((tb, d), lambda i, p: (i, 0)),
            scratch_shapes=[pltpu.VMEM((TOP_K, tb, d), y.dtype), pltpu.SemaphoreType.DMA(())]),
        out_shape=jax.ShapeDtypeStruct((m, d), x.dtype),
        compiler_params=_cparams("arbitrary"),
        name="moe_combine_norm",
    )(pos, x, y, final_g.reshape(1, d).astype(F32))


def _rope_tables(positions, dim, pad_to):
    inv = ROPE_THETA ** (-jnp.arange(0, dim, 2, dtype=F32) / dim)
    ang = positions.astype(F32).reshape(-1)[:, None] * inv
    cos, sin = jnp.cos(ang), jnp.sin(ang)
    pad = jnp.zeros((ang.shape[0], pad_to - dim), F32)
    return (jnp.concatenate([cos, cos, pad], axis=1), jnp.concatenate([-sin, sin, pad], axis=1))


def _swap_halves(w):
    half = w.shape[-1] // 2
    return jnp.concatenate([w[..., half:], w[..., :half]], axis=-1)


def kernel(x, positions, ln_mix, w_in, ret_norm, hgrn_norm, hgrn_lb_logits, mla_q_norm, mla_w_uq,
           mla_kv_norm, mla_w_ukv, w_out, ln_ffn, ffn_w1, ffn_w3, ffn_w2, moe_router_w, moe_w1,
           moe_w3, moe_w2, final_norm):
    batch, seq, d_model = x.shape
    depth = w_in.shape[0]
    q_rank, kv_rank = mla_q_norm.shape[1], mla_kv_norm.shape[1]
    ret_w = 2 * RET_HEADS * RET_QK_DIM + 2 * d_model
    hg_e = HGRN_HEADS * HGRN_EXPAND
    hg_w = 2 * hg_e + 2 * d_model
    mla_w = q_rank + kv_rank + MLA_ROPE_DIM

    cos_r, sin_r = _rope_tables(positions, RET_QK_DIM, RET_QK_DIM)
    cm, sm = _rope_tables(positions, MLA_ROPE_DIM, LANES)
    lb_all = jnp.cumsum(jax.nn.softmax(hgrn_lb_logits.astype(F32), axis=0), axis=0)
    lb_all = lb_all - lb_all[:1]

    xf = x.reshape(batch * seq, d_model)
    for l in range(depth):
        last = l == depth - 1
        wl = w_in[l]
        o = ret_w + hg_w
        w_kr = wl[:, o + q_rank + kv_rank:o + mla_w]
        w_mla = jnp.concatenate([wl[:, o:o + mla_w], _swap_halves(w_kr)], axis=1).astype(BF16)
        w_gate = wl[:, o + mla_w:].astype(BF16)

        h = rmsnorm(xf, ln_mix[l], BF16)
        r = staged_matmul(h, [w_in], l, 0, ret_w, BF16)
        hf = staged_matmul(h, [w_in], l, ret_w, hg_e, F32)
        hr = staged_matmul(h, [w_in], l, ret_w + hg_e, hg_w - hg_e, BF16)
        pm = matmul(h, w_mla, F32, bn=w_mla.shape[1])
        gates = matmul(h, w_gate, BF16, act="sigmoid")

        o_a = retention(r, cos_r, sin_r, ret_norm[l], batch, seq)
        o_b = hgrn2(hf, hr, lb_all[l], hgrn_norm[l], batch, seq)

        cqn, ckvn, kr = mla_prep(pm, mla_q_norm[l], mla_kv_norm[l], cm, sm)
        wq = mla_w_uq[l].reshape(q_rank, MLA_HEADS, MLA_NOPE_DIM + MLA_ROPE_DIM)
        wq_rope = wq[..., MLA_NOPE_DIM:]
        wq = jnp.concatenate([wq, _swap_halves(wq_rope)], axis=-1).reshape(q_rank, -1).astype(BF16)
        q = matmul(cqn, wq, BF16)
        kv = matmul(ckvn, mla_w_ukv[l].astype(BF16), BF16)
        o_c = mla_attention(q, kv, kr, cm, sm, batch, seq)

        y = merge(gates, o_a, o_b, o_c)
        xf = staged_matmul(y, [w_out], l, 0, d_model, F32, res=xf)

        if l % 2 == 0:
            j = l // 2
            h2 = rmsnorm(xf, ln_ffn[l], BF16)
            g = staged_matmul(h2, [ffn_w1, ffn_w3], j, 0, ffn_w1.shape[-1], BF16, epilogue="swiglu")
            ffn_dim = g.shape[1]
            bk = ffn_dim // 7 if ffn_dim % (7 * 256) == 0 else ffn_dim
            xf = matmul_acc_res(g, ffn_w2[j].astype(BF16), xf, bk)
            if last:
                xf = rmsnorm(xf, final_norm, x.dtype)
        else:
            j = l // 2
            h2, idx, wts = moe_router(xf, ln_ffn[l], moe_router_w[j])
            pos, row_tok, row_w, tables = _routing_tables(idx, wts, MOE_TILE)
            xs = moe_gather(h2, row_tok)
            g = moe_matmul(xs, [moe_w1, moe_w3], j, tables, BF16, MOE_UP_BN)
            yrows = moe_matmul(g, [moe_w2], j, tables, F32, MOE_DOWN_BN, row_scale=row_w)
            xf = moe_combine(xf, yrows, pos, final_norm, apply_norm=last)
    return xf.reshape(batch, seq, d_model)
```

```python
import functools

import jax
import jax.numpy as jnp
from jax import lax
from jax.experimental import pallas as pl
from jax.experimental.pallas import tpu as pltpu

F32 = jnp.float32
BF16 = jnp.bfloat16

RET_HEADS = 16
RET_QK_DIM = 128
HGRN_HEADS = 16
HGRN_EXPAND = 128
HGRN_CHUNK = 32
MLA_HEADS = 32
MLA_NOPE_DIM = 128
MLA_ROPE_DIM = 64
MLA_V_DIM = 128
N_BRANCHES = 3
N_EXPERTS = 8
TOP_K = 2
ROPE_THETA = 10000.0
EPS = 1e-6
LOG2_E = 1.4426950408889634

LANES = 128
SUBLANES = 8
VMEM_LIMIT_BYTES = 58 * 2 ** 20

ROW_TILE = 512
MM_BM = 1024
MM_BN = 512
RET_BLOCK = 512
HGRN_BLOCK = 256
ATTN_BLOCK = 512
ATTN_KV_BLOCK = 512
ATTN_HEADS_PER_STEP = 2
MOE_TILE = 256
MOE_UP_BN = 896
MOE_DOWN_BN = 1024
FFN_DOWN_BK = 3584
FFN_DOWN_BN = 1024
GATHER_TILE = 256


def _cparams(*sem):
    return pltpu.CompilerParams(dimension_semantics=sem, vmem_limit_bytes=VMEM_LIMIT_BYTES)


def _tile(n, preferred):
    for t in range(min(preferred, n) // LANES * LANES, 0, -LANES):
        if n % t == 0:
            return t
    return n


def _dot(a, b):
    return jnp.dot(a, b, preferred_element_type=F32)


def _dot_nt(a, b):
    return lax.dot_general(a, b, (((1,), (1,)), ((), ())), preferred_element_type=F32)


def _sigmoid(x):
    return 1.0 / (1.0 + jnp.exp(-x))


def _silu(x):
    return x * _sigmoid(x)


def _rmsnorm_kernel(x_ref, g_ref, o_ref):
    x = x_ref[...].astype(F32)
    ms = jnp.mean(x * x, axis=-1, keepdims=True)
    o_ref[...] = (x * lax.rsqrt(ms + EPS) * g_ref[...]).astype(o_ref.dtype)


def rmsnorm(x, g, out_dtype):
    m, d = x.shape
    rows = min(ROW_TILE, m)
    return pl.pallas_call(
        _rmsnorm_kernel,
        grid=(m // rows,),
        in_specs=[pl.BlockSpec((rows, d), lambda i: (i, 0)),
                  pl.BlockSpec((1, d), lambda i: (0, 0))],
        out_specs=pl.BlockSpec((rows, d), lambda i: (i, 0)),
        out_shape=jax.ShapeDtypeStruct((m, d), out_dtype),
        compiler_params=_cparams("parallel"),
        name="rmsnorm",
    )(x, g.reshape(1, d).astype(F32))


def _mm_kernel(*refs, act, has_res):
    a_ref, w_ref = refs[0], refs[1]
    o_ref = refs[-1]
    acc = _dot(a_ref[...], w_ref[...])
    if act == "sigmoid":
        acc = _sigmoid(acc)
    if has_res:
        acc = refs[2][...] + acc
    o_ref[...] = acc.astype(o_ref.dtype)


def matmul(a, w, out_dtype, act=None, res=None, bm=MM_BM, bn=MM_BN):
    m, k = a.shape
    n = w.shape[1]
    bm = _tile(m, bm)
    bn = _tile(n, bn)
    in_specs = [pl.BlockSpec((bm, k), lambda j, i: (i, 0)),
                pl.BlockSpec((k, bn), lambda j, i: (0, j))]
    args = [a, w]
    if res is not None:
        in_specs.append(pl.BlockSpec((bm, bn), lambda j, i: (i, j)))
        args.append(res)
    return pl.pallas_call(
        functools.partial(_mm_kernel, act=act, has_res=res is not None),
        grid=(n // bn, m // bm),
        in_specs=in_specs,
        out_specs=pl.BlockSpec((bm, bn), lambda j, i: (i, j)),
        out_shape=jax.ShapeDtypeStruct((m, n), out_dtype),
        compiler_params=_cparams("parallel", "parallel"),
        name="matmul",
    )(*args)


def _staged_mm_kernel(*refs, n_w, layer, col0, bn, epilogue, has_res, transposed):
    a_ref = refs[0]
    w_hbm = refs[1:1 + n_w]
    res_ref = refs[1 + n_w] if has_res else None
    o_ref = refs[1 + n_w + int(has_res)]
    stage = refs[-1 - 2 * n_w:-1 - n_w]
    wb = refs[-1 - n_w:-1]
    sem = refs[-1]
    j = pl.program_id(0)
    i = pl.program_id(1)

    def copy(jj, t):
        if transposed:
            rows = pl.ds(pl.multiple_of(col0 + jj * bn, SUBLANES), bn)
            return pltpu.make_async_copy(w_hbm[t].at[layer, rows, :], stage[t], sem.at[t])
        cols = pl.ds(pl.multiple_of(col0 + jj * bn, LANES), bn)
        return pltpu.make_async_copy(w_hbm[t].at[layer, :, cols], stage[t], sem.at[t])

    def cast(t):
        if not transposed:
            wb[t][...] = stage[t][...].astype(BF16)
            return
        k = wb[t].shape[0]
        step = _tile(k, 512)
        for c in range(0, k, step):
            wb[t][c:c + step, :] = stage[t][:, c:c + step].T.astype(BF16)

    @pl.when(i == 0)
    def _():
        @pl.when(j == 0)
        def _():
            for t in range(n_w):
                copy(0, t).start()

        for t in range(n_w):
            copy(j, t).wait()
            cast(t)

        @pl.when(j + 1 < pl.num_programs(0))
        def _():
            for t in range(n_w):
                copy(j + 1, t).start()

    a = a_ref[...]
    acc = _dot(a, wb[0][...])
    if epilogue == "swiglu":
        acc = _silu(acc) * _dot(a, wb[1][...])
    elif epilogue == "sigmoid":
        acc = _sigmoid(acc)
    if has_res:
        acc = res_ref[...] + acc
    o_ref[...] = acc.astype(o_ref.dtype)


def staged_matmul(a, ws, layer, col0, n, out_dtype, epilogue=None, res=None, transposed=False,
                  bm=MM_BM, bn=MM_BN):
    m, k = a.shape
    n_w = len(ws)
    bm = _tile(m, bm)
    bn = _tile(n, bn)
    assert col0 % (SUBLANES if transposed else LANES) == 0 and n % bn == 0
    in_specs = [pl.BlockSpec((bm, k), lambda j, i: (i, 0))]
    in_specs += [pl.BlockSpec(memory_space=pl.ANY)] * n_w
    args = [a, *ws]
    if res is not None:
        in_specs.append(pl.BlockSpec((bm, bn), lambda j, i: (i, j)))
        args.append(res)
    stage_shape = (bn, k) if transposed else (k, bn)
    scratch = ([pltpu.VMEM(stage_shape, F32)] * n_w + [pltpu.VMEM((k, bn), BF16)] * n_w
               + [pltpu.SemaphoreType.DMA((n_w,))])
    return pl.pallas_call(
        functools.partial(_staged_mm_kernel, n_w=n_w, layer=layer, col0=col0, bn=bn,
                          epilogue=epilogue, has_res=res is not None, transposed=transposed),
        grid=(n // bn, m // bm),
        in_specs=in_specs,
        out_specs=pl.BlockSpec((bm, bn), lambda j, i: (i, j)),
        out_shape=jax.ShapeDtypeStruct((m, n), out_dtype),
        scratch_shapes=scratch,
        compiler_params=_cparams("arbitrary", "arbitrary"),
        name="staged_matmul",
    )(*args)


def _mm_acc_kernel(a_ref, w_ref, res_ref, o_ref, acc_ref):
    kk = pl.program_id(2)

    @pl.when(kk == 0)
    def _():
        acc_ref[...] = jnp.zeros_like(acc_ref)

    acc_ref[...] += _dot(a_ref[...], w_ref[...])

    @pl.when(kk == pl.num_programs(2) - 1)
    def _():
        o_ref[...] = (res_ref[...] + acc_ref[...]).astype(o_ref.dtype)


def matmul_acc_res(a, w, res, bk, bm=MM_BM, bn=MM_BN):
    m, k = a.shape
    n = w.shape[1]
    bm = _tile(m, bm)
    bn = _tile(n, bn)
    return pl.pallas_call(
        _mm_acc_kernel,
        grid=(n // bn, m // bm, k // bk),
        in_specs=[pl.BlockSpec((bm, bk), lambda j, i, kk: (i, kk)),
                  pl.BlockSpec((bk, bn), lambda j, i, kk: (kk, j)),
                  pl.BlockSpec((bm, bn), lambda j, i, kk: (i, j))],
        out_specs=pl.BlockSpec((bm, bn), lambda j, i, kk: (i, j)),
        out_shape=jax.ShapeDtypeStruct((m, n), res.dtype),
        scratch_shapes=[pltpu.VMEM((bm, bn), F32)],
        compiler_params=_cparams("parallel", "parallel", "arbitrary"),
        name="matmul_acc_res",
    )(a, w, res)


def _head_norm_gate(o, gain, g):
    y = o * lax.rsqrt(jnp.mean(o * o, axis=-1, keepdims=True) + EPS) * gain
    return y * _silu(g.astype(F32))


def _retention_kernel(q_ref, k_ref, v_ref, g_ref, cos_ref, sin_ref, dmat_ref, qdec_ref, kdec_ref,
                      cdec_ref, gain_ref, o_ref, st_ref):
    @pl.when(pl.program_id(2) == 0)
    def _():
        st_ref[...] = jnp.zeros_like(st_ref)

    cos = cos_ref[...]
    sin = sin_ref[...]
    half = RET_QK_DIM // 2
    q = q_ref[...].astype(F32)
    k = k_ref[...].astype(F32)
    qr = q * cos + pltpu.roll(q, half, 1) * sin
    kr = (k * cos + pltpu.roll(k, half, 1) * sin) * (RET_QK_DIM ** -0.5)
    v = v_ref[...]
    scores = _dot_nt(qr.astype(BF16), kr.astype(BF16)) * dmat_ref[0]
    intra = _dot(scores.astype(BF16), v)
    st = st_ref[...]
    cross = _dot_nt((qr * qdec_ref[0]).astype(BF16), st.astype(BF16))
    v_t = v.astype(F32).T.astype(BF16)
    st_ref[...] = cdec_ref[0] * st + _dot(v_t, (kr * kdec_ref[0]).astype(BF16))
    o_ref[...] = _head_norm_gate(intra + cross, gain_ref[...], g_ref[...]).astype(o_ref.dtype)


def retention(r, cos2, sin2, norm_g, batch, seq):
    heads, dk = RET_HEADS, RET_QK_DIM
    dv = (r.shape[1] - 2 * heads * dk) // (2 * heads)
    blk = min(RET_BLOCK, seq)
    nblk = seq // blk
    log_gamma = jnp.log(1.0 - jnp.exp2(-5.0 - jnp.arange(heads, dtype=F32)))
    idx = jnp.arange(blk, dtype=F32)
    rel = idx[:, None] - idx[None, :]
    dmat = jnp.where(rel >= 0, jnp.exp(log_gamma[:, None, None] * jnp.maximum(rel, 0.0)), 0.0)
    qdec = jnp.broadcast_to(jnp.exp(log_gamma[:, None] * (idx + 1.0))[:, :, None], (heads, blk, dk))
    kdec = jnp.broadcast_to(jnp.exp(log_gamma[:, None] * (blk - 1.0 - idx))[:, :, None], (heads, blk, dk))
    cdec = jnp.broadcast_to(jnp.exp(log_gamma * blk)[:, None, None], (heads, 1, dk))
    v_off = 2 * heads * dk // dv
    g_off = v_off + heads
    row = lambda b, h, t: b * nblk + t
    return pl.pallas_call(
        _retention_kernel,
        grid=(batch, heads, nblk),
        in_specs=[pl.BlockSpec((blk, dk), lambda b, h, t: (row(b, h, t), h)),
                  pl.BlockSpec((blk, dk), lambda b, h, t: (row(b, h, t), heads + h)),
                  pl.BlockSpec((blk, dv), lambda b, h, t: (row(b, h, t), v_off + h)),
                  pl.BlockSpec((blk, dv), lambda b, h, t: (row(b, h, t), g_off + h)),
                  pl.BlockSpec((blk, dk), lambda b, h, t: (row(b, h, t), 0)),
                  pl.BlockSpec((blk, dk), lambda b, h, t: (row(b, h, t), 0)),
                  pl.BlockSpec((1, blk, blk), lambda b, h, t: (h, 0, 0)),
                  pl.BlockSpec((1, blk, dk), lambda b, h, t: (h, 0, 0)),
                  pl.BlockSpec((1, blk, dk), lambda b, h, t: (h, 0, 0)),
                  pl.BlockSpec((1, 1, dk), lambda b, h, t: (h, 0, 0)),
                  pl.BlockSpec((1, dv), lambda b, h, t: (0, h))],
        out_specs=pl.BlockSpec((blk, dv), lambda b, h, t: (row(b, h, t), h)),
        out_shape=jax.ShapeDtypeStruct((batch * seq, heads * dv), BF16),
        scratch_shapes=[pltpu.VMEM((dv, dk), F32)],
        compiler_params=_cparams("parallel", "parallel", "arbitrary"),
        name="retention",
    )(r, r, r, r, cos2, sin2, dmat, qdec, kdec, cdec, norm_g.reshape(1, -1).astype(F32))


def _hgrn_kernel(f_ref, q_ref, i_ref, og_ref, la_ref, l1_ref, oml_ref, gain_ref, o_ref, st_ref):
    @pl.when(pl.program_id(2) == 0)
    def _():
        st_ref[...] = jnp.zeros_like(st_ref)

    blk, e = f_ref.shape
    c = HGRN_CHUNK
    nch = blk // c
    z = f_ref[...]
    log_sig = jnp.minimum(z, 0.0) - jnp.log1p(jnp.exp(-jnp.abs(z)))
    la = la_ref[...]
    u = l1_ref[...] + log_sig
    log_f = jnp.maximum(la, u) + jnp.log1p(jnp.exp(-jnp.abs(la - u)))
    key = oml_ref[...] * (1.0 / (1.0 + jnp.exp(z)))

    row = lax.broadcasted_iota(jnp.int32, (blk, e), 0)
    row_in_chunk = row % c
    b = log_f
    shift = 1
    while shift < c:
        b = b + jnp.where(row_in_chunk >= shift, pltpu.roll(b, shift, 0), 0.0)
        shift *= 2
    b3 = b.reshape(nch, c, e)
    b_mid = jnp.broadcast_to(b3[:, c // 2 - 1:c // 2, :], (nch, c, e)).reshape(blk, e)
    b_end_rows = b3[:, c - 1:c, :]
    b_end = jnp.broadcast_to(b_end_rows, (nch, c, e)).reshape(blk, e)

    q = q_ref[...].astype(F32)
    v = i_ref[...]
    qf = (q * jnp.exp(b - b_mid)).astype(BF16)
    kf = (key * jnp.exp(b_mid - b)).astype(BF16)
    scores = _dot_nt(qf, kf)
    r2 = lax.broadcasted_iota(jnp.int32, (blk, blk), 0)
    c2 = lax.broadcasted_iota(jnp.int32, (blk, blk), 1)
    keep = (r2 // c == c2 // c) & (c2 <= r2)
    intra = _dot(jnp.where(keep, scores, 0.0).astype(BF16), v)

    q_in = (q * jnp.exp(b)).astype(BF16)
    k_out = (key * jnp.exp(b_end - b)).astype(BF16)
    chunk_of_row = row // c
    zero = jnp.zeros_like(q_in)
    k_wide = jnp.concatenate([jnp.where(chunk_of_row == j, k_out, zero) for j in range(nch)], axis=1)
    q_wide = jnp.concatenate([jnp.where(chunk_of_row == j, q_in, zero) for j in range(nch)], axis=1)
    v_t = v.astype(F32).T.astype(BF16)
    upd = _dot(v_t, k_wide)
    dec = jnp.exp(b_end_rows)
    st = st_ref[...]
    states = []
    for j in range(nch):
        states.append(st.astype(BF16))
        st = st * dec[j] + upd[:, j * e:(j + 1) * e]
    st_ref[...] = st
    inter = _dot_nt(q_wide, jnp.concatenate(states, axis=1))
    o_ref[...] = _head_norm_gate(intra + inter, gain_ref[...], og_ref[...]).astype(o_ref.dtype)


def hgrn2(hf, hr, lb, norm_g, batch, seq):
    heads, e = HGRN_HEADS, HGRN_EXPAND
    dv = (hr.shape[1] - heads * e) // (2 * heads)
    blk = min(HGRN_BLOCK, seq)
    nblk = seq // blk
    lb = lb.astype(F32).reshape(1, -1)
    la, l1, oml = jnp.log(lb), jnp.log1p(-lb), 1.0 - lb
    i_off = heads * e // dv
    og_off = i_off + heads
    row = lambda b, h, t: b * nblk + t
    vec = pl.BlockSpec((1, e), lambda b, h, t: (0, h))
    return pl.pallas_call(
        _hgrn_kernel,
        grid=(batch, heads, nblk),
        in_specs=[pl.BlockSpec((blk, e), lambda b, h, t: (row(b, h, t), h)),
                  pl.BlockSpec((blk, e), lambda b, h, t: (row(b, h, t), h)),
                  pl.BlockSpec((blk, dv), lambda b, h, t: (row(b, h, t), i_off + h)),
                  pl.BlockSpec((blk, dv), lambda b, h, t: (row(b, h, t), og_off + h)),
                  vec, vec, vec,
                  pl.BlockSpec((1, dv), lambda b, h, t: (0, h))],
        out_specs=pl.BlockSpec((blk, dv), lambda b, h, t: (row(b, h, t), h)),
        out_shape=jax.ShapeDtypeStruct((batch * seq, heads * dv), BF16),
        scratch_shapes=[pltpu.VMEM((dv, e), F32)],
        compiler_params=_cparams("parallel", "parallel", "arbitrary"),
        name="hgrn2",
    )(hf, hr, hr, hr, la, l1, oml, norm_g.reshape(1, -1).astype(F32))


def _mla_prep_kernel(p_ref, qg_ref, kvg_ref, cm_ref, sm_ref, cq_ref, ckv_ref, kr_ref, *, q_rank, kv_rank):
    p = p_ref[...]

    def norm(x, g):
        return x * lax.rsqrt(jnp.mean(x * x, axis=-1, keepdims=True) + EPS) * g

    cq_ref[...] = norm(p[:, :q_rank], qg_ref[...]).astype(cq_ref.dtype)
    ckv_ref[...] = norm(p[:, q_rank:q_rank + kv_rank], kvg_ref[...]).astype(ckv_ref.dtype)
    kr = p[:, q_rank + kv_rank:q_rank + kv_rank + LANES]
    lane = lax.broadcasted_iota(jnp.int32, kr.shape, 1)
    half = MLA_ROPE_DIM // 2
    swapped = jnp.where(lane < MLA_ROPE_DIM + half, pltpu.roll(kr, half, 1),
                        pltpu.roll(kr, MLA_ROPE_DIM + half, 1))
    pair = jnp.where(lane < MLA_ROPE_DIM, kr, swapped)
    kr_ref[...] = (pair * cm_ref[...] + pltpu.roll(pair, MLA_ROPE_DIM, 1) * sm_ref[...]).astype(kr_ref.dtype)


def mla_prep(pm, q_norm, kv_norm, cm, sm):
    m = pm.shape[0]
    q_rank, kv_rank = q_norm.shape[0], kv_norm.shape[0]
    rows = min(ROW_TILE, m)
    full = lambda w: pl.BlockSpec((rows, w), lambda i: (i, 0))
    return pl.pallas_call(
        functools.partial(_mla_prep_kernel, q_rank=q_rank, kv_rank=kv_rank),
        grid=(m // rows,),
        in_specs=[full(pm.shape[1]),
                  pl.BlockSpec((1, q_rank), lambda i: (0, 0)),
                  pl.BlockSpec((1, kv_rank), lambda i: (0, 0)),
                  full(LANES), full(LANES)],
        out_specs=[full(q_rank), full(kv_rank), full(LANES)],
        out_shape=[jax.ShapeDtypeStruct((m, q_rank), BF16),
                   jax.ShapeDtypeStruct((m, kv_rank), BF16),
                   jax.ShapeDtypeStruct((m, LANES), BF16)],
        compiler_params=_cparams("parallel"),
        name="mla_prep",
    )(pm, q_norm.reshape(1, -1).astype(F32), kv_norm.reshape(1, -1).astype(F32), cm, sm)


def _attn_kernel(q_ref, kv_ref, kr_ref, cm_ref, sm_ref, o_ref, kcat_ref, *, scale, heads_per_step, tk):
    i = pl.program_id(2)
    tq = q_ref.shape[0]
    kv_w = MLA_NOPE_DIM + MLA_V_DIM
    q_w = MLA_NOPE_DIM + 2 * MLA_ROPE_DIM

    @pl.when(i == 0)
    def _():
        for hh in range(heads_per_step):
            kcat_ref[hh, :, :MLA_NOPE_DIM] = kv_ref[:, hh * kv_w:hh * kv_w + MLA_NOPE_DIM]
            kcat_ref[hh, :, MLA_NOPE_DIM:] = kr_ref[...]

    qfs = []
    for hh in range(heads_per_step):
        qb = q_ref[:, hh * q_w:(hh + 1) * q_w].astype(F32)
        qr = qb[:, MLA_NOPE_DIM:]
        qr = qr * cm_ref[...] + pltpu.roll(qr, MLA_ROPE_DIM, 1) * sm_ref[...]
        qfs.append((jnp.concatenate([qb[:, :MLA_NOPE_DIM], qr], axis=1) * (scale * LOG2_E)).astype(BF16))

    def step(j, carry, masked):
        rows = pl.ds(pl.multiple_of(j * tk, tk), tk)
        out = []
        for hh in range(heads_per_step):
            m, l, acc = carry[hh]
            s = _dot_nt(qfs[hh], kcat_ref[hh, rows, :])
            if masked:
                r2 = lax.broadcasted_iota(jnp.int32, s.shape, 0) + i * tq
                c2 = lax.broadcasted_iota(jnp.int32, s.shape, 1) + j * tk
                s = jnp.where(c2 <= r2, s, -jnp.inf)
            m_new = jnp.maximum(m, jnp.max(s, axis=-1, keepdims=True))
            p = jnp.exp2(s - m_new)
            alpha = jnp.exp2(m - m_new)
            l = alpha * l + jnp.sum(p, axis=-1, keepdims=True)
            v = kv_ref[rows, hh * kv_w + MLA_NOPE_DIM:(hh + 1) * kv_w]
            acc = alpha * acc + _dot(p.astype(BF16), v)
            out.append((m_new, l, acc))
        return tuple(out)

    init = tuple((jnp.full((tq, 1), -jnp.inf, F32), jnp.zeros((tq, 1), F32), jnp.zeros((tq, MLA_V_DIM), F32))
                 for _ in range(heads_per_step))
    per_q = tq // tk
    carry = lax.fori_loop(0, i * per_q, lambda j, cr: step(j, cr, False), init)
    for d in range(per_q):
        carry = step(i * per_q + d, carry, True)
    for hh in range(heads_per_step):
        _, l, acc = carry[hh]
        o_ref[:, hh * MLA_V_DIM:(hh + 1) * MLA_V_DIM] = (acc / l).astype(o_ref.dtype)


def mla_attention(q, kv, kr, cm, sm, batch, seq):
    heads = MLA_HEADS
    hp = ATTN_HEADS_PER_STEP
    tq = min(ATTN_BLOCK, seq)
    nq = seq // tq
    scale = (MLA_NOPE_DIM + MLA_ROPE_DIM) ** -0.5
    qw = MLA_NOPE_DIM + 2 * MLA_ROPE_DIM
    kvw = MLA_NOPE_DIM + MLA_V_DIM
    return pl.pallas_call(
        functools.partial(_attn_kernel, scale=scale, heads_per_step=hp, tk=min(ATTN_KV_BLOCK, tq)),
        grid=(batch, heads // hp, nq),
        in_specs=[pl.BlockSpec((tq, hp * qw), lambda b, h, i: (b * nq + i, h)),
                  pl.BlockSpec((seq, hp * kvw), lambda b, h, i: (b, h)),
                  pl.BlockSpec((seq, LANES), lambda b, h, i: (b, 0)),
                  pl.BlockSpec((tq, LANES), lambda b, h, i: (b * nq + i, 0)),
                  pl.BlockSpec((tq, LANES), lambda b, h, i: (b * nq + i, 0))],
        out_specs=pl.BlockSpec((tq, hp * MLA_V_DIM), lambda b, h, i: (b * nq + i, h)),
        out_shape=jax.ShapeDtypeStruct((batch * seq, heads * MLA_V_DIM), BF16),
        scratch_shapes=[pltpu.VMEM((hp, seq, MLA_NOPE_DIM + LANES), BF16)],
        compiler_params=_cparams("parallel", "parallel", "arbitrary"),
        name="mla_attention",
    )(q, kv, kr, cm, sm)


def _merge_kernel(ga_ref, gb_ref, gc_ref, a_ref, b_ref, c_ref, o_ref):
    y = (ga_ref[...].astype(F32) * a_ref[...].astype(F32)
         + gb_ref[...].astype(F32) * b_ref[...].astype(F32)
         + gc_ref[...].astype(F32) * c_ref[...].astype(F32))
    o_ref[...] = y.astype(o_ref.dtype)


def merge(gates, oa, ob, oc):
    m, d = oa.shape
    rows = min(ROW_TILE, m)
    cols = min(1024, d)
    nc = d // cols
    blk = lambda off: pl.BlockSpec((rows, cols), lambda i, j: (i, off * nc + j))
    return pl.pallas_call(
        _merge_kernel,
        grid=(m // rows, nc),
        in_specs=[blk(0), blk(1), blk(2), blk(0), blk(0), blk(0)],
        out_specs=blk(0),
        out_shape=jax.ShapeDtypeStruct((m, d), BF16),
        compiler_params=_cparams("parallel", "parallel"),
        name="merge",
    )(gates, gates, gates, oa, ob, oc)


def _router_kernel(x_ref, g_ref, r_ref, h_ref, idx_ref, w_ref):
    x = x_ref[...]
    h = x * lax.rsqrt(jnp.mean(x * x, axis=-1, keepdims=True) + EPS) * g_ref[...]
    h_ref[...] = h
    logits = jnp.dot(h, r_ref[...], preferred_element_type=F32, precision=lax.Precision.HIGHEST)
    lane = lax.broadcasted_iota(jnp.int32, logits.shape, 1)
    lg = jnp.where(lane < N_EXPERTS, logits, -jnp.inf)
    m1 = jnp.max(lg, axis=-1, keepdims=True)
    i1 = jnp.min(jnp.where(lg == m1, lane, LANES), axis=-1, keepdims=True)
    lg2 = jnp.where(lane == i1, -jnp.inf, lg)
    m2 = jnp.max(lg2, axis=-1, keepdims=True)
    i2 = jnp.min(jnp.where(lg2 == m2, lane, LANES), axis=-1, keepdims=True)
    e2 = jnp.exp(m2 - m1)
    w1 = 1.0 / (1.0 + e2)
    w2 = e2 / (1.0 + e2)
    idx_ref[...] = jnp.where(lane == 0, i1, i2)
    w_ref[...] = jnp.where(lane == 0, w1, w2)


def moe_router(x, g, router):
    m, d = x.shape
    rows = min(ROW_TILE, m)
    r_pad = jnp.zeros((d, LANES), F32).at[:, :N_EXPERTS].set(router.astype(F32))
    h, idx, w = pl.pallas_call(
        _router_kernel,
        grid=(m // rows,),
        in_specs=[pl.BlockSpec((rows, d), lambda i: (i, 0)),
                  pl.BlockSpec((1, d), lambda i: (0, 0)),
                  pl.BlockSpec((d, LANES), lambda i: (0, 0))],
        out_specs=[pl.BlockSpec((rows, d), lambda i: (i, 0)),
                   pl.BlockSpec((rows, LANES), lambda i: (i, 0)),
                   pl.BlockSpec((rows, LANES), lambda i: (i, 0))],
        out_shape=[jax.ShapeDtypeStruct((m, d), F32),
                   jax.ShapeDtypeStruct((m, LANES), jnp.int32),
                   jax.ShapeDtypeStruct((m, LANES), F32)],
        compiler_params=_cparams("parallel"),
        name="moe_router",
    )(x, g.reshape(1, d).astype(F32), r_pad)
    return h, idx[:, :TOP_K], w[:, :TOP_K]


def _routing_tables(idx, wts, tile):
    t = idx.shape[0]
    pairs = t * TOP_K
    e = idx.reshape(pairs)
    onehot = (e[:, None] == jnp.arange(N_EXPERTS, dtype=jnp.int32)[None, :]).astype(jnp.int32)
    csum = jnp.cumsum(onehot, axis=0)
    rank = jnp.sum(csum * onehot, axis=1) - 1
    counts = csum[-1]
    padded = ((counts + tile - 1) // tile) * tile
    ends = jnp.cumsum(padded)
    starts = ends - padded
    pos = (jnp.sum(onehot * starts[None, :], axis=1) + rank).astype(jnp.int32)
    rows = pairs + N_EXPERTS * tile
    row_tok = jnp.zeros((rows,), jnp.int32).at[pos].set(jnp.arange(pairs, dtype=jnp.int32) // TOP_K)
    row_w = jnp.zeros((rows,), F32).at[pos].set(wts.reshape(pairs))
    tile_start = jnp.arange(rows // tile, dtype=jnp.int32) * tile
    tile_e = jnp.minimum(jnp.sum((tile_start[:, None] >= ends[None, :]).astype(jnp.int32), axis=1),
                         N_EXPERTS - 1).astype(jnp.int32)
    n_used = (ends[-1] // tile).astype(jnp.int32).reshape(1)
    n_tiles = rows // tile
    t_idx = jnp.arange(n_tiles, dtype=jnp.int32)
    prev_e = jnp.concatenate([jnp.full((1,), -1, jnp.int32), tile_e[:-1]])
    first = ((tile_e != prev_e) & (t_idx < n_used[0])).astype(jnp.int32)
    later_first = (t_idx[None, :] > t_idx[:, None]) & (first[None, :] == 1)
    next_idx = jnp.min(jnp.where(later_first, t_idx[None, :], n_tiles), axis=1)
    nxt = jnp.where(next_idx < n_tiles, tile_e[jnp.minimum(next_idx, n_tiles - 1)], -1).astype(jnp.int32)
    return pos, row_tok, row_w.reshape(rows, 1), (tile_e, first, nxt, n_used)


def _row_copy(src_hbm, src_row, dst, dst_row, sem):
    return pltpu.make_async_copy(src_hbm.at[pl.ds(src_row, 1), :], dst.at[pl.ds(dst_row, 1), :], sem)


def _gather_kernel(tok_ref, h_hbm, o_ref, buf, sem):
    tg = buf.shape[0]
    base = pl.program_id(0) * tg

    def start(r, carry):
        _row_copy(h_hbm, tok_ref[base + r], buf, r, sem).start()
        return carry

    def wait(r, carry):
        _row_copy(h_hbm, 0, buf, r, sem).wait()
        return carry

    lax.fori_loop(0, tg, start, 0)
    lax.fori_loop(0, tg, wait, 0)
    o_ref[...] = buf[...].astype(o_ref.dtype)


def moe_gather(h, row_tok):
    rows = row_tok.shape[0]
    d = h.shape[1]
    tg = GATHER_TILE
    return pl.pallas_call(
        _gather_kernel,
        grid_spec=pltpu.PrefetchScalarGridSpec(
            num_scalar_prefetch=1,
            grid=(rows // tg,),
            in_specs=[pl.BlockSpec(memory_space=pl.ANY)],
            out_specs=pl.BlockSpec((tg, d), lambda i, tok: (i, 0)),
            scratch_shapes=[pltpu.VMEM((tg, d), h.dtype), pltpu.SemaphoreType.DMA(())]),
        out_shape=jax.ShapeDtypeStruct((rows, d), BF16),
        compiler_params=_cparams("arbitrary"),
        name="moe_gather",
    )(row_tok, h)


def _moe_mm_kernel(*refs, n_w, layer, bn, has_scale):
    te_ref, first_ref, nxt_ref, nu_ref, a_ref = refs[:5]
    w_hbm = refs[5:5 + n_w]
    rw_ref = refs[5 + n_w] if has_scale else None
    o_ref = refs[5 + n_w + int(has_scale)]
    stage = refs[-1 - 2 * n_w:-1 - n_w]
    wb = refs[-1 - n_w:-1]
    sem = refs[-1]
    j = pl.program_id(0)
    i = pl.program_id(1)

    def copy(e, jj, t):
        cols = pl.ds(pl.multiple_of(jj * bn, LANES), bn)
        return pltpu.make_async_copy(w_hbm[t].at[layer, e, :, cols], stage[t], sem.at[t])

    @pl.when(first_ref[i] == 1)
    def _():
        @pl.when((j == 0) & (i == 0))
        def _():
            for t in range(n_w):
                copy(te_ref[0], 0, t).start()

        for t in range(n_w):
            copy(te_ref[i], j, t).wait()
            wb[t][...] = stage[t][...].astype(BF16)

        nxt = nxt_ref[i]

        @pl.when(nxt >= 0)
        def _():
            for t in range(n_w):
                copy(nxt, j, t).start()

        @pl.when((nxt < 0) & (j + 1 < pl.num_programs(0)))
        def _():
            for t in range(n_w):
                copy(te_ref[0], j + 1, t).start()

    used = i < nu_ref[0]

    @pl.when(used)
    def _():
        a = a_ref[...]
        acc = _dot(a, wb[0][...])
        if n_w == 2:
            acc = _silu(acc) * _dot(a, wb[1][...])
        if has_scale:
            acc = rw_ref[...] * acc
        o_ref[...] = acc.astype(o_ref.dtype)

    @pl.when(jnp.logical_not(used))
    def _():
        o_ref[...] = jnp.zeros_like(o_ref)


def moe_matmul(a, ws, layer, tables, out_dtype, bn, row_scale=None):
    tile_e, first, nxt, n_used = tables
    rows, k = a.shape
    n = ws[0].shape[-1]
    n_w = len(ws)
    tm = MOE_TILE
    bn = _tile(n, bn)
    idx = lambda j, i, *_: (i, 0)
    in_specs = [pl.BlockSpec((tm, k), idx)] + [pl.BlockSpec(memory_space=pl.ANY)] * n_w
    args = [a, *ws]
    if row_scale is not None:
        in_specs.append(pl.BlockSpec((tm, 1), idx))
        args.append(row_scale)
    scratch = ([pltpu.VMEM((k, bn), F32)] * n_w + [pltpu.VMEM((k, bn), BF16)] * n_w
               + [pltpu.SemaphoreType.DMA((n_w,))])
    return pl.pallas_call(
        functools.partial(_moe_mm_kernel, n_w=n_w, layer=layer, bn=bn, has_scale=row_scale is not None),
        grid_spec=pltpu.PrefetchScalarGridSpec(
            num_scalar_prefetch=4,
            grid=(n // bn, rows // tm),
            in_specs=in_specs,
            out_specs=pl.BlockSpec((tm, bn), lambda j, i, *_: (i, j)),
            scratch_shapes=scratch),
        out_shape=jax.ShapeDtypeStruct((rows, n), out_dtype),
        compiler_params=_cparams("arbitrary", "arbitrary"),
        name="moe_matmul",
    )(tile_e, first, nxt, n_used, *args)


def _combine_kernel(pos_ref, x_ref, y_hbm, g_ref, o_ref, buf, sem, *, apply_norm):
    tb = x_ref.shape[0]
    base = pl.program_id(0) * tb

    def start(r, carry):
        for s in range(TOP_K):
            _row_copy(y_hbm, pos_ref[(base + r) * TOP_K + s], buf.at[s], r, sem).start()
        return carry

    def wait(r, carry):
        for s in range(TOP_K):
            _row_copy(y_hbm, 0, buf.at[s], r, sem).wait()
        return carry

    lax.fori_loop(0, tb, start, 0)
    lax.fori_loop(0, tb, wait, 0)
    x = x_ref[...]
    for s in range(TOP_K):
        x = x + buf[s]
    if apply_norm:
        x = x * lax.rsqrt(jnp.mean(x * x, axis=-1, keepdims=True) + EPS) * g_ref[...]
    o_ref[...] = x.astype(o_ref.dtype)


def moe_combine(x, y, pos, final_g, apply_norm):
    m, d = x.shape
    tb = GATHER_TILE
    return pl.pallas_call(
        functools.partial(_combine_kernel, apply_norm=apply_norm),
        grid_spec=pltpu.PrefetchScalarGridSpec(
            num_scalar_prefetch=1,
            grid=(m // tb,),
            in_specs=[pl.BlockSpec((tb, d), lambda i, p: (i, 0)),
                      pl.BlockSpec(memory_space=pl.ANY),
                      pl.BlockSpec((1, d), lambda i, p: (0, 0))],
            out_specs=pl.BlockSpec((tb, d), lambda i, p: (i, 0)),
            scratch_shapes=[pltpu.VMEM((TOP_K, tb, d), y.dtype), pltpu.SemaphoreType.DMA(())]),
        out_shape=jax.ShapeDtypeStruct((m, d), x.dtype),
        compiler_params=_cparams("arbitrary"),
        name="moe_combine_norm",
    )(pos, x, y, final_g.reshape(1, d).astype(F32))


def _rope_tables(positions, dim, pad_to):
    inv = ROPE_THETA ** (-jnp.arange(0, dim, 2, dtype=F32) / dim)
    ang = positions.astype(F32).reshape(-1)[:, None] * inv
    cos, sin = jnp.cos(ang), jnp.sin(ang)
    pad = jnp.zeros((ang.shape[0], pad_to - dim), F32)
    return (jnp.concatenate([cos, cos, pad], axis=1), jnp.concatenate([-sin, sin, pad], axis=1))


def _swap_halves(w):
    half = w.shape[-1] // 2
    return jnp.concatenate([w[..., half:], w[..., :half]], axis=-1)


def kernel(x, positions, ln_mix, w_in, ret_norm, hgrn_norm, hgrn_lb_logits, mla_q_norm, mla_w_uq,
           mla_kv_norm, mla_w_ukv, w_out, ln_ffn, ffn_w1, ffn_w3, ffn_w2, moe_router_w, moe_w1,
           moe_w3, moe_w2, final_norm):
    batch, seq, d_model = x.shape
    depth = w_in.shape[0]
    q_rank, kv_rank = mla_q_norm.shape[1], mla_kv_norm.shape[1]
    ret_w = 2 * RET_HEADS * RET_QK_DIM + 2 * d_model
    hg_e = HGRN_HEADS * HGRN_EXPAND
    hg_w = 2 * hg_e + 2 * d_model
    mla_w = q_rank + kv_rank + MLA_ROPE_DIM
    mla_pad = -(-(q_rank + kv_rank + LANES) // (2 * LANES)) * (2 * LANES)
    w_in_t = jnp.swapaxes(w_in, 1, 2)

    cos_r, sin_r = _rope_tables(positions, RET_QK_DIM, RET_QK_DIM)
    cm, sm = _rope_tables(positions, MLA_ROPE_DIM, LANES)
    lb_all = jnp.cumsum(jax.nn.softmax(hgrn_lb_logits.astype(F32), axis=0), axis=0)
    lb_all = lb_all - lb_all[:1]

    xf = x.reshape(batch * seq, d_model)
    for l in range(depth):
        last = l == depth - 1
        o = ret_w + hg_w
        h = rmsnorm(xf, ln_mix[l], BF16)
        r = staged_matmul(h, [w_in_t], l, 0, ret_w, BF16, transposed=True)
        hf = staged_matmul(h, [w_in_t], l, ret_w, hg_e, F32, transposed=True)
        hr = staged_matmul(h, [w_in_t], l, ret_w + hg_e, hg_w - hg_e, BF16, transposed=True)
        pm = staged_matmul(h, [w_in_t], l, o, mla_pad, F32, transposed=True, bn=mla_pad // 2)
        gates = staged_matmul(h, [w_in_t], l, o + mla_w, N_BRANCHES * d_model, BF16, epilogue="sigmoid",
                              transposed=True)

        o_a = retention(r, cos_r, sin_r, ret_norm[l], batch, seq)
        o_b = hgrn2(hf, hr, lb_all[l], hgrn_norm[l], batch, seq)

        cqn, ckvn, kr = mla_prep(pm, mla_q_norm[l], mla_kv_norm[l], cm, sm)
        wq = mla_w_uq[l].reshape(q_rank, MLA_HEADS, MLA_NOPE_DIM + MLA_ROPE_DIM)
        wq_rope = wq[..., MLA_NOPE_DIM:]
        wq = jnp.concatenate([wq, _swap_halves(wq_rope)], axis=-1).reshape(q_rank, -1).astype(BF16)
        q = matmul(cqn, wq, BF16)
        kv = matmul(ckvn, mla_w_ukv[l].astype(BF16), BF16)
        o_c = mla_attention(q, kv, kr, cm, sm, batch, seq)

        y = merge(gates, o_a, o_b, o_c)
        xf = staged_matmul(y, [w_out], l, 0, d_model, F32, res=xf)

        if l % 2 == 0:
            j = l // 2
            h2 = rmsnorm(xf, ln_ffn[l], BF16)
            g = staged_matmul(h2, [ffn_w1, ffn_w3], j, 0, ffn_w1.shape[-1], BF16, epilogue="swiglu")
            xf = matmul_acc_res(g, ffn_w2[j].astype(BF16), xf, _tile(g.shape[1], FFN_DOWN_BK), bn=FFN_DOWN_BN)
            if last:
                xf = rmsnorm(xf, final_norm, x.dtype)
        else:
            j = l // 2
            h2, idx, wts = moe_router(xf, ln_ffn[l], moe_router_w[j])
            pos, row_tok, row_w, tables = _routing_tables(idx, wts, MOE_TILE)
            xs = moe_gather(h2, row_tok)
            g = moe_matmul(xs, [moe_w1, moe_w3], j, tables, BF16, MOE_UP_BN)
            yrows = moe_matmul(g, [moe_w2], j, tables, F32, MOE_DOWN_BN, row_scale=row_w)
            xf = moe_combine(xf, yrows, pos, final_norm, apply_norm=last)
    return xf.reshape(batch, seq, d_model)
```

```python
import functools

import jax
import jax.numpy as jnp
from jax import lax
from jax.experimental import pallas as pl
from jax.experimental.pallas import tpu as pltpu

F32 = jnp.float32
BF16 = jnp.bfloat16

RET_HEADS = 16
RET_QK_DIM = 128
HGRN_HEADS = 16
HGRN_EXPAND = 128
HGRN_CHUNK = 32
MLA_HEADS = 32
MLA_NOPE_DIM = 128
MLA_ROPE_DIM = 64
MLA_V_DIM = 128
N_BRANCHES = 3
N_EXPERTS = 8
TOP_K = 2
ROPE_THETA = 10000.0
EPS = 1e-6
LOG2_E = 1.4426950408889634

LANES = 128
SUBLANES = 8
VMEM_LIMIT_BYTES = 58 * 2 ** 20

ROW_TILE = 512
MM_BM = 1024
MM_BN = 512
RET_BLOCK = 512
HGRN_BLOCK = 256
ATTN_BLOCK = 512
ATTN_KV_BLOCK = 512
ATTN_HEADS_PER_STEP = 2
MOE_TILE = 512
MOE_UP_BN = 512
MOE_DOWN_BN = 1024
FFN_DOWN_BK = 3584
FFN_DOWN_BN = 1024
GATHER_TILE = 256


def _cparams(*sem):
    return pltpu.CompilerParams(dimension_semantics=sem, vmem_limit_bytes=VMEM_LIMIT_BYTES)


def _tile(n, preferred):
    for t in range(min(preferred, n) // LANES * LANES, 0, -LANES):
        if n % t == 0:
            return t
    return n


def _dot(a, b):
    return jnp.dot(a, b, preferred_element_type=F32)


def _dot_nt(a, b):
    return lax.dot_general(a, b, (((1,), (1,)), ((), ())), preferred_element_type=F32)


def _sigmoid(x):
    return 0.5 * jnp.tanh(0.5 * x) + 0.5


def _silu(x):
    return x * _sigmoid(x)


def _rmsnorm_kernel(x_ref, g_ref, o_ref):
    x = x_ref[...].astype(F32)
    ms = jnp.mean(x * x, axis=-1, keepdims=True)
    o_ref[...] = (x * lax.rsqrt(ms + EPS) * g_ref[...]).astype(o_ref.dtype)


def rmsnorm(x, g, out_dtype):
    m, d = x.shape
    rows = min(ROW_TILE, m)
    return pl.pallas_call(
        _rmsnorm_kernel,
        grid=(m // rows,),
        in_specs=[pl.BlockSpec((rows, d), lambda i: (i, 0)),
                  pl.BlockSpec((1, d), lambda i: (0, 0))],
        out_specs=pl.BlockSpec((rows, d), lambda i: (i, 0)),
        out_shape=jax.ShapeDtypeStruct((m, d), out_dtype),
        compiler_params=_cparams("parallel"),
        name="rmsnorm",
    )(x, g.reshape(1, d).astype(F32))


def _mm_kernel(*refs, act, has_res):
    a_ref, w_ref = refs[0], refs[1]
    o_ref = refs[-1]
    acc = _dot(a_ref[...], w_ref[...])
    if act == "sigmoid":
        acc = _sigmoid(acc)
    if has_res:
        acc = refs[2][...] + acc
    o_ref[...] = acc.astype(o_ref.dtype)


def matmul(a, w, out_dtype, act=None, res=None, bm=MM_BM, bn=MM_BN):
    m, k = a.shape
    n = w.shape[1]
    bm = _tile(m, bm)
    bn = _tile(n, bn)
    in_specs = [pl.BlockSpec((bm, k), lambda j, i: (i, 0)),
                pl.BlockSpec((k, bn), lambda j, i: (0, j))]
    args = [a, w]
    if res is not None:
        in_specs.append(pl.BlockSpec((bm, bn), lambda j, i: (i, j)))
        args.append(res)
    return pl.pallas_call(
        functools.partial(_mm_kernel, act=act, has_res=res is not None),
        grid=(n // bn, m // bm),
        in_specs=in_specs,
        out_specs=pl.BlockSpec((bm, bn), lambda j, i: (i, j)),
        out_shape=jax.ShapeDtypeStruct((m, n), out_dtype),
        compiler_params=_cparams("parallel", "parallel"),
        name="matmul",
    )(*args)


def _staged_mm_kernel(*refs, n_w, layer, col0, bn, epilogue, has_res, transposed):
    a_ref = refs[0]
    w_hbm = refs[1:1 + n_w]
    res_ref = refs[1 + n_w] if has_res else None
    o_ref = refs[1 + n_w + int(has_res)]
    stage = refs[-1 - 2 * n_w:-1 - n_w]
    wb = refs[-1 - n_w:-1]
    sem = refs[-1]
    j = pl.program_id(0)
    i = pl.program_id(1)

    def copy(jj, t):
        if transposed:
            rows = pl.ds(pl.multiple_of(col0 + jj * bn, SUBLANES), bn)
            return pltpu.make_async_copy(w_hbm[t].at[layer, rows, :], stage[t], sem.at[t])
        cols = pl.ds(pl.multiple_of(col0 + jj * bn, LANES), bn)
        return pltpu.make_async_copy(w_hbm[t].at[layer, :, cols], stage[t], sem.at[t])

    def cast(t):
        if not transposed:
            wb[t][...] = stage[t][...].astype(BF16)
            return
        k = wb[t].shape[0]
        step = _tile(k, 512)
        for c in range(0, k, step):
            wb[t][c:c + step, :] = stage[t][:, c:c + step].T.astype(BF16)

    @pl.when(i == 0)
    def _():
        @pl.when(j == 0)
        def _():
            for t in range(n_w):
                copy(0, t).start()

        for t in range(n_w):
            copy(j, t).wait()
            cast(t)

        @pl.when(j + 1 < pl.num_programs(0))
        def _():
            for t in range(n_w):
                copy(j + 1, t).start()

    a = a_ref[...]
    acc = _dot(a, wb[0][...])
    if epilogue == "swiglu":
        acc = _silu(acc) * _dot(a, wb[1][...])
    elif epilogue == "sigmoid":
        acc = _sigmoid(acc)
    if has_res:
        acc = res_ref[...] + acc
    o_ref[...] = acc.astype(o_ref.dtype)


def staged_matmul(a, ws, layer, col0, n, out_dtype, epilogue=None, res=None, transposed=False,
                  bm=MM_BM, bn=MM_BN):
    m, k = a.shape
    n_w = len(ws)
    bm = _tile(m, bm)
    bn = _tile(n, bn)
    assert col0 % (SUBLANES if transposed else LANES) == 0 and n % bn == 0
    in_specs = [pl.BlockSpec((bm, k), lambda j, i: (i, 0))]
    in_specs += [pl.BlockSpec(memory_space=pl.ANY)] * n_w
    args = [a, *ws]
    if res is not None:
        in_specs.append(pl.BlockSpec((bm, bn), lambda j, i: (i, j)))
        args.append(res)
    stage_shape = (bn, k) if transposed else (k, bn)
    scratch = ([pltpu.VMEM(stage_shape, F32)] * n_w + [pltpu.VMEM((k, bn), BF16)] * n_w
               + [pltpu.SemaphoreType.DMA((n_w,))])
    return pl.pallas_call(
        functools.partial(_staged_mm_kernel, n_w=n_w, layer=layer, col0=col0, bn=bn,
                          epilogue=epilogue, has_res=res is not None, transposed=transposed),
        grid=(n // bn, m // bm),
        in_specs=in_specs,
        out_specs=pl.BlockSpec((bm, bn), lambda j, i: (i, j)),
        out_shape=jax.ShapeDtypeStruct((m, n), out_dtype),
        scratch_shapes=scratch,
        compiler_params=_cparams("arbitrary", "arbitrary"),
        name="staged_matmul",
    )(*args)


def _mm_acc_kernel(a_ref, w_ref, res_ref, o_ref, acc_ref):
    kk = pl.program_id(2)

    @pl.when(kk == 0)
    def _():
        acc_ref[...] = jnp.zeros_like(acc_ref)

    acc_ref[...] += _dot(a_ref[...], w_ref[...])

    @pl.when(kk == pl.num_programs(2) - 1)
    def _():
        o_ref[...] = (res_ref[...] + acc_ref[...]).astype(o_ref.dtype)


def matmul_acc_res(a, w, res, bk, bm=MM_BM, bn=MM_BN):
    m, k = a.shape
    n = w.shape[1]
    bm = _tile(m, bm)
    bn = _tile(n, bn)
    return pl.pallas_call(
        _mm_acc_kernel,
        grid=(n // bn, m // bm, k // bk),
        in_specs=[pl.BlockSpec((bm, bk), lambda j, i, kk: (i, kk)),
                  pl.BlockSpec((bk, bn), lambda j, i, kk: (kk, j)),
                  pl.BlockSpec((bm, bn), lambda j, i, kk: (i, j))],
        out_specs=pl.BlockSpec((bm, bn), lambda j, i, kk: (i, j)),
        out_shape=jax.ShapeDtypeStruct((m, n), res.dtype),
        scratch_shapes=[pltpu.VMEM((bm, bn), F32)],
        compiler_params=_cparams("parallel", "parallel", "arbitrary"),
        name="matmul_acc_res",
    )(a, w, res)


def _head_norm_gate(o, gain, g):
    y = o * lax.rsqrt(jnp.mean(o * o, axis=-1, keepdims=True) + EPS) * gain
    return y * _silu(g.astype(F32))


def _retention_kernel(q_ref, k_ref, v_ref, g_ref, mg_ref, cos_ref, sin_ref, dmat_ref, qdec_ref, kdec_ref,
                      cdec_ref, gain_ref, o_ref, st_ref):
    @pl.when(pl.program_id(2) == 0)
    def _():
        st_ref[...] = jnp.zeros_like(st_ref)

    cos = cos_ref[...]
    sin = sin_ref[...]
    half = RET_QK_DIM // 2
    q = q_ref[...].astype(F32)
    k = k_ref[...].astype(F32)
    qr = q * cos + pltpu.roll(q, half, 1) * sin
    kr = (k * cos + pltpu.roll(k, half, 1) * sin) * (RET_QK_DIM ** -0.5)
    v = v_ref[...]
    scores = _dot_nt(qr.astype(BF16), kr.astype(BF16)) * dmat_ref[0]
    intra = _dot(scores.astype(BF16), v)
    st = st_ref[...]
    cross = _dot_nt((qr * qdec_ref[0]).astype(BF16), st.astype(BF16))
    v_t = v.astype(F32).T.astype(BF16)
    st_ref[...] = cdec_ref[0] * st + _dot(v_t, (kr * kdec_ref[0]).astype(BF16))
    y = _head_norm_gate(intra + cross, gain_ref[...], g_ref[...])
    o_ref[...] = (mg_ref[...].astype(F32) * y).astype(o_ref.dtype)


def retention(r, gates, cos2, sin2, norm_g, batch, seq):
    heads, dk = RET_HEADS, RET_QK_DIM
    dv = (r.shape[1] - 2 * heads * dk) // (2 * heads)
    blk = min(RET_BLOCK, seq)
    nblk = seq // blk
    log_gamma = jnp.log(1.0 - jnp.exp2(-5.0 - jnp.arange(heads, dtype=F32)))
    idx = jnp.arange(blk, dtype=F32)
    rel = idx[:, None] - idx[None, :]
    dmat = jnp.where(rel >= 0, jnp.exp(log_gamma[:, None, None] * jnp.maximum(rel, 0.0)), 0.0)
    qdec = jnp.broadcast_to(jnp.exp(log_gamma[:, None] * (idx + 1.0))[:, :, None], (heads, blk, dk))
    kdec = jnp.broadcast_to(jnp.exp(log_gamma[:, None] * (blk - 1.0 - idx))[:, :, None], (heads, blk, dk))
    cdec = jnp.broadcast_to(jnp.exp(log_gamma * blk)[:, None, None], (heads, 1, dk))
    v_off = 2 * heads * dk // dv
    g_off = v_off + heads
    row = lambda b, h, t: b * nblk + t
    return pl.pallas_call(
        _retention_kernel,
        grid=(batch, heads, nblk),
        in_specs=[pl.BlockSpec((blk, dk), lambda b, h, t: (row(b, h, t), h)),
                  pl.BlockSpec((blk, dk), lambda b, h, t: (row(b, h, t), heads + h)),
                  pl.BlockSpec((blk, dv), lambda b, h, t: (row(b, h, t), v_off + h)),
                  pl.BlockSpec((blk, dv), lambda b, h, t: (row(b, h, t), g_off + h)),
                  pl.BlockSpec((blk, dv), lambda b, h, t: (row(b, h, t), h)),
                  pl.BlockSpec((blk, dk), lambda b, h, t: (row(b, h, t), 0)),
                  pl.BlockSpec((blk, dk), lambda b, h, t: (row(b, h, t), 0)),
                  pl.BlockSpec((1, blk, blk), lambda b, h, t: (h, 0, 0)),
                  pl.BlockSpec((1, blk, dk), lambda b, h, t: (h, 0, 0)),
                  pl.BlockSpec((1, blk, dk), lambda b, h, t: (h, 0, 0)),
                  pl.BlockSpec((1, 1, dk), lambda b, h, t: (h, 0, 0)),
                  pl.BlockSpec((1, dv), lambda b, h, t: (0, h))],
        out_specs=pl.BlockSpec((blk, dv), lambda b, h, t: (row(b, h, t), h)),
        out_shape=jax.ShapeDtypeStruct((batch * seq, heads * dv), BF16),
        scratch_shapes=[pltpu.VMEM((dv, dk), F32)],
        compiler_params=_cparams("parallel", "parallel", "arbitrary"),
        name="retention",
    )(r, r, r, r, gates, cos2, sin2, dmat, qdec, kdec, cdec, norm_g.reshape(1, -1).astype(F32))


def _hgrn_kernel(f_ref, q_ref, i_ref, og_ref, mg_ref, y_ref, la_ref, l1_ref, oml_ref, gain_ref, o_ref,
                 st_ref):
    @pl.when(pl.program_id(2) == 0)
    def _():
        st_ref[...] = jnp.zeros_like(st_ref)

    blk, e = f_ref.shape
    c = HGRN_CHUNK
    nch = blk // c
    z = f_ref[...]
    log_sig = jnp.minimum(z, 0.0) - jnp.log1p(jnp.exp(-jnp.abs(z)))
    la = la_ref[...]
    u = l1_ref[...] + log_sig
    log_f = jnp.maximum(la, u) + jnp.log1p(jnp.exp(-jnp.abs(la - u)))
    key = oml_ref[...] * (1.0 / (1.0 + jnp.exp(z)))

    row = lax.broadcasted_iota(jnp.int32, (blk, e), 0)
    row_in_chunk = row % c
    b = log_f
    shift = 1
    while shift < c:
        b = b + jnp.where(row_in_chunk >= shift, pltpu.roll(b, shift, 0), 0.0)
        shift *= 2
    b3 = b.reshape(nch, c, e)
    b_mid = jnp.broadcast_to(b3[:, c // 2 - 1:c // 2, :], (nch, c, e)).reshape(blk, e)
    b_end_rows = b3[:, c - 1:c, :]
    b_end = jnp.broadcast_to(b_end_rows, (nch, c, e)).reshape(blk, e)

    q = q_ref[...].astype(F32)
    v = i_ref[...]
    qf = (q * jnp.exp(b - b_mid)).astype(BF16)
    kf = (key * jnp.exp(b_mid - b)).astype(BF16)
    scores = _dot_nt(qf, kf)
    r2 = lax.broadcasted_iota(jnp.int32, (blk, blk), 0)
    c2 = lax.broadcasted_iota(jnp.int32, (blk, blk), 1)
    keep = (r2 // c == c2 // c) & (c2 <= r2)
    intra = _dot(jnp.where(keep, scores, 0.0).astype(BF16), v)

    q_in = (q * jnp.exp(b)).astype(BF16)
    k_out = (key * jnp.exp(b_end - b)).astype(BF16)
    chunk_of_row = row // c
    zero = jnp.zeros_like(q_in)
    k_wide = jnp.concatenate([jnp.where(chunk_of_row == j, k_out, zero) for j in range(nch)], axis=1)
    q_wide = jnp.concatenate([jnp.where(chunk_of_row == j, q_in, zero) for j in range(nch)], axis=1)
    v_t = v.astype(F32).T.astype(BF16)
    upd = _dot(v_t, k_wide)
    dec = jnp.exp(b_end_rows)
    st = st_ref[...]
    states = []
    for j in range(nch):
        states.append(st.astype(BF16))
        st = st * dec[j] + upd[:, j * e:(j + 1) * e]
    st_ref[...] = st
    inter = _dot_nt(q_wide, jnp.concatenate(states, axis=1))
    y = _head_norm_gate(intra + inter, gain_ref[...], og_ref[...])
    o_ref[...] = (y_ref[...].astype(F32) + mg_ref[...].astype(F32) * y).astype(o_ref.dtype)


def hgrn2(hf, hr, gates, y_prev, lb, norm_g, batch, seq):
    heads, e = HGRN_HEADS, HGRN_EXPAND
    dv = (hr.shape[1] - heads * e) // (2 * heads)
    blk = min(HGRN_BLOCK, seq)
    nblk = seq // blk
    lb = lb.astype(F32).reshape(1, -1)
    la, l1, oml = jnp.log(lb), jnp.log1p(-lb), 1.0 - lb
    i_off = heads * e // dv
    og_off = i_off + heads
    row = lambda b, h, t: b * nblk + t
    vec = pl.BlockSpec((1, e), lambda b, h, t: (0, h))
    return pl.pallas_call(
        _hgrn_kernel,
        grid=(batch, heads, nblk),
        in_specs=[pl.BlockSpec((blk, e), lambda b, h, t: (row(b, h, t), h)),
                  pl.BlockSpec((blk, e), lambda b, h, t: (row(b, h, t), h)),
                  pl.BlockSpec((blk, dv), lambda b, h, t: (row(b, h, t), i_off + h)),
                  pl.BlockSpec((blk, dv), lambda b, h, t: (row(b, h, t), og_off + h)),
                  pl.BlockSpec((blk, dv), lambda b, h, t: (row(b, h, t), heads + h)),
                  pl.BlockSpec((blk, dv), lambda b, h, t: (row(b, h, t), h)),
                  vec, vec, vec,
                  pl.BlockSpec((1, dv), lambda b, h, t: (0, h))],
        out_specs=pl.BlockSpec((blk, dv), lambda b, h, t: (row(b, h, t), h)),
        out_shape=jax.ShapeDtypeStruct((batch * seq, heads * dv), BF16),
        scratch_shapes=[pltpu.VMEM((dv, e), F32)],
        compiler_params=_cparams("parallel", "parallel", "arbitrary"),
        name="hgrn2",
    )(hf, hr, hr, hr, gates, y_prev, la, l1, oml, norm_g.reshape(1, -1).astype(F32))


def _mla_prep_kernel(p_ref, qg_ref, kvg_ref, cm_ref, sm_ref, cq_ref, ckv_ref, kr_ref, *, q_rank, kv_rank):
    p = p_ref[...]

    def norm(x, g):
        return x * lax.rsqrt(jnp.mean(x * x, axis=-1, keepdims=True) + EPS) * g

    cq_ref[...] = norm(p[:, :q_rank], qg_ref[...]).astype(cq_ref.dtype)
    ckv_ref[...] = norm(p[:, q_rank:q_rank + kv_rank], kvg_ref[...]).astype(ckv_ref.dtype)
    kr = p[:, q_rank + kv_rank:q_rank + kv_rank + LANES]
    lane = lax.broadcasted_iota(jnp.int32, kr.shape, 1)
    half = MLA_ROPE_DIM // 2
    swapped = jnp.where(lane < MLA_ROPE_DIM + half, pltpu.roll(kr, half, 1),
                        pltpu.roll(kr, MLA_ROPE_DIM + half, 1))
    pair = jnp.where(lane < MLA_ROPE_DIM, kr, swapped)
    kr_ref[...] = (pair * cm_ref[...] + pltpu.roll(pair, MLA_ROPE_DIM, 1) * sm_ref[...]).astype(kr_ref.dtype)


def mla_prep(pm, q_norm, kv_norm, cm, sm):
    m = pm.shape[0]
    q_rank, kv_rank = q_norm.shape[0], kv_norm.shape[0]
    rows = min(ROW_TILE, m)
    full = lambda w: pl.BlockSpec((rows, w), lambda i: (i, 0))
    return pl.pallas_call(
        functools.partial(_mla_prep_kernel, q_rank=q_rank, kv_rank=kv_rank),
        grid=(m // rows,),
        in_specs=[full(pm.shape[1]),
                  pl.BlockSpec((1, q_rank), lambda i: (0, 0)),
                  pl.BlockSpec((1, kv_rank), lambda i: (0, 0)),
                  full(LANES), full(LANES)],
        out_specs=[full(q_rank), full(kv_rank), full(LANES)],
        out_shape=[jax.ShapeDtypeStruct((m, q_rank), BF16),
                   jax.ShapeDtypeStruct((m, kv_rank), BF16),
                   jax.ShapeDtypeStruct((m, LANES), BF16)],
        compiler_params=_cparams("parallel"),
        name="mla_prep",
    )(pm, q_norm.reshape(1, -1).astype(F32), kv_norm.reshape(1, -1).astype(F32), cm, sm)


def _attn_kernel(q_ref, kv_ref, kr_ref, cm_ref, sm_ref, mg_ref, y_ref, o_ref, kcat_ref, *, scale,
                 heads_per_step, tk):
    i = pl.program_id(2)
    tq = q_ref.shape[0]
    kv_w = MLA_NOPE_DIM + MLA_V_DIM
    q_w = MLA_NOPE_DIM + 2 * MLA_ROPE_DIM

    @pl.when(i == 0)
    def _():
        for hh in range(heads_per_step):
            kcat_ref[hh, :, :MLA_NOPE_DIM] = kv_ref[:, hh * kv_w:hh * kv_w + MLA_NOPE_DIM]
            kcat_ref[hh, :, MLA_NOPE_DIM:] = kr_ref[...]

    qfs = []
    for hh in range(heads_per_step):
        qb = q_ref[:, hh * q_w:(hh + 1) * q_w].astype(F32)
        qr = qb[:, MLA_NOPE_DIM:]
        qr = qr * cm_ref[...] + pltpu.roll(qr, MLA_ROPE_DIM, 1) * sm_ref[...]
        qfs.append((jnp.concatenate([qb[:, :MLA_NOPE_DIM], qr], axis=1) * (scale * LOG2_E)).astype(BF16))

    def step(j, carry, masked):
        rows = pl.ds(pl.multiple_of(j * tk, tk), tk)
        out = []
        for hh in range(heads_per_step):
            m, l, acc = carry[hh]
            s = _dot_nt(qfs[hh], kcat_ref[hh, rows, :])
            if masked:
                r2 = lax.broadcasted_iota(jnp.int32, s.shape, 0) + i * tq
                c2 = lax.broadcasted_iota(jnp.int32, s.shape, 1) + j * tk
                s = jnp.where(c2 <= r2, s, -jnp.inf)
            m_new = jnp.maximum(m, jnp.max(s, axis=-1, keepdims=True))
            p = jnp.exp2(s - m_new)
            alpha = jnp.exp2(m - m_new)
            l = alpha * l + jnp.sum(p, axis=-1, keepdims=True)
            v = kv_ref[rows, hh * kv_w + MLA_NOPE_DIM:(hh + 1) * kv_w]
            acc = alpha * acc + _dot(p.astype(BF16), v)
            out.append((m_new, l, acc))
        return tuple(out)

    init = tuple((jnp.full((tq, 1), -jnp.inf, F32), jnp.zeros((tq, 1), F32), jnp.zeros((tq, MLA_V_DIM), F32))
                 for _ in range(heads_per_step))
    per_q = tq // tk
    carry = lax.fori_loop(0, i * per_q, lambda j, cr: step(j, cr, False), init)
    for d in range(per_q):
        carry = step(i * per_q + d, carry, True)
    for hh in range(heads_per_step):
        _, l, acc = carry[hh]
        cols = slice(hh * MLA_V_DIM, (hh + 1) * MLA_V_DIM)
        y = y_ref[:, cols].astype(F32) + mg_ref[:, cols].astype(F32) * (acc / l)
        o_ref[:, cols] = y.astype(o_ref.dtype)


def mla_attention(q, kv, kr, cm, sm, gates, y_prev, batch, seq):
    heads = MLA_HEADS
    hp = ATTN_HEADS_PER_STEP
    tq = min(ATTN_BLOCK, seq)
    tk = min(ATTN_KV_BLOCK, tq)
    nq = seq // tq
    scale = (MLA_NOPE_DIM + MLA_ROPE_DIM) ** -0.5
    qw = MLA_NOPE_DIM + 2 * MLA_ROPE_DIM
    kvw = MLA_NOPE_DIM + MLA_V_DIM
    return pl.pallas_call(
        functools.partial(_attn_kernel, scale=scale, heads_per_step=hp, tk=tk),
        grid=(batch, heads // hp, nq),
        in_specs=[pl.BlockSpec((tq, hp * qw), lambda b, h, i: (b * nq + i, h)),
                  pl.BlockSpec((seq, hp * kvw), lambda b, h, i: (b, h)),
                  pl.BlockSpec((seq, LANES), lambda b, h, i: (b, 0)),
                  pl.BlockSpec((tq, LANES), lambda b, h, i: (b * nq + i, 0)),
                  pl.BlockSpec((tq, LANES), lambda b, h, i: (b * nq + i, 0)),
                  pl.BlockSpec((tq, hp * MLA_V_DIM), lambda b, h, i: (b * nq + i, 2 * (heads // hp) + h)),
                  pl.BlockSpec((tq, hp * MLA_V_DIM), lambda b, h, i: (b * nq + i, h))],
        out_specs=pl.BlockSpec((tq, hp * MLA_V_DIM), lambda b, h, i: (b * nq + i, h)),
        out_shape=jax.ShapeDtypeStruct((batch * seq, heads * MLA_V_DIM), BF16),
        scratch_shapes=[pltpu.VMEM((hp, seq, MLA_NOPE_DIM + LANES), BF16)],
        compiler_params=_cparams("parallel", "parallel", "arbitrary"),
        name="mla_attention",
    )(q, kv, kr, cm, sm, gates, y_prev)


def _router_kernel(x_ref, g_ref, r_ref, h_ref, idx_ref, w_ref):
    x = x_ref[...]
    h = x * lax.rsqrt(jnp.mean(x * x, axis=-1, keepdims=True) + EPS) * g_ref[...]
    h_ref[...] = h
    logits = jnp.dot(h, r_ref[...], preferred_element_type=F32, precision=lax.Precision.HIGHEST)
    lane = lax.broadcasted_iota(jnp.int32, logits.shape, 1)
    lg = jnp.where(lane < N_EXPERTS, logits, -jnp.inf)
    m1 = jnp.max(lg, axis=-1, keepdims=True)
    i1 = jnp.min(jnp.where(lg == m1, lane, LANES), axis=-1, keepdims=True)
    lg2 = jnp.where(lane == i1, -jnp.inf, lg)
    m2 = jnp.max(lg2, axis=-1, keepdims=True)
    i2 = jnp.min(jnp.where(lg2 == m2, lane, LANES), axis=-1, keepdims=True)
    e2 = jnp.exp(m2 - m1)
    w1 = 1.0 / (1.0 + e2)
    w2 = e2 / (1.0 + e2)
    idx_ref[...] = jnp.where(lane == 0, i1, i2)
    w_ref[...] = jnp.where(lane == 0, w1, w2)


def moe_router(x, g, router):
    m, d = x.shape
    rows = min(ROW_TILE, m)
    r_pad = jnp.zeros((d, LANES), F32).at[:, :N_EXPERTS].set(router.astype(F32))
    h, idx, w = pl.pallas_call(
        _router_kernel,
        grid=(m // rows,),
        in_specs=[pl.BlockSpec((rows, d), lambda i: (i, 0)),
                  pl.BlockSpec((1, d), lambda i: (0, 0)),
                  pl.BlockSpec((d, LANES), lambda i: (0, 0))],
        out_specs=[pl.BlockSpec((rows, d), lambda i: (i, 0)),
                   pl.BlockSpec((rows, LANES), lambda i: (i, 0)),
                   pl.BlockSpec((rows, LANES), lambda i: (i, 0))],
        out_shape=[jax.ShapeDtypeStruct((m, d), F32),
                   jax.ShapeDtypeStruct((m, LANES), jnp.int32),
                   jax.ShapeDtypeStruct((m, LANES), F32)],
        compiler_params=_cparams("parallel"),
        name="moe_router",
    )(x, g.reshape(1, d).astype(F32), r_pad)
    return h, idx[:, :TOP_K], w[:, :TOP_K]


def _routing_tables(idx, wts, tile):
    t = idx.shape[0]
    pairs = t * TOP_K
    e = idx.reshape(pairs)
    onehot = (e[:, None] == jnp.arange(N_EXPERTS, dtype=jnp.int32)[None, :]).astype(jnp.int32)
    csum = jnp.cumsum(onehot, axis=0)
    rank = jnp.sum(csum * onehot, axis=1) - 1
    counts = csum[-1]
    padded = ((counts + tile - 1) // tile) * tile
    ends = jnp.cumsum(padded)
    starts = ends - padded
    pos = (jnp.sum(onehot * starts[None, :], axis=1) + rank).astype(jnp.int32)
    rows = pairs + N_EXPERTS * tile
    row_tok = jnp.zeros((rows,), jnp.int32).at[pos].set(jnp.arange(pairs, dtype=jnp.int32) // TOP_K)
    row_w = jnp.zeros((rows,), F32).at[pos].set(wts.reshape(pairs))
    tile_start = jnp.arange(rows // tile, dtype=jnp.int32) * tile
    tile_e = jnp.minimum(jnp.sum((tile_start[:, None] >= ends[None, :]).astype(jnp.int32), axis=1),
                         N_EXPERTS - 1).astype(jnp.int32)
    n_used = (ends[-1] // tile).astype(jnp.int32).reshape(1)
    n_tiles = rows // tile
    t_idx = jnp.arange(n_tiles, dtype=jnp.int32)
    prev_e = jnp.concatenate([jnp.full((1,), -1, jnp.int32), tile_e[:-1]])
    first = ((tile_e != prev_e) & (t_idx < n_used[0])).astype(jnp.int32)
    later_first = (t_idx[None, :] > t_idx[:, None]) & (first[None, :] == 1)
    next_idx = jnp.min(jnp.where(later_first, t_idx[None, :], n_tiles), axis=1)
    nxt = jnp.where(next_idx < n_tiles, tile_e[jnp.minimum(next_idx, n_tiles - 1)], -1).astype(jnp.int32)
    return pos, row_tok, row_w.reshape(rows, 1), (tile_e, first, nxt, n_used)


def _row_copy(src_hbm, src_row, dst, dst_row, sem):
    return pltpu.make_async_copy(src_hbm.at[pl.ds(src_row, 1), :], dst.at[pl.ds(dst_row, 1), :], sem)


def _gather_kernel(tok_ref, h_hbm, o_ref, buf, sem):
    tg = buf.shape[1]
    i = pl.program_id(0)
    slot = i % 2

    def issue(tile, dst_slot):
        def start(r, carry):
            _row_copy(h_hbm, tok_ref[tile * tg + r], buf.at[dst_slot], r, sem.at[dst_slot]).start()
            return carry
        lax.fori_loop(0, tg, start, 0, unroll=8)

    @pl.when(i == 0)
    def _():
        issue(0, 0)

    @pl.when(i + 1 < pl.num_programs(0))
    def _():
        issue(i + 1, 1 - slot)

    def wait(r, carry):
        _row_copy(h_hbm, 0, buf.at[slot], r, sem.at[slot]).wait()
        return carry

    lax.fori_loop(0, tg, wait, 0, unroll=8)
    o_ref[...] = buf[slot].astype(o_ref.dtype)


def moe_gather(h, row_tok):
    rows = row_tok.shape[0]
    d = h.shape[1]
    tg = GATHER_TILE
    return pl.pallas_call(
        _gather_kernel,
        grid_spec=pltpu.PrefetchScalarGridSpec(
            num_scalar_prefetch=1,
            grid=(rows // tg,),
            in_specs=[pl.BlockSpec(memory_space=pl.ANY)],
            out_specs=pl.BlockSpec((tg, d), lambda i, tok: (i, 0)),
            scratch_shapes=[pltpu.VMEM((2, tg, d), h.dtype), pltpu.SemaphoreType.DMA((2,))]),
        out_shape=jax.ShapeDtypeStruct((rows, d), BF16),
        compiler_params=_cparams("arbitrary"),
        name="moe_gather",
    )(row_tok, h)


def _moe_mm_kernel(*refs, n_w, layer, bn, has_scale):
    te_ref, first_ref, nxt_ref, nu_ref, a_ref = refs[:5]
    w_hbm = refs[5:5 + n_w]
    rw_ref = refs[5 + n_w] if has_scale else None
    o_ref = refs[5 + n_w + int(has_scale)]
    stage = refs[-1 - 2 * n_w:-1 - n_w]
    wb = refs[-1 - n_w:-1]
    sem = refs[-1]
    j = pl.program_id(0)
    i = pl.program_id(1)

    def copy(e, jj, t):
        cols = pl.ds(pl.multiple_of(jj * bn, LANES), bn)
        return pltpu.make_async_copy(w_hbm[t].at[layer, e, :, cols], stage[t], sem.at[t])

    @pl.when(first_ref[i] == 1)
    def _():
        @pl.when((j == 0) & (i == 0))
        def _():
            for t in range(n_w):
                copy(te_ref[0], 0, t).start()

        for t in range(n_w):
            copy(te_ref[i], j, t).wait()
            wb[t][...] = stage[t][...].astype(BF16)

        nxt = nxt_ref[i]

        @pl.when(nxt >= 0)
        def _():
            for t in range(n_w):
                copy(nxt, j, t).start()

        @pl.when((nxt < 0) & (j + 1 < pl.num_programs(0)))
        def _():
            for t in range(n_w):
                copy(te_ref[0], j + 1, t).start()

    used = i < nu_ref[0]

    @pl.when(used)
    def _():
        a = a_ref[...]
        acc = _dot(a, wb[0][...])
        if n_w == 2:
            acc = _silu(acc) * _dot(a, wb[1][...])
        if has_scale:
            acc = rw_ref[...] * acc
        o_ref[...] = acc.astype(o_ref.dtype)

    @pl.when(jnp.logical_not(used))
    def _():
        o_ref[...] = jnp.zeros_like(o_ref)


def moe_matmul(a, ws, layer, tables, out_dtype, bn, row_scale=None):
    tile_e, first, nxt, n_used = tables
    rows, k = a.shape
    n = ws[0].shape[-1]
    n_w = len(ws)
    tm = MOE_TILE
    bn = _tile(n, bn)
    idx = lambda j, i, *_: (i, 0)
    in_specs = [pl.BlockSpec((tm, k), idx)] + [pl.BlockSpec(memory_space=pl.ANY)] * n_w
    args = [a, *ws]
    if row_scale is not None:
        in_specs.append(pl.BlockSpec((tm, 1), idx))
        args.append(row_scale)
    scratch = ([pltpu.VMEM((k, bn), F32)] * n_w + [pltpu.VMEM((k, bn), BF16)] * n_w
               + [pltpu.SemaphoreType.DMA((n_w,))])
    return pl.pallas_call(
        functools.partial(_moe_mm_kernel, n_w=n_w, layer=layer, bn=bn, has_scale=row_scale is not None),
        grid_spec=pltpu.PrefetchScalarGridSpec(
            num_scalar_prefetch=4,
            grid=(n // bn, rows // tm),
            in_specs=in_specs,
            out_specs=pl.BlockSpec((tm, bn), lambda j, i, *_: (i, j)),
            scratch_shapes=scratch),
        out_shape=jax.ShapeDtypeStruct((rows, n), out_dtype),
        compiler_params=_cparams("arbitrary", "arbitrary"),
        name="moe_matmul",
    )(tile_e, first, nxt, n_used, *args)


def _combine_kernel(pos_ref, x_ref, y_hbm, g_ref, o_ref, buf, sem, *, apply_norm):
    tb = x_ref.shape[0]
    i = pl.program_id(0)
    slot = i % 2

    def issue(tile, dst_slot):
        def start(r, carry):
            for s in range(TOP_K):
                _row_copy(y_hbm, pos_ref[(tile * tb + r) * TOP_K + s], buf.at[dst_slot, s], r,
                          sem.at[dst_slot]).start()
            return carry
        lax.fori_loop(0, tb, start, 0, unroll=4)

    @pl.when(i == 0)
    def _():
        issue(0, 0)

    @pl.when(i + 1 < pl.num_programs(0))
    def _():
        issue(i + 1, 1 - slot)

    def wait(r, carry):
        for s in range(TOP_K):
            _row_copy(y_hbm, 0, buf.at[slot, s], r, sem.at[slot]).wait()
        return carry

    lax.fori_loop(0, tb, wait, 0, unroll=4)
    x = x_ref[...]
    for s in range(TOP_K):
        x = x + buf[slot, s]
    if apply_norm:
        x = x * lax.rsqrt(jnp.mean(x * x, axis=-1, keepdims=True) + EPS) * g_ref[...]
    o_ref[...] = x.astype(o_ref.dtype)


def moe_combine(x, y, pos, final_g, apply_norm):
    m, d = x.shape
    tb = GATHER_TILE
    return pl.pallas_call(
        functools.partial(_combine_kernel, apply_norm=apply_norm),
        grid_spec=pltpu.PrefetchScalarGridSpec(
            num_scalar_prefetch=1,
            grid=(m // tb,),
            in_specs=[pl.BlockSpec((tb, d), lambda i, p: (i, 0)),
                      pl.BlockSpec(memory_space=pl.ANY),
                      pl.BlockSpec((1, d), lambda i, p: (0, 0))],
            out_specs=pl.BlockSpec((tb, d), lambda i, p: (i, 0)),
            scratch_shapes=[pltpu.VMEM((2, TOP_K, tb, d), y.dtype), pltpu.SemaphoreType.DMA((2,))]),
        out_shape=jax.ShapeDtypeStruct((m, d), x.dtype),
        compiler_params=_cparams("arbitrary"),
        name="moe_combine_norm",
    )(pos, x, y, final_g.reshape(1, d).astype(F32))


def _rope_tables(positions, dim, pad_to):
    inv = ROPE_THETA ** (-jnp.arange(0, dim, 2, dtype=F32) / dim)
    ang = positions.astype(F32).reshape(-1)[:, None] * inv
    cos, sin = jnp.cos(ang), jnp.sin(ang)
    pad = jnp.zeros((ang.shape[0], pad_to - dim), F32)
    return (jnp.concatenate([cos, cos, pad], axis=1), jnp.concatenate([-sin, sin, pad], axis=1))


def _swap_halves(w):
    half = w.shape[-1] // 2
    return jnp.concatenate([w[..., half:], w[..., :half]], axis=-1)


def kernel(x, positions, ln_mix, w_in, ret_norm, hgrn_norm, hgrn_lb_logits, mla_q_norm, mla_w_uq,
           mla_kv_norm, mla_w_ukv, w_out, ln_ffn, ffn_w1, ffn_w3, ffn_w2, moe_router_w, moe_w1,
           moe_w3, moe_w2, final_norm):
    batch, seq, d_model = x.shape
    depth = w_in.shape[0]
    q_rank, kv_rank = mla_q_norm.shape[1], mla_kv_norm.shape[1]
    ret_w = 2 * RET_HEADS * RET_QK_DIM + 2 * d_model
    hg_e = HGRN_HEADS * HGRN_EXPAND
    hg_w = 2 * hg_e + 2 * d_model
    mla_w = q_rank + kv_rank + MLA_ROPE_DIM
    mla_pad = -(-(q_rank + kv_rank + LANES) // (2 * LANES)) * (2 * LANES)
    w_in_t = jnp.swapaxes(w_in, 1, 2)

    cos_r, sin_r = _rope_tables(positions, RET_QK_DIM, RET_QK_DIM)
    cm, sm = _rope_tables(positions, MLA_ROPE_DIM, LANES)
    lb_all = jnp.cumsum(jax.nn.softmax(hgrn_lb_logits.astype(F32), axis=0), axis=0)
    lb_all = lb_all - lb_all[:1]

    xf = x.reshape(batch * seq, d_model)
    for l in range(depth):
        last = l == depth - 1
        o = ret_w + hg_w
        h = rmsnorm(xf, ln_mix[l], BF16)
        r = staged_matmul(h, [w_in_t], l, 0, ret_w, BF16, transposed=True)
        hf = staged_matmul(h, [w_in_t], l, ret_w, hg_e, F32, transposed=True)
        hr = staged_matmul(h, [w_in_t], l, ret_w + hg_e, hg_w - hg_e, BF16, transposed=True)
        pm = staged_matmul(h, [w_in_t], l, o, mla_pad, F32, transposed=True, bn=mla_pad // 2)
        gates = staged_matmul(h, [w_in_t], l, o + mla_w, N_BRANCHES * d_model, BF16, epilogue="sigmoid",
                              transposed=True)

        y = retention(r, gates, cos_r, sin_r, ret_norm[l], batch, seq)
        y = hgrn2(hf, hr, gates, y, lb_all[l], hgrn_norm[l], batch, seq)

        cqn, ckvn, kr = mla_prep(pm, mla_q_norm[l], mla_kv_norm[l], cm, sm)
        wq = mla_w_uq[l].reshape(q_rank, MLA_HEADS, MLA_NOPE_DIM + MLA_ROPE_DIM)
        wq_rope = wq[..., MLA_NOPE_DIM:]
        wq = jnp.concatenate([wq, _swap_halves(wq_rope)], axis=-1).reshape(q_rank, -1).astype(BF16)
        q = matmul(cqn, wq, BF16)
        kv = matmul(ckvn, mla_w_ukv[l].astype(BF16), BF16)
        y = mla_attention(q, kv, kr, cm, sm, gates, y, batch, seq)
        xf = staged_matmul(y, [w_out], l, 0, d_model, F32, res=xf)

        if l % 2 == 0:
            j = l // 2
            h2 = rmsnorm(xf, ln_ffn[l], BF16)
            g = staged_matmul(h2, [ffn_w1, ffn_w3], j, 0, ffn_w1.shape[-1], BF16, epilogue="swiglu")
            xf = matmul_acc_res(g, ffn_w2[j].astype(BF16), xf, _tile(g.shape[1], FFN_DOWN_BK), bn=FFN_DOWN_BN)
            if last:
                xf = rmsnorm(xf, final_norm, x.dtype)
        else:
            j = l // 2
            h2, idx, wts = moe_router(xf, ln_ffn[l], moe_router_w[j])
            pos, row_tok, row_w, tables = _routing_tables(idx, wts, MOE_TILE)
            xs = moe_gather(h2, row_tok)
            g = moe_matmul(xs, [moe_w1, moe_w3], j, tables, BF16, MOE_UP_BN)
            yrows = moe_matmul(g, [moe_w2], j, tables, F32, MOE_DOWN_BN, row_scale=row_w)
            xf = moe_combine(xf, yrows, pos, final_norm, apply_norm=last)
    return xf.reshape(batch, seq, d_model)
```

```python
import functools

import jax
import jax.numpy as jnp
from jax import lax
from jax.experimental import pallas as pl
from jax.experimental.pallas import tpu as pltpu

F32 = jnp.float32
BF16 = jnp.bfloat16

RET_HEADS = 16
RET_QK_DIM = 128
HGRN_HEADS = 16
HGRN_EXPAND = 128
HGRN_CHUNK = 32
MLA_HEADS = 32
MLA_NOPE_DIM = 128
MLA_ROPE_DIM = 64
MLA_V_DIM = 128
N_BRANCHES = 3
N_EXPERTS = 8
TOP_K = 2
ROPE_THETA = 10000.0
EPS = 1e-6
LOG2_E = 1.4426950408889634

LANES = 128
SUBLANES = 8
VMEM_LIMIT_BYTES = 58 * 2 ** 20

ROW_TILE = 512
MM_BM = 1024
MM_BN = 512
MLA_UP_BN = 2048
RET_BLOCK = 512
HGRN_BLOCK = 256
ATTN_BLOCK = 512
ATTN_KV_BLOCK = 512
ATTN_HEADS_PER_STEP = 4
MOE_TILE = 512
MOE_UP_BN = 512
MOE_DOWN_BN = 1024
FFN_DOWN_BK = 3584
FFN_DOWN_BN = 1024
GATHER_TILE = 256


def _cparams(*sem):
    return pltpu.CompilerParams(dimension_semantics=sem, vmem_limit_bytes=VMEM_LIMIT_BYTES)


def _tile(n, preferred):
    for t in range(min(preferred, n) // LANES * LANES, 0, -LANES):
        if n % t == 0:
            return t
    return n


def _dot(a, b):
    return jnp.dot(a, b, preferred_element_type=F32)


def _dot_nt(a, b):
    return lax.dot_general(a, b, (((1,), (1,)), ((), ())), preferred_element_type=F32)


def _sigmoid(x):
    return 0.5 * jnp.tanh(0.5 * x) + 0.5


def _silu(x):
    return x * _sigmoid(x)


def _rmsnorm_kernel(x_ref, g_ref, o_ref):
    x = x_ref[...].astype(F32)
    ms = jnp.mean(x * x, axis=-1, keepdims=True)
    o_ref[...] = (x * lax.rsqrt(ms + EPS) * g_ref[...]).astype(o_ref.dtype)


def rmsnorm(x, g, out_dtype):
    m, d = x.shape
    rows = min(ROW_TILE, m)
    return pl.pallas_call(
        _rmsnorm_kernel,
        grid=(m // rows,),
        in_specs=[pl.BlockSpec((rows, d), lambda i: (i, 0)),
                  pl.BlockSpec((1, d), lambda i: (0, 0))],
        out_specs=pl.BlockSpec((rows, d), lambda i: (i, 0)),
        out_shape=jax.ShapeDtypeStruct((m, d), out_dtype),
        compiler_params=_cparams("parallel"),
        name="rmsnorm",
    )(x, g.reshape(1, d).astype(F32))


def _mm_kernel(*refs, act, has_res):
    a_ref, w_ref = refs[0], refs[1]
    o_ref = refs[-1]
    acc = _dot(a_ref[...], w_ref[...])
    if act == "sigmoid":
        acc = _sigmoid(acc)
    if has_res:
        acc = refs[2][...] + acc
    o_ref[...] = acc.astype(o_ref.dtype)


def matmul(a, w, out_dtype, act=None, res=None, bm=MM_BM, bn=MM_BN):
    m, k = a.shape
    n = w.shape[1]
    bm = _tile(m, bm)
    bn = _tile(n, bn)
    in_specs = [pl.BlockSpec((bm, k), lambda j, i: (i, 0)),
                pl.BlockSpec((k, bn), lambda j, i: (0, j))]
    args = [a, w]
    if res is not None:
        in_specs.append(pl.BlockSpec((bm, bn), lambda j, i: (i, j)))
        args.append(res)
    return pl.pallas_call(
        functools.partial(_mm_kernel, act=act, has_res=res is not None),
        grid=(n // bn, m // bm),
        in_specs=in_specs,
        out_specs=pl.BlockSpec((bm, bn), lambda j, i: (i, j)),
        out_shape=jax.ShapeDtypeStruct((m, n), out_dtype),
        compiler_params=_cparams("parallel", "parallel"),
        name="matmul",
    )(*args)


def _staged_mm_kernel(*refs, n_w, layer, col0, bn, epilogue, has_res, transposed):
    a_ref = refs[0]
    w_hbm = refs[1:1 + n_w]
    res_ref = refs[1 + n_w] if has_res else None
    o_ref = refs[1 + n_w + int(has_res)]
    stage = refs[-1 - 2 * n_w:-1 - n_w]
    wb = refs[-1 - n_w:-1]
    sem = refs[-1]
    j = pl.program_id(0)
    i = pl.program_id(1)

    def copy(jj, t):
        if transposed:
            rows = pl.ds(pl.multiple_of(col0 + jj * bn, SUBLANES), bn)
            return pltpu.make_async_copy(w_hbm[t].at[layer, rows, :], stage[t], sem.at[t])
        cols = pl.ds(pl.multiple_of(col0 + jj * bn, LANES), bn)
        return pltpu.make_async_copy(w_hbm[t].at[layer, :, cols], stage[t], sem.at[t])

    def cast(t):
        if not transposed:
            wb[t][...] = stage[t][...].astype(BF16)
            return
        k = wb[t].shape[0]
        step = _tile(k, 512)
        for c in range(0, k, step):
            wb[t][c:c + step, :] = stage[t][:, c:c + step].T.astype(BF16)

    @pl.when(i == 0)
    def _():
        @pl.when(j == 0)
        def _():
            for t in range(n_w):
                copy(0, t).start()

        for t in range(n_w):
            copy(j, t).wait()
            cast(t)

        @pl.when(j + 1 < pl.num_programs(0))
        def _():
            for t in range(n_w):
                copy(j + 1, t).start()

    a = a_ref[...]
    acc = _dot(a, wb[0][...])
    if epilogue == "swiglu":
        acc = _silu(acc) * _dot(a, wb[1][...])
    elif epilogue == "sigmoid":
        acc = _sigmoid(acc)
    if has_res:
        acc = res_ref[...] + acc
    o_ref[...] = acc.astype(o_ref.dtype)


def staged_matmul(a, ws, layer, col0, n, out_dtype, epilogue=None, res=None, transposed=False,
                  bm=MM_BM, bn=MM_BN):
    m, k = a.shape
    n_w = len(ws)
    bm = _tile(m, bm)
    bn = _tile(n, bn)
    assert col0 % (SUBLANES if transposed else LANES) == 0 and n % bn == 0
    in_specs = [pl.BlockSpec((bm, k), lambda j, i: (i, 0))]
    in_specs += [pl.BlockSpec(memory_space=pl.ANY)] * n_w
    args = [a, *ws]
    if res is not None:
        in_specs.append(pl.BlockSpec((bm, bn), lambda j, i: (i, j)))
        args.append(res)
    stage_shape = (bn, k) if transposed else (k, bn)
    scratch = ([pltpu.VMEM(stage_shape, F32)] * n_w + [pltpu.VMEM((k, bn), BF16)] * n_w
               + [pltpu.SemaphoreType.DMA((n_w,))])
    return pl.pallas_call(
        functools.partial(_staged_mm_kernel, n_w=n_w, layer=layer, col0=col0, bn=bn,
                          epilogue=epilogue, has_res=res is not None, transposed=transposed),
        grid=(n // bn, m // bm),
        in_specs=in_specs,
        out_specs=pl.BlockSpec((bm, bn), lambda j, i: (i, j)),
        out_shape=jax.ShapeDtypeStruct((m, n), out_dtype),
        scratch_shapes=scratch,
        compiler_params=_cparams("arbitrary", "arbitrary"),
        name="staged_matmul",
    )(*args)


def _mm_acc_kernel(a_ref, w_ref, res_ref, o_ref, acc_ref):
    kk = pl.program_id(2)

    @pl.when(kk == 0)
    def _():
        acc_ref[...] = jnp.zeros_like(acc_ref)

    acc_ref[...] += _dot(a_ref[...], w_ref[...])

    @pl.when(kk == pl.num_programs(2) - 1)
    def _():
        o_ref[...] = (res_ref[...] + acc_ref[...]).astype(o_ref.dtype)


def matmul_acc_res(a, w, res, bk, bm=MM_BM, bn=MM_BN):
    m, k = a.shape
    n = w.shape[1]
    bm = _tile(m, bm)
    bn = _tile(n, bn)
    return pl.pallas_call(
        _mm_acc_kernel,
        grid=(n // bn, m // bm, k // bk),
        in_specs=[pl.BlockSpec((bm, bk), lambda j, i, kk: (i, kk)),
                  pl.BlockSpec((bk, bn), lambda j, i, kk: (kk, j)),
                  pl.BlockSpec((bm, bn), lambda j, i, kk: (i, j))],
        out_specs=pl.BlockSpec((bm, bn), lambda j, i, kk: (i, j)),
        out_shape=jax.ShapeDtypeStruct((m, n), res.dtype),
        scratch_shapes=[pltpu.VMEM((bm, bn), F32)],
        compiler_params=_cparams("parallel", "parallel", "arbitrary"),
        name="matmul_acc_res",
    )(a, w, res)


def _head_norm_gate(o, gain, g):
    y = o * lax.rsqrt(jnp.mean(o * o, axis=-1, keepdims=True) + EPS) * gain
    return y * _silu(g.astype(F32))


def _retention_kernel(q_ref, k_ref, v_ref, g_ref, mg_ref, cos_ref, sin_ref, dmat_ref, qdec_ref, kdec_ref,
                      cdec_ref, gain_ref, o_ref, st_ref):
    @pl.when(pl.program_id(2) == 0)
    def _():
        st_ref[...] = jnp.zeros_like(st_ref)

    cos = cos_ref[...]
    sin = sin_ref[...]
    half = RET_QK_DIM // 2
    q = q_ref[...].astype(F32)
    k = k_ref[...].astype(F32)
    qr = q * cos + pltpu.roll(q, half, 1) * sin
    kr = (k * cos + pltpu.roll(k, half, 1) * sin) * (RET_QK_DIM ** -0.5)
    v = v_ref[...]
    scores = _dot_nt(qr.astype(BF16), kr.astype(BF16))
    st = st_ref[...]
    cross = _dot_nt((qr * qdec_ref[0]).astype(BF16), st.astype(BF16))
    v_t = v.astype(F32).T.astype(BF16)
    st_ref[...] = cdec_ref[0] * st + _dot(v_t, (kr * kdec_ref[0]).astype(BF16))
    intra = _dot((scores * dmat_ref[0]).astype(BF16), v)
    y = _head_norm_gate(intra + cross, gain_ref[...], g_ref[...])
    o_ref[...] = (mg_ref[...].astype(F32) * y).astype(o_ref.dtype)


def retention(r, gates, cos2, sin2, norm_g, batch, seq):
    heads, dk = RET_HEADS, RET_QK_DIM
    dv = (r.shape[1] - 2 * heads * dk) // (2 * heads)
    blk = min(RET_BLOCK, seq)
    nblk = seq // blk
    log_gamma = jnp.log(1.0 - jnp.exp2(-5.0 - jnp.arange(heads, dtype=F32)))
    idx = jnp.arange(blk, dtype=F32)
    rel = idx[:, None] - idx[None, :]
    dmat = jnp.where(rel >= 0, jnp.exp(log_gamma[:, None, None] * jnp.maximum(rel, 0.0)), 0.0)
    qdec = jnp.broadcast_to(jnp.exp(log_gamma[:, None] * (idx + 1.0))[:, :, None], (heads, blk, dk))
    kdec = jnp.broadcast_to(jnp.exp(log_gamma[:, None] * (blk - 1.0 - idx))[:, :, None], (heads, blk, dk))
    cdec = jnp.broadcast_to(jnp.exp(log_gamma * blk)[:, None, None], (heads, 1, dk))
    v_off = 2 * heads * dk // dv
    g_off = v_off + heads
    row = lambda b, h, t: b * nblk + t
    return pl.pallas_call(
        _retention_kernel,
        grid=(batch, heads, nblk),
        in_specs=[pl.BlockSpec((blk, dk), lambda b, h, t: (row(b, h, t), h)),
                  pl.BlockSpec((blk, dk), lambda b, h, t: (row(b, h, t), heads + h)),
                  pl.BlockSpec((blk, dv), lambda b, h, t: (row(b, h, t), v_off + h)),
                  pl.BlockSpec((blk, dv), lambda b, h, t: (row(b, h, t), g_off + h)),
                  pl.BlockSpec((blk, dv), lambda b, h, t: (row(b, h, t), h)),
                  pl.BlockSpec((blk, dk), lambda b, h, t: (row(b, h, t), 0)),
                  pl.BlockSpec((blk, dk), lambda b, h, t: (row(b, h, t), 0)),
                  pl.BlockSpec((1, blk, blk), lambda b, h, t: (h, 0, 0)),
                  pl.BlockSpec((1, blk, dk), lambda b, h, t: (h, 0, 0)),
                  pl.BlockSpec((1, blk, dk), lambda b, h, t: (h, 0, 0)),
                  pl.BlockSpec((1, 1, dk), lambda b, h, t: (h, 0, 0)),
                  pl.BlockSpec((1, dv), lambda b, h, t: (0, h))],
        out_specs=pl.BlockSpec((blk, dv), lambda b, h, t: (row(b, h, t), h)),
        out_shape=jax.ShapeDtypeStruct((batch * seq, heads * dv), BF16),
        scratch_shapes=[pltpu.VMEM((dv, dk), F32)],
        compiler_params=_cparams("parallel", "parallel", "arbitrary"),
        name="retention",
    )(r, r, r, r, gates, cos2, sin2, dmat, qdec, kdec, cdec, norm_g.reshape(1, -1).astype(F32))


def _hgrn_kernel(f_ref, q_ref, i_ref, og_ref, mg_ref, y_ref, la_ref, l1_ref, oml_ref, gain_ref, o_ref,
                 st_ref):
    @pl.when(pl.program_id(2) == 0)
    def _():
        st_ref[...] = jnp.zeros_like(st_ref)

    blk, e = f_ref.shape
    c = HGRN_CHUNK
    nch = blk // c
    z = f_ref[...]
    log_sig = jnp.minimum(z, 0.0) - jnp.log1p(jnp.exp(-jnp.abs(z)))
    la = la_ref[...]
    u = l1_ref[...] + log_sig
    log_f = jnp.maximum(la, u) + jnp.log1p(jnp.exp(-jnp.abs(la - u)))
    key = oml_ref[...] * (1.0 / (1.0 + jnp.exp(z)))

    row = lax.broadcasted_iota(jnp.int32, (blk, e), 0)
    row_in_chunk = row % c
    b = log_f
    shift = 1
    while shift < c:
        b = b + jnp.where(row_in_chunk >= shift, pltpu.roll(b, shift, 0), 0.0)
        shift *= 2
    b3 = b.reshape(nch, c, e)
    b_mid = jnp.broadcast_to(b3[:, c // 2 - 1:c // 2, :], (nch, c, e)).reshape(blk, e)
    b_end_rows = b3[:, c - 1:c, :]
    b_end = jnp.broadcast_to(b_end_rows, (nch, c, e)).reshape(blk, e)

    q = q_ref[...].astype(F32)
    v = i_ref[...]
    qf = (q * jnp.exp(b - b_mid)).astype(BF16)
    kf = (key * jnp.exp(b_mid - b)).astype(BF16)
    q_in = (q * jnp.exp(b)).astype(BF16)
    k_out = (key * jnp.exp(b_end - b)).astype(BF16)
    chunk_of_row = row // c
    zero = jnp.zeros_like(q_in)
    k_wide = jnp.concatenate([jnp.where(chunk_of_row == j, k_out, zero) for j in range(nch)], axis=1)
    q_wide = jnp.concatenate([jnp.where(chunk_of_row == j, q_in, zero) for j in range(nch)], axis=1)
    v_t = v.astype(F32).T.astype(BF16)
    scores = _dot_nt(qf, kf)
    upd = _dot(v_t, k_wide)
    r2 = lax.broadcasted_iota(jnp.int32, (blk, blk), 0)
    c2 = lax.broadcasted_iota(jnp.int32, (blk, blk), 1)
    keep = (r2 // c == c2 // c) & (c2 <= r2)
    intra = _dot(jnp.where(keep, scores, 0.0).astype(BF16), v)
    dec = jnp.exp(b_end_rows)
    st = st_ref[...]
    states = []
    for j in range(nch):
        states.append(st.astype(BF16))
        st = st * dec[j] + upd[:, j * e:(j + 1) * e]
    st_ref[...] = st
    inter = _dot_nt(q_wide, jnp.concatenate(states, axis=1))
    y = _head_norm_gate(intra + inter, gain_ref[...], og_ref[...])
    o_ref[...] = (y_ref[...].astype(F32) + mg_ref[...].astype(F32) * y).astype(o_ref.dtype)


def hgrn2(hf, hr, gates, y_prev, lb, norm_g, batch, seq):
    heads, e = HGRN_HEADS, HGRN_EXPAND
    dv = (hr.shape[1] - heads * e) // (2 * heads)
    blk = min(HGRN_BLOCK, seq)
    nblk = seq // blk
    lb = lb.astype(F32).reshape(1, -1)
    la, l1, oml = jnp.log(lb), jnp.log1p(-lb), 1.0 - lb
    i_off = heads * e // dv
    og_off = i_off + heads
    row = lambda b, h, t: b * nblk + t
    vec = pl.BlockSpec((1, e), lambda b, h, t: (0, h))
    return pl.pallas_call(
        _hgrn_kernel,
        grid=(batch, heads, nblk),
        in_specs=[pl.BlockSpec((blk, e), lambda b, h, t: (row(b, h, t), h)),
                  pl.BlockSpec((blk, e), lambda b, h, t: (row(b, h, t), h)),
                  pl.BlockSpec((blk, dv), lambda b, h, t: (row(b, h, t), i_off + h)),
                  pl.BlockSpec((blk, dv), lambda b, h, t: (row(b, h, t), og_off + h)),
                  pl.BlockSpec((blk, dv), lambda b, h, t: (row(b, h, t), heads + h)),
                  pl.BlockSpec((blk, dv), lambda b, h, t: (row(b, h, t), h)),
                  vec, vec, vec,
                  pl.BlockSpec((1, dv), lambda b, h, t: (0, h))],
        out_specs=pl.BlockSpec((blk, dv), lambda b, h, t: (row(b, h, t), h)),
        out_shape=jax.ShapeDtypeStruct((batch * seq, heads * dv), BF16),
        scratch_shapes=[pltpu.VMEM((dv, e), F32)],
        compiler_params=_cparams("parallel", "parallel", "arbitrary"),
        name="hgrn2",
    )(hf, hr, hr, hr, gates, y_prev, la, l1, oml, norm_g.reshape(1, -1).astype(F32))


def _mla_prep_kernel(p_ref, qg_ref, kvg_ref, cm_ref, sm_ref, cq_ref, ckv_ref, kr_ref, *, q_rank, kv_rank):
    p = p_ref[...]

    def norm(x, g):
        return x * lax.rsqrt(jnp.mean(x * x, axis=-1, keepdims=True) + EPS) * g

    cq_ref[...] = norm(p[:, :q_rank], qg_ref[...]).astype(cq_ref.dtype)
    ckv_ref[...] = norm(p[:, q_rank:q_rank + kv_rank], kvg_ref[...]).astype(ckv_ref.dtype)
    kr = p[:, q_rank + kv_rank:q_rank + kv_rank + LANES]
    lane = lax.broadcasted_iota(jnp.int32, kr.shape, 1)
    half = MLA_ROPE_DIM // 2
    swapped = jnp.where(lane < MLA_ROPE_DIM + half, pltpu.roll(kr, half, 1),
                        pltpu.roll(kr, MLA_ROPE_DIM + half, 1))
    pair = jnp.where(lane < MLA_ROPE_DIM, kr, swapped)
    kr_ref[...] = (pair * cm_ref[...] + pltpu.roll(pair, MLA_ROPE_DIM, 1) * sm_ref[...]).astype(kr_ref.dtype)


def mla_prep(pm, q_norm, kv_norm, cm, sm):
    m = pm.shape[0]
    q_rank, kv_rank = q_norm.shape[0], kv_norm.shape[0]
    rows = min(ROW_TILE, m)
    full = lambda w: pl.BlockSpec((rows, w), lambda i: (i, 0))
    return pl.pallas_call(
        functools.partial(_mla_prep_kernel, q_rank=q_rank, kv_rank=kv_rank),
        grid=(m // rows,),
        in_specs=[full(pm.shape[1]),
                  pl.BlockSpec((1, q_rank), lambda i: (0, 0)),
                  pl.BlockSpec((1, kv_rank), lambda i: (0, 0)),
                  full(LANES), full(LANES)],
        out_specs=[full(q_rank), full(kv_rank), full(LANES)],
        out_shape=[jax.ShapeDtypeStruct((m, q_rank), BF16),
                   jax.ShapeDtypeStruct((m, kv_rank), BF16),
                   jax.ShapeDtypeStruct((m, LANES), BF16)],
        compiler_params=_cparams("parallel"),
        name="mla_prep",
    )(pm, q_norm.reshape(1, -1).astype(F32), kv_norm.reshape(1, -1).astype(F32), cm, sm)


def _attn_kernel(q_ref, kv_ref, kr_ref, cm_ref, sm_ref, mg_ref, y_ref, o_ref, kcat_ref, vt_ref, *, scale,
                 heads_per_step, tk):
    i = pl.program_id(2)
    tq = q_ref.shape[0]
    kv_w = MLA_NOPE_DIM + MLA_V_DIM
    q_w = MLA_NOPE_DIM + 2 * MLA_ROPE_DIM

    @pl.when(i == 0)
    def _():
        for hh in range(heads_per_step):
            kcat_ref[hh, :, :MLA_NOPE_DIM] = kv_ref[:, hh * kv_w:hh * kv_w + MLA_NOPE_DIM]
            kcat_ref[hh, :, MLA_NOPE_DIM:] = kr_ref[...]
            for jj in range(vt_ref.shape[1]):
                v = kv_ref[jj * tk:(jj + 1) * tk, hh * kv_w + MLA_NOPE_DIM:(hh + 1) * kv_w]
                vt_ref[hh, jj] = v.astype(F32).T.astype(BF16)

    qfs = []
    for hh in range(heads_per_step):
        qb = q_ref[:, hh * q_w:(hh + 1) * q_w].astype(F32)
        qr = qb[:, MLA_NOPE_DIM:]
        qr = qr * cm_ref[...] + pltpu.roll(qr, MLA_ROPE_DIM, 1) * sm_ref[...]
        qfs.append((jnp.concatenate([qb[:, :MLA_NOPE_DIM], qr], axis=1) * (scale * LOG2_E)).astype(BF16))

    def step(j, carry, masked):
        rows = pl.ds(pl.multiple_of(j * tk, tk), tk)
        out = []
        scores = [_dot_nt(kcat_ref[hh, rows, :], qfs[hh]) for hh in range(heads_per_step)]
        for hh in range(heads_per_step):
            m, l, acc = carry[hh]
            s = scores[hh]
            if masked:
                kpos = lax.broadcasted_iota(jnp.int32, s.shape, 0) + j * tk
                qpos = lax.broadcasted_iota(jnp.int32, s.shape, 1) + i * tq
                s = jnp.where(kpos <= qpos, s, -jnp.inf)
            m_new = jnp.maximum(m, jnp.max(s, axis=0, keepdims=True))
            p = jnp.exp2(s - m_new)
            alpha = jnp.exp2(m - m_new)
            l = alpha * l + jnp.sum(p, axis=0, keepdims=True)
            acc = alpha * acc + _dot(vt_ref[hh, j], p.astype(BF16))
            out.append((m_new, l, acc))
        return tuple(out)

    init = tuple((jnp.full((1, tq), -jnp.inf, F32), jnp.zeros((1, tq), F32), jnp.zeros((MLA_V_DIM, tq), F32))
                 for _ in range(heads_per_step))
    per_q = tq // tk
    carry = lax.fori_loop(0, i * per_q, lambda j, cr: step(j, cr, False), init)
    for d in range(per_q):
        carry = step(i * per_q + d, carry, True)
    for hh in range(heads_per_step):
        _, l, acc = carry[hh]
        cols = slice(hh * MLA_V_DIM, (hh + 1) * MLA_V_DIM)
        y = y_ref[:, cols].astype(F32) + mg_ref[:, cols].astype(F32) * (acc / l).T
        o_ref[:, cols] = y.astype(o_ref.dtype)


def mla_attention(q, kv, kr, cm, sm, gates, y_prev, batch, seq):
    heads = MLA_HEADS
    hp = ATTN_HEADS_PER_STEP
    tq = min(ATTN_BLOCK, seq)
    tk = min(ATTN_KV_BLOCK, tq)
    nq = seq // tq
    scale = (MLA_NOPE_DIM + MLA_ROPE_DIM) ** -0.5
    qw = MLA_NOPE_DIM + 2 * MLA_ROPE_DIM
    kvw = MLA_NOPE_DIM + MLA_V_DIM
    return pl.pallas_call(
        functools.partial(_attn_kernel, scale=scale, heads_per_step=hp, tk=tk),
        grid=(batch, heads // hp, nq),
        in_specs=[pl.BlockSpec((tq, hp * qw), lambda b, h, i: (b * nq + i, h)),
                  pl.BlockSpec((seq, hp * kvw), lambda b, h, i: (b, h)),
                  pl.BlockSpec((seq, LANES), lambda b, h, i: (b, 0)),
                  pl.BlockSpec((tq, LANES), lambda b, h, i: (b * nq + i, 0)),
                  pl.BlockSpec((tq, LANES), lambda b, h, i: (b * nq + i, 0)),
                  pl.BlockSpec((tq, hp * MLA_V_DIM), lambda b, h, i: (b * nq + i, 2 * (heads // hp) + h)),
                  pl.BlockSpec((tq, hp * MLA_V_DIM), lambda b, h, i: (b * nq + i, h))],
        out_specs=pl.BlockSpec((tq, hp * MLA_V_DIM), lambda b, h, i: (b * nq + i, h)),
        out_shape=jax.ShapeDtypeStruct((batch * seq, heads * MLA_V_DIM), BF16),
        scratch_shapes=[pltpu.VMEM((hp, seq, MLA_NOPE_DIM + LANES), BF16),
                        pltpu.VMEM((hp, seq // tk, MLA_V_DIM, tk), BF16)],
        compiler_params=_cparams("parallel", "parallel", "arbitrary"),
        name="mla_attention",
    )(q, kv, kr, cm, sm, gates, y_prev)


def _router_kernel(x_ref, g_ref, r_ref, h_ref, idx_ref, w_ref):
    x = x_ref[...]
    h = x * lax.rsqrt(jnp.mean(x * x, axis=-1, keepdims=True) + EPS) * g_ref[...]
    h_ref[...] = h
    logits = jnp.dot(h, r_ref[...], preferred_element_type=F32, precision=lax.Precision.HIGHEST)
    lane = lax.broadcasted_iota(jnp.int32, logits.shape, 1)
    lg = jnp.where(lane < N_EXPERTS, logits, -jnp.inf)
    m1 = jnp.max(lg, axis=-1, keepdims=True)
    i1 = jnp.min(jnp.where(lg == m1, lane, LANES), axis=-1, keepdims=True)
    lg2 = jnp.where(lane == i1, -jnp.inf, lg)
    m2 = jnp.max(lg2, axis=-1, keepdims=True)
    i2 = jnp.min(jnp.where(lg2 == m2, lane, LANES), axis=-1, keepdims=True)
    e2 = jnp.exp(m2 - m1)
    w1 = 1.0 / (1.0 + e2)
    w2 = e2 / (1.0 + e2)
    idx_ref[...] = jnp.where(lane == 0, i1, i2)
    w_ref[...] = jnp.where(lane == 0, w1, w2)


def moe_router(x, g, router):
    m, d = x.shape
    rows = min(ROW_TILE, m)
    r_pad = jnp.zeros((d, LANES), F32).at[:, :N_EXPERTS].set(router.astype(F32))
    h, idx, w = pl.pallas_call(
        _router_kernel,
        grid=(m // rows,),
        in_specs=[pl.BlockSpec((rows, d), lambda i: (i, 0)),
                  pl.BlockSpec((1, d), lambda i: (0, 0)),
                  pl.BlockSpec((d, LANES), lambda i: (0, 0))],
        out_specs=[pl.BlockSpec((rows, d), lambda i: (i, 0)),
                   pl.BlockSpec((rows, LANES), lambda i: (i, 0)),
                   pl.BlockSpec((rows, LANES), lambda i: (i, 0))],
        out_shape=[jax.ShapeDtypeStruct((m, d), F32),
                   jax.ShapeDtypeStruct((m, LANES), jnp.int32),
                   jax.ShapeDtypeStruct((m, LANES), F32)],
        compiler_params=_cparams("parallel"),
        name="moe_router",
    )(x, g.reshape(1, d).astype(F32), r_pad)
    return h, idx[:, :TOP_K], w[:, :TOP_K]


def _routing_tables(idx, wts, tile):
    t = idx.shape[0]
    pairs = t * TOP_K
    e = idx.reshape(pairs)
    onehot = (e[:, None] == jnp.arange(N_EXPERTS, dtype=jnp.int32)[None, :]).astype(jnp.int32)
    csum = jnp.cumsum(onehot, axis=0)
    rank = jnp.sum(csum * onehot, axis=1) - 1
    counts = csum[-1]
    padded = ((counts + tile - 1) // tile) * tile
    ends = jnp.cumsum(padded)
    starts = ends - padded
    pos = (jnp.sum(onehot * starts[None, :], axis=1) + rank).astype(jnp.int32)
    rows = pairs + N_EXPERTS * tile
    row_tok = jnp.zeros((rows,), jnp.int32).at[pos].set(jnp.arange(pairs, dtype=jnp.int32) // TOP_K)
    row_w = jnp.zeros((rows,), F32).at[pos].set(wts.reshape(pairs))
    tile_start = jnp.arange(rows // tile, dtype=jnp.int32) * tile
    tile_e = jnp.minimum(jnp.sum((tile_start[:, None] >= ends[None, :]).astype(jnp.int32), axis=1),
                         N_EXPERTS - 1).astype(jnp.int32)
    n_used = (ends[-1] // tile).astype(jnp.int32).reshape(1)
    n_tiles = rows // tile
    t_idx = jnp.arange(n_tiles, dtype=jnp.int32)
    prev_e = jnp.concatenate([jnp.full((1,), -1, jnp.int32), tile_e[:-1]])
    first = ((tile_e != prev_e) & (t_idx < n_used[0])).astype(jnp.int32)
    later_first = (t_idx[None, :] > t_idx[:, None]) & (first[None, :] == 1)
    next_idx = jnp.min(jnp.where(later_first, t_idx[None, :], n_tiles), axis=1)
    nxt = jnp.where(next_idx < n_tiles, tile_e[jnp.minimum(next_idx, n_tiles - 1)], -1).astype(jnp.int32)
    return pos, row_tok, row_w.reshape(rows, 1), (tile_e, first, nxt, n_used)


def _row_copy(src_hbm, src_row, dst, dst_row, sem):
    return pltpu.make_async_copy(src_hbm.at[pl.ds(src_row, 1), :], dst.at[pl.ds(dst_row, 1), :], sem)


def _gather_kernel(tok_ref, h_hbm, o_ref, buf, sem):
    tg = buf.shape[1]
    i = pl.program_id(0)
    slot = i % 2

    def issue(tile, dst_slot):
        def start(r, carry):
            _row_copy(h_hbm, tok_ref[tile * tg + r], buf.at[dst_slot], r, sem.at[dst_slot]).start()
            return carry
        lax.fori_loop(0, tg, start, 0, unroll=8)

    @pl.when(i == 0)
    def _():
        issue(0, 0)

    @pl.when(i + 1 < pl.num_programs(0))
    def _():
        issue(i + 1, 1 - slot)

    def wait(r, carry):
        _row_copy(h_hbm, 0, buf.at[slot], r, sem.at[slot]).wait()
        return carry

    lax.fori_loop(0, tg, wait, 0, unroll=8)
    o_ref[...] = buf[slot].astype(o_ref.dtype)


def moe_gather(h, row_tok):
    rows = row_tok.shape[0]
    d = h.shape[1]
    tg = GATHER_TILE
    return pl.pallas_call(
        _gather_kernel,
        grid_spec=pltpu.PrefetchScalarGridSpec(
            num_scalar_prefetch=1,
            grid=(rows // tg,),
            in_specs=[pl.BlockSpec(memory_space=pl.ANY)],
            out_specs=pl.BlockSpec((tg, d), lambda i, tok: (i, 0)),
            scratch_shapes=[pltpu.VMEM((2, tg, d), h.dtype), pltpu.SemaphoreType.DMA((2,))]),
        out_shape=jax.ShapeDtypeStruct((rows, d), BF16),
        compiler_params=_cparams("arbitrary"),
        name="moe_gather",
    )(row_tok, h)


def _moe_mm_kernel(*refs, n_w, layer, bn, has_scale):
    te_ref, first_ref, nxt_ref, nu_ref, a_ref = refs[:5]
    w_hbm = refs[5:5 + n_w]
    rw_ref = refs[5 + n_w] if has_scale else None
    o_ref = refs[5 + n_w + int(has_scale)]
    stage = refs[-1 - 2 * n_w:-1 - n_w]
    wb = refs[-1 - n_w:-1]
    sem = refs[-1]
    j = pl.program_id(0)
    i = pl.program_id(1)

    def copy(e, jj, t):
        cols = pl.ds(pl.multiple_of(jj * bn, LANES), bn)
        return pltpu.make_async_copy(w_hbm[t].at[layer, e, :, cols], stage[t], sem.at[t])

    @pl.when(first_ref[i] == 1)
    def _():
        @pl.when((j == 0) & (i == 0))
        def _():
            for t in range(n_w):
                copy(te_ref[0], 0, t).start()

        for t in range(n_w):
            copy(te_ref[i], j, t).wait()
            wb[t][...] = stage[t][...].astype(BF16)

        nxt = nxt_ref[i]

        @pl.when(nxt >= 0)
        def _():
            for t in range(n_w):
                copy(nxt, j, t).start()

        @pl.when((nxt < 0) & (j + 1 < pl.num_programs(0)))
        def _():
            for t in range(n_w):
                copy(te_ref[0], j + 1, t).start()

    used = i < nu_ref[0]

    @pl.when(used)
    def _():
        a = a_ref[...]
        acc = _dot(a, wb[0][...])
        if n_w == 2:
            acc = _silu(acc) * _dot(a, wb[1][...])
        if has_scale:
            acc = rw_ref[...] * acc
        o_ref[...] = acc.astype(o_ref.dtype)

    @pl.when(jnp.logical_not(used))
    def _():
        o_ref[...] = jnp.zeros_like(o_ref)


def moe_matmul(a, ws, layer, tables, out_dtype, bn, row_scale=None):
    tile_e, first, nxt, n_used = tables
    rows, k = a.shape
    n = ws[0].shape[-1]
    n_w = len(ws)
    tm = MOE_TILE
    bn = _tile(n, bn)
    idx = lambda j, i, *_: (i, 0)
    in_specs = [pl.BlockSpec((tm, k), idx)] + [pl.BlockSpec(memory_space=pl.ANY)] * n_w
    args = [a, *ws]
    if row_scale is not None:
        in_specs.append(pl.BlockSpec((tm, 1), idx))
        args.append(row_scale)
    scratch = ([pltpu.VMEM((k, bn), F32)] * n_w + [pltpu.VMEM((k, bn), BF16)] * n_w
               + [pltpu.SemaphoreType.DMA((n_w,))])
    return pl.pallas_call(
        functools.partial(_moe_mm_kernel, n_w=n_w, layer=layer, bn=bn, has_scale=row_scale is not None),
        grid_spec=pltpu.PrefetchScalarGridSpec(
            num_scalar_prefetch=4,
            grid=(n // bn, rows // tm),
            in_specs=in_specs,
            out_specs=pl.BlockSpec((tm, bn), lambda j, i, *_: (i, j)),
            scratch_shapes=scratch),
        out_shape=jax.ShapeDtypeStruct((rows, n), out_dtype),
        compiler_params=_cparams("arbitrary", "arbitrary"),
        name="moe_matmul",
    )(tile_e, first, nxt, n_used, *args)


def _combine_kernel(pos_ref, x_ref, y_hbm, g_ref, o_ref, buf, sem, *, apply_norm):
    tb = x_ref.shape[0]
    i = pl.program_id(0)
    slot = i % 2

    def issue(tile, dst_slot):
        def start(r, carry):
            for s in range(TOP_K):
                _row_copy(y_hbm, pos_ref[(tile * tb + r) * TOP_K + s], buf.at[dst_slot, s], r,
                          sem.at[dst_slot]).start()
            return carry
        lax.fori_loop(0, tb, start, 0, unroll=4)

    @pl.when(i == 0)
    def _():
        issue(0, 0)

    @pl.when(i + 1 < pl.num_programs(0))
    def _():
        issue(i + 1, 1 - slot)

    def wait(r, carry):
        for s in range(TOP_K):
            _row_copy(y_hbm, 0, buf.at[slot, s], r, sem.at[slot]).wait()
        return carry

    lax.fori_loop(0, tb, wait, 0, unroll=4)
    x = x_ref[...]
    for s in range(TOP_K):
        x = x + buf[slot, s]
    if apply_norm:
        x = x * lax.rsqrt(jnp.mean(x * x, axis=-1, keepdims=True) + EPS) * g_ref[...]
    o_ref[...] = x.astype(o_ref.dtype)


def moe_combine(x, y, pos, final_g, apply_norm):
    m, d = x.shape
    tb = GATHER_TILE
    return pl.pallas_call(
        functools.partial(_combine_kernel, apply_norm=apply_norm),
        grid_spec=pltpu.PrefetchScalarGridSpec(
            num_scalar_prefetch=1,
            grid=(m // tb,),
            in_specs=[pl.BlockSpec((tb, d), lambda i, p: (i, 0)),
                      pl.BlockSpec(memory_space=pl.ANY),
                      pl.BlockSpec((1, d), lambda i, p: (0, 0))],
            out_specs=pl.BlockSpec((tb, d), lambda i, p: (i, 0)),
            scratch_shapes=[pltpu.VMEM((2, TOP_K, tb, d), y.dtype), pltpu.SemaphoreType.DMA((2,))]),
        out_shape=jax.ShapeDtypeStruct((m, d), x.dtype),
        compiler_params=_cparams("arbitrary"),
        name="moe_combine_norm",
    )(pos, x, y, final_g.reshape(1, d).astype(F32))


def _rope_tables(positions, dim, pad_to):
    inv = ROPE_THETA ** (-jnp.arange(0, dim, 2, dtype=F32) / dim)
    ang = positions.astype(F32).reshape(-1)[:, None] * inv
    cos, sin = jnp.cos(ang), jnp.sin(ang)
    pad = jnp.zeros((ang.shape[0], pad_to - dim), F32)
    return (jnp.concatenate([cos, cos, pad], axis=1), jnp.concatenate([-sin, sin, pad], axis=1))


def _swap_halves(w):
    half = w.shape[-1] // 2
    return jnp.concatenate([w[..., half:], w[..., :half]], axis=-1)


def kernel(x, positions, ln_mix, w_in, ret_norm, hgrn_norm, hgrn_lb_logits, mla_q_norm, mla_w_uq,
           mla_kv_norm, mla_w_ukv, w_out, ln_ffn, ffn_w1, ffn_w3, ffn_w2, moe_router_w, moe_w1,
           moe_w3, moe_w2, final_norm):
    batch, seq, d_model = x.shape
    depth = w_in.shape[0]
    q_rank, kv_rank = mla_q_norm.shape[1], mla_kv_norm.shape[1]
    ret_w = 2 * RET_HEADS * RET_QK_DIM + 2 * d_model
    hg_e = HGRN_HEADS * HGRN_EXPAND
    hg_w = 2 * hg_e + 2 * d_model
    mla_w = q_rank + kv_rank + MLA_ROPE_DIM
    mla_pad = -(-(q_rank + kv_rank + LANES) // (2 * LANES)) * (2 * LANES)
    w_in_t = jnp.swapaxes(w_in, 1, 2)

    cos_r, sin_r = _rope_tables(positions, RET_QK_DIM, RET_QK_DIM)
    cm, sm = _rope_tables(positions, MLA_ROPE_DIM, LANES)
    lb_all = jnp.cumsum(jax.nn.softmax(hgrn_lb_logits.astype(F32), axis=0), axis=0)
    lb_all = lb_all - lb_all[:1]

    xf = x.reshape(batch * seq, d_model)
    for l in range(depth):
        last = l == depth - 1
        o = ret_w + hg_w
        h = rmsnorm(xf, ln_mix[l], BF16)
        r = staged_matmul(h, [w_in_t], l, 0, ret_w, BF16, transposed=True)
        hf = staged_matmul(h, [w_in_t], l, ret_w, hg_e, F32, transposed=True)
        hr = staged_matmul(h, [w_in_t], l, ret_w + hg_e, hg_w - hg_e, BF16, transposed=True)
        pm = staged_matmul(h, [w_in_t], l, o, mla_pad, F32, transposed=True, bn=mla_pad // 2)
        gates = staged_matmul(h, [w_in_t], l, o + mla_w, N_BRANCHES * d_model, BF16, epilogue="sigmoid",
                              transposed=True)

        y = retention(r, gates, cos_r, sin_r, ret_norm[l], batch, seq)
        y = hgrn2(hf, hr, gates, y, lb_all[l], hgrn_norm[l], batch, seq)

        cqn, ckvn, kr = mla_prep(pm, mla_q_norm[l], mla_kv_norm[l], cm, sm)
        wq = mla_w_uq[l].reshape(q_rank, MLA_HEADS, MLA_NOPE_DIM + MLA_ROPE_DIM)
        wq_rope = wq[..., MLA_NOPE_DIM:]
        wq = jnp.concatenate([wq, _swap_halves(wq_rope)], axis=-1).reshape(q_rank, -1).astype(BF16)
        q = matmul(cqn, wq, BF16, bn=MLA_UP_BN)
        kv = matmul(ckvn, mla_w_ukv[l].astype(BF16), BF16, bn=MLA_UP_BN)
        y = mla_attention(q, kv, kr, cm, sm, gates, y, batch, seq)
        xf = staged_matmul(y, [w_out], l, 0, d_model, F32, res=xf)

        if l % 2 == 0:
            j = l // 2
            h2 = rmsnorm(xf, ln_ffn[l], BF16)
            g = staged_matmul(h2, [ffn_w1, ffn_w3], j, 0, ffn_w1.shape[-1], BF16, epilogue="swiglu")
            xf = matmul_acc_res(g, ffn_w2[j].astype(BF16), xf, _tile(g.shape[1], FFN_DOWN_BK), bn=FFN_DOWN_BN)
            if last:
                xf = rmsnorm(xf, final_norm, x.dtype)
        else:
            j = l // 2
            h2, idx, wts = moe_router(xf, ln_ffn[l], moe_router_w[j])
            pos, row_tok, row_w, tables = _routing_tables(idx, wts, MOE_TILE)
            xs = moe_gather(h2, row_tok)
            g = moe_matmul(xs, [moe_w1, moe_w3], j, tables, BF16, MOE_UP_BN)
            yrows = moe_matmul(g, [moe_w2], j, tables, F32, MOE_DOWN_BN, row_scale=row_w)
            xf = moe_combine(xf, yrows, pos, final_norm, apply_norm=last)
    return xf.reshape(batch, seq, d_model)
```

```python
import functools

import jax
import jax.numpy as jnp
from jax import lax
from jax.experimental import pallas as pl
from jax.experimental.pallas import tpu as pltpu

F32 = jnp.float32
BF16 = jnp.bfloat16

RET_HEADS = 16
RET_QK_DIM = 128
HGRN_HEADS = 16
HGRN_EXPAND = 128
HGRN_CHUNK = 32
MLA_HEADS = 32
MLA_NOPE_DIM = 128
MLA_ROPE_DIM = 64
MLA_V_DIM = 128
N_BRANCHES = 3
N_EXPERTS = 8
TOP_K = 2
ROPE_THETA = 10000.0
EPS = 1e-6
LOG2_E = 1.4426950408889634

LANES = 128
SUBLANES = 8
VMEM_LIMIT_BYTES = 58 * 2 ** 20

ROW_TILE = 512
MM_BM = 1024
MM_BN = 512
MLA_UP_BN = 2048
RET_BLOCK = 512
HGRN_BLOCK = 256
MIXER_HEADS_PER_STEP = 4
ATTN_BLOCK = 512
ATTN_KV_BLOCK = 512
ATTN_HEADS_PER_STEP = 4
MOE_TILE = 512
MOE_UP_BN = 512
MOE_DOWN_BN = 1024
FFN_DOWN_BK = 3584
FFN_DOWN_BN = 1024
GATHER_TILE = 256


def _cparams(*sem):
    return pltpu.CompilerParams(dimension_semantics=sem, vmem_limit_bytes=VMEM_LIMIT_BYTES)


def _tile(n, preferred):
    for t in range(min(preferred, n) // LANES * LANES, 0, -LANES):
        if n % t == 0:
            return t
    return n


def _dot(a, b):
    return jnp.dot(a, b, preferred_element_type=F32)


def _dot_nt(a, b):
    return lax.dot_general(a, b, (((1,), (1,)), ((), ())), preferred_element_type=F32)


def _sigmoid(x):
    return 0.5 * jnp.tanh(0.5 * x) + 0.5


def _silu(x):
    return x * _sigmoid(x)


def _rmsnorm_kernel(x_ref, g_ref, o_ref):
    x = x_ref[...].astype(F32)
    ms = jnp.mean(x * x, axis=-1, keepdims=True)
    o_ref[...] = (x * lax.rsqrt(ms + EPS) * g_ref[...]).astype(o_ref.dtype)


def rmsnorm(x, g, out_dtype):
    m, d = x.shape
    rows = min(ROW_TILE, m)
    return pl.pallas_call(
        _rmsnorm_kernel,
        grid=(m // rows,),
        in_specs=[pl.BlockSpec((rows, d), lambda i: (i, 0)),
                  pl.BlockSpec((1, d), lambda i: (0, 0))],
        out_specs=pl.BlockSpec((rows, d), lambda i: (i, 0)),
        out_shape=jax.ShapeDtypeStruct((m, d), out_dtype),
        compiler_params=_cparams("parallel"),
        name="rmsnorm",
    )(x, g.reshape(1, d).astype(F32))


def _mm_kernel(*refs, act, has_res):
    a_ref, w_ref = refs[0], refs[1]
    o_ref = refs[-1]
    acc = _dot(a_ref[...], w_ref[...])
    if act == "sigmoid":
        acc = _sigmoid(acc)
    if has_res:
        acc = refs[2][...] + acc
    o_ref[...] = acc.astype(o_ref.dtype)


def matmul(a, w, out_dtype, act=None, res=None, bm=MM_BM, bn=MM_BN):
    m, k = a.shape
    n = w.shape[1]
    bm = _tile(m, bm)
    bn = _tile(n, bn)
    in_specs = [pl.BlockSpec((bm, k), lambda j, i: (i, 0)),
                pl.BlockSpec((k, bn), lambda j, i: (0, j))]
    args = [a, w]
    if res is not None:
        in_specs.append(pl.BlockSpec((bm, bn), lambda j, i: (i, j)))
        args.append(res)
    return pl.pallas_call(
        functools.partial(_mm_kernel, act=act, has_res=res is not None),
        grid=(n // bn, m // bm),
        in_specs=in_specs,
        out_specs=pl.BlockSpec((bm, bn), lambda j, i: (i, j)),
        out_shape=jax.ShapeDtypeStruct((m, n), out_dtype),
        compiler_params=_cparams("parallel", "parallel"),
        name="matmul",
    )(*args)


def _staged_mm_kernel(*refs, n_w, layer, col0, bn, epilogue, has_res, transposed):
    a_ref = refs[0]
    w_hbm = refs[1:1 + n_w]
    res_ref = refs[1 + n_w] if has_res else None
    o_ref = refs[1 + n_w + int(has_res)]
    stage = refs[-1 - 2 * n_w:-1 - n_w]
    wb = refs[-1 - n_w:-1]
    sem = refs[-1]
    j = pl.program_id(0)
    i = pl.program_id(1)

    def copy(jj, t):
        if transposed:
            rows = pl.ds(pl.multiple_of(col0 + jj * bn, SUBLANES), bn)
            return pltpu.make_async_copy(w_hbm[t].at[layer, rows, :], stage[t], sem.at[t])
        cols = pl.ds(pl.multiple_of(col0 + jj * bn, LANES), bn)
        return pltpu.make_async_copy(w_hbm[t].at[layer, :, cols], stage[t], sem.at[t])

    def cast(t):
        if not transposed:
            wb[t][...] = stage[t][...].astype(BF16)
            return
        k = wb[t].shape[0]
        step = _tile(k, 512)
        for c in range(0, k, step):
            wb[t][c:c + step, :] = stage[t][:, c:c + step].T.astype(BF16)

    @pl.when(i == 0)
    def _():
        @pl.when(j == 0)
        def _():
            for t in range(n_w):
                copy(0, t).start()

        for t in range(n_w):
            copy(j, t).wait()
            cast(t)

        @pl.when(j + 1 < pl.num_programs(0))
        def _():
            for t in range(n_w):
                copy(j + 1, t).start()

    a = a_ref[...]
    acc = _dot(a, wb[0][...])
    if epilogue == "swiglu":
        acc = _silu(acc) * _dot(a, wb[1][...])
    elif epilogue == "sigmoid":
        acc = _sigmoid(acc)
    if has_res:
        acc = res_ref[...] + acc
    o_ref[...] = acc.astype(o_ref.dtype)


def staged_matmul(a, ws, layer, col0, n, out_dtype, epilogue=None, res=None, transposed=False,
                  bm=MM_BM, bn=MM_BN):
    m, k = a.shape
    n_w = len(ws)
    bm = _tile(m, bm)
    bn = _tile(n, bn)
    assert col0 % (SUBLANES if transposed else LANES) == 0 and n % bn == 0
    in_specs = [pl.BlockSpec((bm, k), lambda j, i: (i, 0))]
    in_specs += [pl.BlockSpec(memory_space=pl.ANY)] * n_w
    args = [a, *ws]
    if res is not None:
        in_specs.append(pl.BlockSpec((bm, bn), lambda j, i: (i, j)))
        args.append(res)
    stage_shape = (bn, k) if transposed else (k, bn)
    scratch = ([pltpu.VMEM(stage_shape, F32)] * n_w + [pltpu.VMEM((k, bn), BF16)] * n_w
               + [pltpu.SemaphoreType.DMA((n_w,))])
    return pl.pallas_call(
        functools.partial(_staged_mm_kernel, n_w=n_w, layer=layer, col0=col0, bn=bn,
                          epilogue=epilogue, has_res=res is not None, transposed=transposed),
        grid=(n // bn, m // bm),
        in_specs=in_specs,
        out_specs=pl.BlockSpec((bm, bn), lambda j, i: (i, j)),
        out_shape=jax.ShapeDtypeStruct((m, n), out_dtype),
        scratch_shapes=scratch,
        compiler_params=_cparams("arbitrary", "arbitrary"),
        name="staged_matmul",
    )(*args)


def _mm_acc_kernel(a_ref, w_ref, res_ref, o_ref, acc_ref):
    kk = pl.program_id(2)

    @pl.when(kk == 0)
    def _():
        acc_ref[...] = jnp.zeros_like(acc_ref)

    acc_ref[...] += _dot(a_ref[...], w_ref[...])

    @pl.when(kk == pl.num_programs(2) - 1)
    def _():
        o_ref[...] = (res_ref[...] + acc_ref[...]).astype(o_ref.dtype)


def matmul_acc_res(a, w, res, bk, bm=MM_BM, bn=MM_BN):
    m, k = a.shape
    n = w.shape[1]
    bm = _tile(m, bm)
    bn = _tile(n, bn)
    return pl.pallas_call(
        _mm_acc_kernel,
        grid=(n // bn, m // bm, k // bk),
        in_specs=[pl.BlockSpec((bm, bk), lambda j, i, kk: (i, kk)),
                  pl.BlockSpec((bk, bn), lambda j, i, kk: (kk, j)),
                  pl.BlockSpec((bm, bn), lambda j, i, kk: (i, j))],
        out_specs=pl.BlockSpec((bm, bn), lambda j, i, kk: (i, j)),
        out_shape=jax.ShapeDtypeStruct((m, n), res.dtype),
        scratch_shapes=[pltpu.VMEM((bm, bn), F32)],
        compiler_params=_cparams("parallel", "parallel", "arbitrary"),
        name="matmul_acc_res",
    )(a, w, res)


def _head_norm_gate(o, gain, g):
    y = o * lax.rsqrt(jnp.mean(o * o, axis=-1, keepdims=True) + EPS) * gain
    return y * _silu(g.astype(F32))


def _retention_kernel(q_ref, k_ref, v_ref, g_ref, mg_ref, cos_ref, sin_ref, dmat_ref, qdec_ref, kdec_ref,
                      cdec_ref, gain_ref, o_ref, st_ref):
    @pl.when(pl.program_id(2) == 0)
    def _():
        st_ref[...] = jnp.zeros_like(st_ref)

    cos = cos_ref[...]
    sin = sin_ref[...]
    dk = RET_QK_DIM
    half = dk // 2
    hp = st_ref.shape[0]
    dv = st_ref.shape[1]
    for hh in range(hp):
        qk_cols = slice(hh * dk, (hh + 1) * dk)
        v_cols = slice(hh * dv, (hh + 1) * dv)
        q = q_ref[:, qk_cols].astype(F32)
        k = k_ref[:, qk_cols].astype(F32)
        qr = q * cos + pltpu.roll(q, half, 1) * sin
        kr = (k * cos + pltpu.roll(k, half, 1) * sin) * (dk ** -0.5)
        v = v_ref[:, v_cols]
        scores = _dot_nt(qr.astype(BF16), kr.astype(BF16))
        st = st_ref[hh]
        cross = _dot_nt((qr * qdec_ref[hh]).astype(BF16), st.astype(BF16))
        v_t = v.astype(F32).T.astype(BF16)
        st_ref[hh] = cdec_ref[hh] * st + _dot(v_t, (kr * kdec_ref[hh]).astype(BF16))
        intra = _dot((scores * dmat_ref[hh]).astype(BF16), v)
        y = _head_norm_gate(intra + cross, gain_ref[:, v_cols], g_ref[:, v_cols])
        o_ref[:, v_cols] = (mg_ref[:, v_cols].astype(F32) * y).astype(o_ref.dtype)


def retention(r, gates, cos2, sin2, norm_g, batch, seq):
    heads, dk = RET_HEADS, RET_QK_DIM
    dv = (r.shape[1] - 2 * heads * dk) // (2 * heads)
    blk = min(RET_BLOCK, seq)
    nblk = seq // blk
    log_gamma = jnp.log(1.0 - jnp.exp2(-5.0 - jnp.arange(heads, dtype=F32)))
    idx = jnp.arange(blk, dtype=F32)
    rel = idx[:, None] - idx[None, :]
    dmat = jnp.where(rel >= 0, jnp.exp(log_gamma[:, None, None] * jnp.maximum(rel, 0.0)), 0.0)
    qdec = jnp.broadcast_to(jnp.exp(log_gamma[:, None] * (idx + 1.0))[:, :, None], (heads, blk, dk))
    kdec = jnp.broadcast_to(jnp.exp(log_gamma[:, None] * (blk - 1.0 - idx))[:, :, None], (heads, blk, dk))
    cdec = jnp.broadcast_to(jnp.exp(log_gamma * blk)[:, None, None], (heads, 1, dk))
    hp = MIXER_HEADS_PER_STEP
    groups = heads // hp
    assert (2 * heads * dk) % (hp * dv) == 0
    v_off = 2 * heads * dk // (hp * dv)
    g_off = v_off + groups
    row = lambda b, h, t: b * nblk + t
    return pl.pallas_call(
        _retention_kernel,
        grid=(batch, groups, nblk),
        in_specs=[pl.BlockSpec((blk, hp * dk), lambda b, h, t: (row(b, h, t), h)),
                  pl.BlockSpec((blk, hp * dk), lambda b, h, t: (row(b, h, t), groups + h)),
                  pl.BlockSpec((blk, hp * dv), lambda b, h, t: (row(b, h, t), v_off + h)),
                  pl.BlockSpec((blk, hp * dv), lambda b, h, t: (row(b, h, t), g_off + h)),
                  pl.BlockSpec((blk, hp * dv), lambda b, h, t: (row(b, h, t), h)),
                  pl.BlockSpec((blk, dk), lambda b, h, t: (row(b, h, t), 0)),
                  pl.BlockSpec((blk, dk), lambda b, h, t: (row(b, h, t), 0)),
                  pl.BlockSpec((hp, blk, blk), lambda b, h, t: (h, 0, 0)),
                  pl.BlockSpec((hp, blk, dk), lambda b, h, t: (h, 0, 0)),
                  pl.BlockSpec((hp, blk, dk), lambda b, h, t: (h, 0, 0)),
                  pl.BlockSpec((hp, 1, dk), lambda b, h, t: (h, 0, 0)),
                  pl.BlockSpec((1, hp * dv), lambda b, h, t: (0, h))],
        out_specs=pl.BlockSpec((blk, hp * dv), lambda b, h, t: (row(b, h, t), h)),
        out_shape=jax.ShapeDtypeStruct((batch * seq, heads * dv), BF16),
        scratch_shapes=[pltpu.VMEM((hp, dv, dk), F32)],
        compiler_params=_cparams("parallel", "parallel", "arbitrary"),
        name="retention",
    )(r, r, r, r, gates, cos2, sin2, dmat, qdec, kdec, cdec, norm_g.reshape(1, -1).astype(F32))


def _hgrn_kernel(f_ref, q_ref, i_ref, og_ref, mg_ref, y_ref, la_ref, l1_ref, oml_ref, gain_ref, o_ref,
                 st_ref):
    @pl.when(pl.program_id(2) == 0)
    def _():
        st_ref[...] = jnp.zeros_like(st_ref)

    blk = f_ref.shape[0]
    hp, dv, e = st_ref.shape
    c = HGRN_CHUNK
    nch = blk // c
    row = lax.broadcasted_iota(jnp.int32, (blk, e), 0)
    row_in_chunk = row % c
    chunk_of_row = row // c
    r2 = lax.broadcasted_iota(jnp.int32, (blk, blk), 0)
    c2 = lax.broadcasted_iota(jnp.int32, (blk, blk), 1)
    keep = (r2 // c == c2 // c) & (c2 <= r2)

    prep = []
    for hh in range(hp):
        e_cols = slice(hh * e, (hh + 1) * e)
        z = f_ref[:, e_cols]
        log_sig = jnp.minimum(z, 0.0) - jnp.log1p(jnp.exp(-jnp.abs(z)))
        la = la_ref[:, e_cols]
        u = l1_ref[:, e_cols] + log_sig
        log_f = jnp.maximum(la, u) + jnp.log1p(jnp.exp(-jnp.abs(la - u)))
        key = oml_ref[:, e_cols] * (1.0 / (1.0 + jnp.exp(z)))
        b = log_f
        shift = 1
        while shift < c:
            b = b + jnp.where(row_in_chunk >= shift, pltpu.roll(b, shift, 0), 0.0)
            shift *= 2
        b3 = b.reshape(nch, c, e)
        b_mid = jnp.broadcast_to(b3[:, c // 2 - 1:c // 2, :], (nch, c, e)).reshape(blk, e)
        b_end_rows = b3[:, c - 1:c, :]
        b_end = jnp.broadcast_to(b_end_rows, (nch, c, e)).reshape(blk, e)
        q = q_ref[:, e_cols].astype(F32)
        v = i_ref[:, hh * dv:(hh + 1) * dv]
        qf = (q * jnp.exp(b - b_mid)).astype(BF16)
        kf = (key * jnp.exp(b_mid - b)).astype(BF16)
        q_in = (q * jnp.exp(b)).astype(BF16)
        k_out = (key * jnp.exp(b_end - b)).astype(BF16)
        zero = jnp.zeros_like(q_in)
        k_wide = jnp.concatenate([jnp.where(chunk_of_row == j, k_out, zero) for j in range(nch)], axis=1)
        q_wide = jnp.concatenate([jnp.where(chunk_of_row == j, q_in, zero) for j in range(nch)], axis=1)
        v_t = v.astype(F32).T.astype(BF16)
        prep.append((qf, kf, k_wide, q_wide, v, v_t, jnp.exp(b_end_rows)))

    scores = [_dot_nt(p[0], p[1]) for p in prep]
    upds = [_dot(p[5], p[2]) for p in prep]
    for hh in range(hp):
        _, _, _, q_wide, v, _, dec = prep[hh]
        intra = _dot(jnp.where(keep, scores[hh], 0.0).astype(BF16), v)
        st = st_ref[hh]
        states = []
        for j in range(nch):
            states.append(st.astype(BF16))
            st = st * dec[j] + upds[hh][:, j * e:(j + 1) * e]
        st_ref[hh] = st
        inter = _dot_nt(q_wide, jnp.concatenate(states, axis=1))
        v_cols = slice(hh * dv, (hh + 1) * dv)
        y = _head_norm_gate(intra + inter, gain_ref[:, v_cols], og_ref[:, v_cols])
        o_ref[:, v_cols] = (y_ref[:, v_cols].astype(F32) + mg_ref[:, v_cols].astype(F32) * y).astype(o_ref.dtype)


def hgrn2(hf, hr, gates, y_prev, lb, norm_g, batch, seq):
    heads, e = HGRN_HEADS, HGRN_EXPAND
    dv = (hr.shape[1] - heads * e) // (2 * heads)
    blk = min(HGRN_BLOCK, seq)
    nblk = seq // blk
    lb = lb.astype(F32).reshape(1, -1)
    la, l1, oml = jnp.log(lb), jnp.log1p(-lb), 1.0 - lb
    hp = MIXER_HEADS_PER_STEP
    groups = heads // hp
    assert (heads * e) % (hp * dv) == 0
    i_off = heads * e // (hp * dv)
    og_off = i_off + groups
    row = lambda b, h, t: b * nblk + t
    vec = pl.BlockSpec((1, hp * e), lambda b, h, t: (0, h))
    return pl.pallas_call(
        _hgrn_kernel,
        grid=(batch, groups, nblk),
        in_specs=[pl.BlockSpec((blk, hp * e), lambda b, h, t: (row(b, h, t), h)),
                  pl.BlockSpec((blk, hp * e), lambda b, h, t: (row(b, h, t), h)),
                  pl.BlockSpec((blk, hp * dv), lambda b, h, t: (row(b, h, t), i_off + h)),
                  pl.BlockSpec((blk, hp * dv), lambda b, h, t: (row(b, h, t), og_off + h)),
                  pl.BlockSpec((blk, hp * dv), lambda b, h, t: (row(b, h, t), groups + h)),
                  pl.BlockSpec((blk, hp * dv), lambda b, h, t: (row(b, h, t), h)),
                  vec, vec, vec,
                  pl.BlockSpec((1, hp * dv), lambda b, h, t: (0, h))],
        out_specs=pl.BlockSpec((blk, hp * dv), lambda b, h, t: (row(b, h, t), h)),
        out_shape=jax.ShapeDtypeStruct((batch * seq, heads * dv), BF16),
        scratch_shapes=[pltpu.VMEM((hp, dv, e), F32)],
        compiler_params=_cparams("parallel", "parallel", "arbitrary"),
        name="hgrn2",
    )(hf, hr, hr, hr, gates, y_prev, la, l1, oml, norm_g.reshape(1, -1).astype(F32))


def _mla_prep_kernel(p_ref, qg_ref, kvg_ref, cm_ref, sm_ref, cq_ref, ckv_ref, kr_ref, *, q_rank, kv_rank):
    p = p_ref[...]

    def norm(x, g):
        return x * lax.rsqrt(jnp.mean(x * x, axis=-1, keepdims=True) + EPS) * g

    cq_ref[...] = norm(p[:, :q_rank], qg_ref[...]).astype(cq_ref.dtype)
    ckv_ref[...] = norm(p[:, q_rank:q_rank + kv_rank], kvg_ref[...]).astype(ckv_ref.dtype)
    kr = p[:, q_rank + kv_rank:q_rank + kv_rank + LANES]
    lane = lax.broadcasted_iota(jnp.int32, kr.shape, 1)
    half = MLA_ROPE_DIM // 2
    swapped = jnp.where(lane < MLA_ROPE_DIM + half, pltpu.roll(kr, half, 1),
                        pltpu.roll(kr, MLA_ROPE_DIM + half, 1))
    pair = jnp.where(lane < MLA_ROPE_DIM, kr, swapped)
    kr_ref[...] = (pair * cm_ref[...] + pltpu.roll(pair, MLA_ROPE_DIM, 1) * sm_ref[...]).astype(kr_ref.dtype)


def mla_prep(pm, q_norm, kv_norm, cm, sm):
    m = pm.shape[0]
    q_rank, kv_rank = q_norm.shape[0], kv_norm.shape[0]
    rows = min(ROW_TILE, m)
    full = lambda w: pl.BlockSpec((rows, w), lambda i: (i, 0))
    return pl.pallas_call(
        functools.partial(_mla_prep_kernel, q_rank=q_rank, kv_rank=kv_rank),
        grid=(m // rows,),
        in_specs=[full(pm.shape[1]),
                  pl.BlockSpec((1, q_rank), lambda i: (0, 0)),
                  pl.BlockSpec((1, kv_rank), lambda i: (0, 0)),
                  full(LANES), full(LANES)],
        out_specs=[full(q_rank), full(kv_rank), full(LANES)],
        out_shape=[jax.ShapeDtypeStruct((m, q_rank), BF16),
                   jax.ShapeDtypeStruct((m, kv_rank), BF16),
                   jax.ShapeDtypeStruct((m, LANES), BF16)],
        compiler_params=_cparams("parallel"),
        name="mla_prep",
    )(pm, q_norm.reshape(1, -1).astype(F32), kv_norm.reshape(1, -1).astype(F32), cm, sm)


def _attn_kernel(q_ref, kv_ref, kr_ref, cm_ref, sm_ref, mg_ref, y_ref, o_ref, kcat_ref, vt_ref, *, scale,
                 heads_per_step, tk):
    i = pl.program_id(2)
    tq = q_ref.shape[0]
    kv_w = MLA_NOPE_DIM + MLA_V_DIM
    q_w = MLA_NOPE_DIM + 2 * MLA_ROPE_DIM

    @pl.when(i == 0)
    def _():
        for hh in range(heads_per_step):
            kcat_ref[hh, :, :MLA_NOPE_DIM] = kv_ref[:, hh * kv_w:hh * kv_w + MLA_NOPE_DIM]
            kcat_ref[hh, :, MLA_NOPE_DIM:] = kr_ref[...]
            for jj in range(vt_ref.shape[1]):
                v = kv_ref[jj * tk:(jj + 1) * tk, hh * kv_w + MLA_NOPE_DIM:(hh + 1) * kv_w]
                vt_ref[hh, jj] = v.astype(F32).T.astype(BF16)

    qfs = []
    for hh in range(heads_per_step):
        qb = q_ref[:, hh * q_w:(hh + 1) * q_w].astype(F32)
        qr = qb[:, MLA_NOPE_DIM:]
        qr = qr * cm_ref[...] + pltpu.roll(qr, MLA_ROPE_DIM, 1) * sm_ref[...]
        qfs.append((jnp.concatenate([qb[:, :MLA_NOPE_DIM], qr], axis=1) * (scale * LOG2_E)).astype(BF16))

    def step(j, carry, masked):
        rows = pl.ds(pl.multiple_of(j * tk, tk), tk)
        out = []
        scores = [_dot_nt(kcat_ref[hh, rows, :], qfs[hh]) for hh in range(heads_per_step)]
        for hh in range(heads_per_step):
            m, l, acc = carry[hh]
            s = scores[hh]
            if masked:
                kpos = lax.broadcasted_iota(jnp.int32, s.shape, 0) + j * tk
                qpos = lax.broadcasted_iota(jnp.int32, s.shape, 1) + i * tq
                s = jnp.where(kpos <= qpos, s, -jnp.inf)
            m_new = jnp.maximum(m, jnp.max(s, axis=0, keepdims=True))
            p = jnp.exp2(s - m_new)
            alpha = jnp.exp2(m - m_new)
            l = alpha * l + jnp.sum(p, axis=0, keepdims=True)
            acc = alpha * acc + _dot(vt_ref[hh, j], p.astype(BF16))
            out.append((m_new, l, acc))
        return tuple(out)

    init = tuple((jnp.full((1, tq), -jnp.inf, F32), jnp.zeros((1, tq), F32), jnp.zeros((MLA_V_DIM, tq), F32))
                 for _ in range(heads_per_step))
    per_q = tq // tk
    carry = lax.fori_loop(0, i * per_q, lambda j, cr: step(j, cr, False), init)
    for d in range(per_q):
        carry = step(i * per_q + d, carry, True)
    for hh in range(heads_per_step):
        _, l, acc = carry[hh]
        cols = slice(hh * MLA_V_DIM, (hh + 1) * MLA_V_DIM)
        y = y_ref[:, cols].astype(F32) + mg_ref[:, cols].astype(F32) * (acc / l).T
        o_ref[:, cols] = y.astype(o_ref.dtype)


def mla_attention(q, kv, kr, cm, sm, gates, y_prev, batch, seq):
    heads = MLA_HEADS
    hp = ATTN_HEADS_PER_STEP
    tq = min(ATTN_BLOCK, seq)
    tk = min(ATTN_KV_BLOCK, tq)
    nq = seq // tq
    scale = (MLA_NOPE_DIM + MLA_ROPE_DIM) ** -0.5
    qw = MLA_NOPE_DIM + 2 * MLA_ROPE_DIM
    kvw = MLA_NOPE_DIM + MLA_V_DIM
    return pl.pallas_call(
        functools.partial(_attn_kernel, scale=scale, heads_per_step=hp, tk=tk),
        grid=(batch, heads // hp, nq),
        in_specs=[pl.BlockSpec((tq, hp * qw), lambda b, h, i: (b * nq + i, h)),
                  pl.BlockSpec((seq, hp * kvw), lambda b, h, i: (b, h)),
                  pl.BlockSpec((seq, LANES), lambda b, h, i: (b, 0)),
                  pl.BlockSpec((tq, LANES), lambda b, h, i: (b * nq + i, 0)),
                  pl.BlockSpec((tq, LANES), lambda b, h, i: (b * nq + i, 0)),
                  pl.BlockSpec((tq, hp * MLA_V_DIM), lambda b, h, i: (b * nq + i, 2 * (heads // hp) + h)),
                  pl.BlockSpec((tq, hp * MLA_V_DIM), lambda b, h, i: (b * nq + i, h))],
        out_specs=pl.BlockSpec((tq, hp * MLA_V_DIM), lambda b, h, i: (b * nq + i, h)),
        out_shape=jax.ShapeDtypeStruct((batch * seq, heads * MLA_V_DIM), BF16),
        scratch_shapes=[pltpu.VMEM((hp, seq, MLA_NOPE_DIM + LANES), BF16),
                        pltpu.VMEM((hp, seq // tk, MLA_V_DIM, tk), BF16)],
        compiler_params=_cparams("parallel", "parallel", "arbitrary"),
        name="mla_attention",
    )(q, kv, kr, cm, sm, gates, y_prev)


def _router_kernel(x_ref, g_ref, r_ref, h_ref, idx_ref, w_ref):
    x = x_ref[...]
    h = x * lax.rsqrt(jnp.mean(x * x, axis=-1, keepdims=True) + EPS) * g_ref[...]
    h_ref[...] = h
    logits = jnp.dot(h, r_ref[...], preferred_element_type=F32, precision=lax.Precision.HIGHEST)
    lane = lax.broadcasted_iota(jnp.int32, logits.shape, 1)
    lg = jnp.where(lane < N_EXPERTS, logits, -jnp.inf)
    m1 = jnp.max(lg, axis=-1, keepdims=True)
    i1 = jnp.min(jnp.where(lg == m1, lane, LANES), axis=-1, keepdims=True)
    lg2 = jnp.where(lane == i1, -jnp.inf, lg)
    m2 = jnp.max(lg2, axis=-1, keepdims=True)
    i2 = jnp.min(jnp.where(lg2 == m2, lane, LANES), axis=-1, keepdims=True)
    e2 = jnp.exp(m2 - m1)
    w1 = 1.0 / (1.0 + e2)
    w2 = e2 / (1.0 + e2)
    idx_ref[...] = jnp.where(lane == 0, i1, i2)
    w_ref[...] = jnp.where(lane == 0, w1, w2)


def moe_router(x, g, router):
    m, d = x.shape
    rows = min(ROW_TILE, m)
    r_pad = jnp.zeros((d, LANES), F32).at[:, :N_EXPERTS].set(router.astype(F32))
    h, idx, w = pl.pallas_call(
        _router_kernel,
        grid=(m // rows,),
        in_specs=[pl.BlockSpec((rows, d), lambda i: (i, 0)),
                  pl.BlockSpec((1, d), lambda i: (0, 0)),
                  pl.BlockSpec((d, LANES), lambda i: (0, 0))],
        out_specs=[pl.BlockSpec((rows, d), lambda i: (i, 0)),
                   pl.BlockSpec((rows, LANES), lambda i: (i, 0)),
                   pl.BlockSpec((rows, LANES), lambda i: (i, 0))],
        out_shape=[jax.ShapeDtypeStruct((m, d), F32),
                   jax.ShapeDtypeStruct((m, LANES), jnp.int32),
                   jax.ShapeDtypeStruct((m, LANES), F32)],
        compiler_params=_cparams("parallel"),
        name="moe_router",
    )(x, g.reshape(1, d).astype(F32), r_pad)
    return h, idx[:, :TOP_K], w[:, :TOP_K]


def _routing_tables(idx, wts, tile):
    t = idx.shape[0]
    pairs = t * TOP_K
    e = idx.reshape(pairs)
    onehot = (e[:, None] == jnp.arange(N_EXPERTS, dtype=jnp.int32)[None, :]).astype(jnp.int32)
    csum = jnp.cumsum(onehot, axis=0)
    rank = jnp.sum(csum * onehot, axis=1) - 1
    counts = csum[-1]
    padded = ((counts + tile - 1) // tile) * tile
    ends = jnp.cumsum(padded)
    starts = ends - padded
    pos = (jnp.sum(onehot * starts[None, :], axis=1) + rank).astype(jnp.int32)
    rows = pairs + N_EXPERTS * tile
    row_tok = jnp.zeros((rows,), jnp.int32).at[pos].set(jnp.arange(pairs, dtype=jnp.int32) // TOP_K)
    row_w = jnp.zeros((rows,), F32).at[pos].set(wts.reshape(pairs))
    tile_start = jnp.arange(rows // tile, dtype=jnp.int32) * tile
    tile_e = jnp.minimum(jnp.sum((tile_start[:, None] >= ends[None, :]).astype(jnp.int32), axis=1),
                         N_EXPERTS - 1).astype(jnp.int32)
    n_used = (ends[-1] // tile).astype(jnp.int32).reshape(1)
    n_tiles = rows // tile
    t_idx = jnp.arange(n_tiles, dtype=jnp.int32)
    prev_e = jnp.concatenate([jnp.full((1,), -1, jnp.int32), tile_e[:-1]])
    first = ((tile_e != prev_e) & (t_idx < n_used[0])).astype(jnp.int32)
    later_first = (t_idx[None, :] > t_idx[:, None]) & (first[None, :] == 1)
    next_idx = jnp.min(jnp.where(later_first, t_idx[None, :], n_tiles), axis=1)
    nxt = jnp.where(next_idx < n_tiles, tile_e[jnp.minimum(next_idx, n_tiles - 1)], -1).astype(jnp.int32)
    return pos, row_tok, row_w.reshape(rows, 1), (tile_e, first, nxt, n_used)


def _row_copy(src_hbm, src_row, dst, dst_row, sem):
    return pltpu.make_async_copy(src_hbm.at[pl.ds(src_row, 1), :], dst.at[pl.ds(dst_row, 1), :], sem)


def _gather_kernel(tok_ref, h_hbm, o_ref, buf, sem):
    tg = buf.shape[1]
    i = pl.program_id(0)
    slot = i % 2

    def issue(tile, dst_slot):
        def start(r, carry):
            _row_copy(h_hbm, tok_ref[tile * tg + r], buf.at[dst_slot], r, sem.at[dst_slot]).start()
            return carry
        lax.fori_loop(0, tg, start, 0, unroll=8)

    @pl.when(i == 0)
    def _():
        issue(0, 0)

    @pl.when(i + 1 < pl.num_programs(0))
    def _():
        issue(i + 1, 1 - slot)

    def wait(r, carry):
        _row_copy(h_hbm, 0, buf.at[slot], r, sem.at[slot]).wait()
        return carry

    lax.fori_loop(0, tg, wait, 0, unroll=8)
    o_ref[...] = buf[slot].astype(o_ref.dtype)


def moe_gather(h, row_tok):
    rows = row_tok.shape[0]
    d = h.shape[1]
    tg = GATHER_TILE
    return pl.pallas_call(
        _gather_kernel,
        grid_spec=pltpu.PrefetchScalarGridSpec(
            num_scalar_prefetch=1,
            grid=(rows // tg,),
            in_specs=[pl.BlockSpec(memory_space=pl.ANY)],
            out_specs=pl.BlockSpec((tg, d), lambda i, tok: (i, 0)),
            scratch_shapes=[pltpu.VMEM((2, tg, d), h.dtype), pltpu.SemaphoreType.DMA((2,))]),
        out_shape=jax.ShapeDtypeStruct((rows, d), BF16),
        compiler_params=_cparams("arbitrary"),
        name="moe_gather",
    )(row_tok, h)


def _moe_mm_kernel(*refs, n_w, layer, bn, has_scale):
    te_ref, first_ref, nxt_ref, nu_ref, a_ref = refs[:5]
    w_hbm = refs[5:5 + n_w]
    rw_ref = refs[5 + n_w] if has_scale else None
    o_ref = refs[5 + n_w + int(has_scale)]
    stage = refs[-1 - 2 * n_w:-1 - n_w]
    wb = refs[-1 - n_w:-1]
    sem = refs[-1]
    j = pl.program_id(0)
    i = pl.program_id(1)

    def copy(e, jj, t):
        cols = pl.ds(pl.multiple_of(jj * bn, LANES), bn)
        return pltpu.make_async_copy(w_hbm[t].at[layer, e, :, cols], stage[t], sem.at[t])

    @pl.when(first_ref[i] == 1)
    def _():
        @pl.when((j == 0) & (i == 0))
        def _():
            for t in range(n_w):
                copy(te_ref[0], 0, t).start()

        for t in range(n_w):
            copy(te_ref[i], j, t).wait()
            wb[t][...] = stage[t][...].astype(BF16)

        nxt = nxt_ref[i]

        @pl.when(nxt >= 0)
        def _():
            for t in range(n_w):
                copy(nxt, j, t).start()

        @pl.when((nxt < 0) & (j + 1 < pl.num_programs(0)))
        def _():
            for t in range(n_w):
                copy(te_ref[0], j + 1, t).start()

    used = i < nu_ref[0]

    @pl.when(used)
    def _():
        a = a_ref[...]
        acc = _dot(a, wb[0][...])
        if n_w == 2:
            acc = _silu(acc) * _dot(a, wb[1][...])
        if has_scale:
            acc = rw_ref[...] * acc
        o_ref[...] = acc.astype(o_ref.dtype)

    @pl.when(jnp.logical_not(used))
    def _():
        o_ref[...] = jnp.zeros_like(o_ref)


def moe_matmul(a, ws, layer, tables, out_dtype, bn, row_scale=None):
    tile_e, first, nxt, n_used = tables
    rows, k = a.shape
    n = ws[0].shape[-1]
    n_w = len(ws)
    tm = MOE_TILE
    bn = _tile(n, bn)
    idx = lambda j, i, *_: (i, 0)
    in_specs = [pl.BlockSpec((tm, k), idx)] + [pl.BlockSpec(memory_space=pl.ANY)] * n_w
    args = [a, *ws]
    if row_scale is not None:
        in_specs.append(pl.BlockSpec((tm, 1), idx))
        args.append(row_scale)
    scratch = ([pltpu.VMEM((k, bn), F32)] * n_w + [pltpu.VMEM((k, bn), BF16)] * n_w
               + [pltpu.SemaphoreType.DMA((n_w,))])
    return pl.pallas_call(
        functools.partial(_moe_mm_kernel, n_w=n_w, layer=layer, bn=bn, has_scale=row_scale is not None),
        grid_spec=pltpu.PrefetchScalarGridSpec(
            num_scalar_prefetch=4,
            grid=(n // bn, rows // tm),
            in_specs=in_specs,
            out_specs=pl.BlockSpec((tm, bn), lambda j, i, *_: (i, j)),
            scratch_shapes=scratch),
        out_shape=jax.ShapeDtypeStruct((rows, n), out_dtype),
        compiler_params=_cparams("arbitrary", "arbitrary"),
        name="moe_matmul",
    )(tile_e, first, nxt, n_used, *args)


def _combine_kernel(pos_ref, x_ref, y_hbm, g_ref, o_ref, buf, sem, *, apply_norm):
    tb = x_ref.shape[0]
    i = pl.program_id(0)
    slot = i % 2

    def issue(tile, dst_slot):
        def start(r, carry):
            for s in range(TOP_K):
                _row_copy(y_hbm, pos_ref[(tile * tb + r) * TOP_K + s], buf.at[dst_slot, s], r,
                          sem.at[dst_slot]).start()
            return carry
        lax.fori_loop(0, tb, start, 0, unroll=4)

    @pl.when(i == 0)
    def _():
        issue(0, 0)

    @pl.when(i + 1 < pl.num_programs(0))
    def _():
        issue(i + 1, 1 - slot)

    def wait(r, carry):
        for s in range(TOP_K):
            _row_copy(y_hbm, 0, buf.at[slot, s], r, sem.at[slot]).wait()
        return carry

    lax.fori_loop(0, tb, wait, 0, unroll=4)
    x = x_ref[...]
    for s in range(TOP_K):
        x = x + buf[slot, s]
    if apply_norm:
        x = x * lax.rsqrt(jnp.mean(x * x, axis=-1, keepdims=True) + EPS) * g_ref[...]
    o_ref[...] = x.astype(o_ref.dtype)


def moe_combine(x, y, pos, final_g, apply_norm):
    m, d = x.shape
    tb = GATHER_TILE
    return pl.pallas_call(
        functools.partial(_combine_kernel, apply_norm=apply_norm),
        grid_spec=pltpu.PrefetchScalarGridSpec(
            num_scalar_prefetch=1,
            grid=(m // tb,),
            in_specs=[pl.BlockSpec((tb, d), lambda i, p: (i, 0)),
                      pl.BlockSpec(memory_space=pl.ANY),
                      pl.BlockSpec((1, d), lambda i, p: (0, 0))],
            out_specs=pl.BlockSpec((tb, d), lambda i, p: (i, 0)),
            scratch_shapes=[pltpu.VMEM((2, TOP_K, tb, d), y.dtype), pltpu.SemaphoreType.DMA((2,))]),
        out_shape=jax.ShapeDtypeStruct((m, d), x.dtype),
        compiler_params=_cparams("arbitrary"),
        name="moe_combine_norm",
    )(pos, x, y, final_g.reshape(1, d).astype(F32))


def _rope_tables(positions, dim, pad_to):
    inv = ROPE_THETA ** (-jnp.arange(0, dim, 2, dtype=F32) / dim)
    ang = positions.astype(F32).reshape(-1)[:, None] * inv
    cos, sin = jnp.cos(ang), jnp.sin(ang)
    pad = jnp.zeros((ang.shape[0], pad_to - dim), F32)
    return (jnp.concatenate([cos, cos, pad], axis=1), jnp.concatenate([-sin, sin, pad], axis=1))


def _swap_halves(w):
    half = w.shape[-1] // 2
    return jnp.concatenate([w[..., half:], w[..., :half]], axis=-1)


def kernel(x, positions, ln_mix, w_in, ret_norm, hgrn_norm, hgrn_lb_logits, mla_q_norm, mla_w_uq,
           mla_kv_norm, mla_w_ukv, w_out, ln_ffn, ffn_w1, ffn_w3, ffn_w2, moe_router_w, moe_w1,
           moe_w3, moe_w2, final_norm):
    batch, seq, d_model = x.shape
    depth = w_in.shape[0]
    q_rank, kv_rank = mla_q_norm.shape[1], mla_kv_norm.shape[1]
    ret_w = 2 * RET_HEADS * RET_QK_DIM + 2 * d_model
    hg_e = HGRN_HEADS * HGRN_EXPAND
    hg_w = 2 * hg_e + 2 * d_model
    mla_w = q_rank + kv_rank + MLA_ROPE_DIM
    mla_pad = -(-(q_rank + kv_rank + LANES) // (2 * LANES)) * (2 * LANES)
    w_in_t = jnp.swapaxes(w_in, 1, 2)

    cos_r, sin_r = _rope_tables(positions, RET_QK_DIM, RET_QK_DIM)
    cm, sm = _rope_tables(positions, MLA_ROPE_DIM, LANES)
    lb_all = jnp.cumsum(jax.nn.softmax(hgrn_lb_logits.astype(F32), axis=0), axis=0)
    lb_all = lb_all - lb_all[:1]

    xf = x.reshape(batch * seq, d_model)
    for l in range(depth):
        last = l == depth - 1
        o = ret_w + hg_w
        h = rmsnorm(xf, ln_mix[l], BF16)
        r = staged_matmul(h, [w_in_t], l, 0, ret_w, BF16, transposed=True)
        hf = staged_matmul(h, [w_in_t], l, ret_w, hg_e, F32, transposed=True)
        hr = staged_matmul(h, [w_in_t], l, ret_w + hg_e, hg_w - hg_e, BF16, transposed=True)
        pm = staged_matmul(h, [w_in_t], l, o, mla_pad, F32, transposed=True, bn=mla_pad // 2)
        gates = staged_matmul(h, [w_in_t], l, o + mla_w, N_BRANCHES * d_model, BF16, epilogue="sigmoid",
                              transposed=True)

        y = retention(r, gates, cos_r, sin_r, ret_norm[l], batch, seq)
        y = hgrn2(hf, hr, gates, y, lb_all[l], hgrn_norm[l], batch, seq)

        cqn, ckvn, kr = mla_prep(pm, mla_q_norm[l], mla_kv_norm[l], cm, sm)
        wq = mla_w_uq[l].reshape(q_rank, MLA_HEADS, MLA_NOPE_DIM + MLA_ROPE_DIM)
        wq_rope = wq[..., MLA_NOPE_DIM:]
        wq = jnp.concatenate([wq, _swap_halves(wq_rope)], axis=-1).reshape(q_rank, -1).astype(BF16)
        q = matmul(cqn, wq, BF16, bn=MLA_UP_BN)
        kv = matmul(ckvn, mla_w_ukv[l].astype(BF16), BF16, bn=MLA_UP_BN)
        y = mla_attention(q, kv, kr, cm, sm, gates, y, batch, seq)
        xf = staged_matmul(y, [w_out], l, 0, d_model, F32, res=xf)

        if l % 2 == 0:
            j = l // 2
            h2 = rmsnorm(xf, ln_ffn[l], BF16)
            g = staged_matmul(h2, [ffn_w1, ffn_w3], j, 0, ffn_w1.shape[-1], BF16, epilogue="swiglu")
            xf = matmul_acc_res(g, ffn_w2[j].astype(BF16), xf, _tile(g.shape[1], FFN_DOWN_BK), bn=FFN_DOWN_BN)
            if last:
                xf = rmsnorm(xf, final_norm, x.dtype)
        else:
            j = l // 2
            h2, idx, wts = moe_router(xf, ln_ffn[l], moe_router_w[j])
            pos, row_tok, row_w, tables = _routing_tables(idx, wts, MOE_TILE)
            xs = moe_gather(h2, row_tok)
            g = moe_matmul(xs, [moe_w1, moe_w3], j, tables, BF16, MOE_UP_BN)
            yrows = moe_matmul(g, [moe_w2], j, tables, F32, MOE_DOWN_BN, row_scale=row_w)
            xf = moe_combine(xf, yrows, pos, final_norm, apply_norm=last)
    return xf.reshape(batch, seq, d_model)
```

```python
import functools

import jax
import jax.numpy as jnp
from jax import lax
from jax.experimental import pallas as pl
from jax.experimental.pallas import tpu as pltpu

F32 = jnp.float32
BF16 = jnp.bfloat16

RET_HEADS = 16
RET_QK_DIM = 128
HGRN_HEADS = 16
HGRN_EXPAND = 128
HGRN_CHUNK = 32
MLA_HEADS = 32
MLA_NOPE_DIM = 128
MLA_ROPE_DIM = 64
MLA_V_DIM = 128
N_BRANCHES = 3
N_EXPERTS = 8
TOP_K = 2
ROPE_THETA = 10000.0
EPS = 1e-6
LOG2_E = 1.4426950408889634
HIGH_HALF_MASK = -65536

LANES = 128
SUBLANES = 8
VMEM_LIMIT_BYTES = 58 * 2 ** 20

ROW_TILE = 512
MM_BM = 1024
MM_BN = 512
MLA_UP_BN = 2048
RET_BLOCK = 512
HGRN_BLOCK = 256
MIXER_HEADS_PER_STEP = 4
ATTN_BLOCK = 512
ATTN_KV_BLOCK = 512
ATTN_HEADS_PER_STEP = 4
MOE_TILE = 512
MOE_UP_BN = 512
MOE_DOWN_BN = 1024
FFN_DOWN_BK = 3584
FFN_DOWN_BN = 1024
GATHER_TILE = 256


def _cparams(*sem):
    return pltpu.CompilerParams(dimension_semantics=sem, vmem_limit_bytes=VMEM_LIMIT_BYTES)


def _tile(n, preferred):
    for t in range(min(preferred, n) // LANES * LANES, 0, -LANES):
        if n % t == 0:
            return t
    return n


def _dot(a, b):
    return jnp.dot(a, b, preferred_element_type=F32)


def _dot_nt(a, b):
    return lax.dot_general(a, b, (((1,), (1,)), ((), ())), preferred_element_type=F32)


def _sigmoid(x):
    return 0.5 * jnp.tanh(0.5 * x) + 0.5


def _silu(x):
    return x * _sigmoid(x)


def _rmsnorm_kernel(x_ref, g_ref, o_ref):
    x = x_ref[...].astype(F32)
    ms = jnp.mean(x * x, axis=-1, keepdims=True)
    o_ref[...] = (x * lax.rsqrt(ms + EPS) * g_ref[...]).astype(o_ref.dtype)


def rmsnorm(x, g, out_dtype):
    m, d = x.shape
    rows = min(ROW_TILE, m)
    return pl.pallas_call(
        _rmsnorm_kernel,
        grid=(m // rows,),
        in_specs=[pl.BlockSpec((rows, d), lambda i: (i, 0)),
                  pl.BlockSpec((1, d), lambda i: (0, 0))],
        out_specs=pl.BlockSpec((rows, d), lambda i: (i, 0)),
        out_shape=jax.ShapeDtypeStruct((m, d), out_dtype),
        compiler_params=_cparams("parallel"),
        name="rmsnorm",
    )(x, g.reshape(1, d).astype(F32))


def _mm_kernel(*refs, q_scale):
    a_ref, w_ref = refs[0], refs[1]
    o_ref = refs[-1]
    acc = _dot(a_ref[...], w_ref[...])
    if q_scale is None:
        o_ref[...] = acc.astype(o_ref.dtype)
        return
    cm, sm = refs[2][...], refs[3][...]
    head_w = MLA_NOPE_DIM + 2 * MLA_ROPE_DIM
    for c in range(0, acc.shape[1], head_w):
        qr = acc[:, c + MLA_NOPE_DIM:c + head_w]
        qr = qr * cm + pltpu.roll(qr, MLA_ROPE_DIM, 1) * sm
        o_ref[:, c:c + MLA_NOPE_DIM] = (acc[:, c:c + MLA_NOPE_DIM] * q_scale).astype(o_ref.dtype)
        o_ref[:, c + MLA_NOPE_DIM:c + head_w] = (qr * q_scale).astype(o_ref.dtype)


def matmul(a, w, out_dtype, bm=MM_BM, bn=MM_BN, q_rope=None):
    m, k = a.shape
    n = w.shape[1]
    bm = _tile(m, bm)
    bn = _tile(n, bn)
    in_specs = [pl.BlockSpec((bm, k), lambda j, i: (i, 0)),
                pl.BlockSpec((k, bn), lambda j, i: (0, j))]
    args = [a, w]
    if q_rope is not None:
        in_specs += [pl.BlockSpec((bm, LANES), lambda j, i: (i, 0))] * 2
        args += [q_rope[0], q_rope[1]]
    return pl.pallas_call(
        functools.partial(_mm_kernel, q_scale=None if q_rope is None else q_rope[2]),
        grid=(n // bn, m // bm),
        in_specs=in_specs,
        out_specs=pl.BlockSpec((bm, bn), lambda j, i: (i, j)),
        out_shape=jax.ShapeDtypeStruct((m, n), out_dtype),
        compiler_params=_cparams("parallel", "parallel"),
        name="matmul",
    )(*args)


def _staged_mm_kernel(*refs, n_w, layer, col0, bn, epilogue, has_res, transposed):
    a_ref = refs[0]
    w_hbm = refs[1:1 + n_w]
    res_ref = refs[1 + n_w] if has_res else None
    o_ref = refs[1 + n_w + int(has_res)]
    stage = refs[-1 - 2 * n_w:-1 - n_w]
    wb = refs[-1 - n_w:-1]
    sem = refs[-1]
    j = pl.program_id(0)
    i = pl.program_id(1)

    def copy(jj, t):
        if transposed:
            rows = pl.ds(pl.multiple_of(col0 + jj * bn, SUBLANES), bn)
            return pltpu.make_async_copy(w_hbm[t].at[layer, rows, :], stage[t], sem.at[t])
        cols = pl.ds(pl.multiple_of(col0 + jj * bn, LANES), bn)
        return pltpu.make_async_copy(w_hbm[t].at[layer, :, cols], stage[t], sem.at[t])

    def cast(t):
        if not transposed:
            wb[t][...] = stage[t][...].astype(BF16)
            return
        k = wb[t].shape[0]
        step = _tile(k, 512)
        for c in range(0, k, step):
            wb[t][c:c + step, :] = stage[t][:, c:c + step].T.astype(BF16)

    @pl.when(i == 0)
    def _():
        @pl.when(j == 0)
        def _():
            for t in range(n_w):
                copy(0, t).start()

        for t in range(n_w):
            copy(j, t).wait()
            cast(t)

        @pl.when(j + 1 < pl.num_programs(0))
        def _():
            for t in range(n_w):
                copy(j + 1, t).start()

    a = a_ref[...]
    acc = _dot(a, wb[0][...])
    if epilogue == "swiglu":
        acc = _silu(acc) * _dot(a, wb[1][...])
    elif epilogue == "sigmoid":
        acc = _sigmoid(acc)
    if has_res:
        acc = res_ref[...] + acc
    o_ref[...] = acc.astype(o_ref.dtype)


def staged_matmul(a, ws, layer, col0, n, out_dtype, epilogue=None, res=None, transposed=False,
                  bm=MM_BM, bn=MM_BN):
    m, k = a.shape
    n_w = len(ws)
    bm = _tile(m, bm)
    bn = _tile(n, bn)
    assert col0 % (SUBLANES if transposed else LANES) == 0 and n % bn == 0
    in_specs = [pl.BlockSpec((bm, k), lambda j, i: (i, 0))]
    in_specs += [pl.BlockSpec(memory_space=pl.ANY)] * n_w
    args = [a, *ws]
    if res is not None:
        in_specs.append(pl.BlockSpec((bm, bn), lambda j, i: (i, j)))
        args.append(res)
    stage_shape = (bn, k) if transposed else (k, bn)
    scratch = ([pltpu.VMEM(stage_shape, F32)] * n_w + [pltpu.VMEM((k, bn), BF16)] * n_w
               + [pltpu.SemaphoreType.DMA((n_w,))])
    return pl.pallas_call(
        functools.partial(_staged_mm_kernel, n_w=n_w, layer=layer, col0=col0, bn=bn,
                          epilogue=epilogue, has_res=res is not None, transposed=transposed),
        grid=(n // bn, m // bm),
        in_specs=in_specs,
        out_specs=pl.BlockSpec((bm, bn), lambda j, i: (i, j)),
        out_shape=jax.ShapeDtypeStruct((m, n), out_dtype),
        scratch_shapes=scratch,
        compiler_params=_cparams("arbitrary", "arbitrary"),
        name="staged_matmul",
    )(*args)


def _mm_acc_kernel(a_ref, w_ref, res_ref, o_ref, acc_ref):
    kk = pl.program_id(2)

    @pl.when(kk == 0)
    def _():
        acc_ref[...] = jnp.zeros_like(acc_ref)

    acc_ref[...] += _dot(a_ref[...], w_ref[...])

    @pl.when(kk == pl.num_programs(2) - 1)
    def _():
        o_ref[...] = (res_ref[...] + acc_ref[...]).astype(o_ref.dtype)


def matmul_acc_res(a, w, res, bk, bm=MM_BM, bn=MM_BN):
    m, k = a.shape
    n = w.shape[1]
    bm = _tile(m, bm)
    bn = _tile(n, bn)
    return pl.pallas_call(
        _mm_acc_kernel,
        grid=(n // bn, m // bm, k // bk),
        in_specs=[pl.BlockSpec((bm, bk), lambda j, i, kk: (i, kk)),
                  pl.BlockSpec((bk, bn), lambda j, i, kk: (kk, j)),
                  pl.BlockSpec((bm, bn), lambda j, i, kk: (i, j))],
        out_specs=pl.BlockSpec((bm, bn), lambda j, i, kk: (i, j)),
        out_shape=jax.ShapeDtypeStruct((m, n), res.dtype),
        scratch_shapes=[pltpu.VMEM((bm, bn), F32)],
        compiler_params=_cparams("parallel", "parallel", "arbitrary"),
        name="matmul_acc_res",
    )(a, w, res)


def _head_norm_gate(o, gain, g):
    y = o * lax.rsqrt(jnp.mean(o * o, axis=-1, keepdims=True) + EPS) * gain
    return y * _silu(g.astype(F32))


def _retention_kernel(q_ref, k_ref, v_ref, g_ref, mg_ref, cos_ref, sin_ref, dmat_ref, qdec_ref, kdec_ref,
                      cdec_ref, gain_ref, o_ref, st_ref):
    @pl.when(pl.program_id(2) == 0)
    def _():
        st_ref[...] = jnp.zeros_like(st_ref)

    cos = cos_ref[...]
    sin = sin_ref[...]
    dk = RET_QK_DIM
    half = dk // 2
    hp = st_ref.shape[0]
    dv = st_ref.shape[1]
    for hh in range(hp):
        qk_cols = slice(hh * dk, (hh + 1) * dk)
        v_cols = slice(hh * dv, (hh + 1) * dv)
        q = q_ref[:, qk_cols].astype(F32)
        k = k_ref[:, qk_cols].astype(F32)
        qr = q * cos + pltpu.roll(q, half, 1) * sin
        kr = (k * cos + pltpu.roll(k, half, 1) * sin) * (dk ** -0.5)
        v = v_ref[:, v_cols]
        scores = _dot_nt(qr.astype(BF16), kr.astype(BF16))
        st = st_ref[hh]
        cross = _dot_nt((qr * qdec_ref[hh]).astype(BF16), st.astype(BF16))
        v_t = v.astype(F32).T.astype(BF16)
        st_ref[hh] = cdec_ref[hh] * st + _dot(v_t, (kr * kdec_ref[hh]).astype(BF16))
        intra = _dot((scores * dmat_ref[hh]).astype(BF16), v)
        y = _head_norm_gate(intra + cross, gain_ref[:, v_cols], g_ref[:, v_cols])
        o_ref[:, v_cols] = (mg_ref[:, v_cols].astype(F32) * y).astype(o_ref.dtype)


def retention(r, gates, cos2, sin2, norm_g, batch, seq):
    heads, dk = RET_HEADS, RET_QK_DIM
    dv = (r.shape[1] - 2 * heads * dk) // (2 * heads)
    blk = min(RET_BLOCK, seq)
    nblk = seq // blk
    log_gamma = jnp.log(1.0 - jnp.exp2(-5.0 - jnp.arange(heads, dtype=F32)))
    idx = jnp.arange(blk, dtype=F32)
    rel = idx[:, None] - idx[None, :]
    dmat = jnp.where(rel >= 0, jnp.exp(log_gamma[:, None, None] * jnp.maximum(rel, 0.0)), 0.0)
    qdec = jnp.broadcast_to(jnp.exp(log_gamma[:, None] * (idx + 1.0))[:, :, None], (heads, blk, dk))
    kdec = jnp.broadcast_to(jnp.exp(log_gamma[:, None] * (blk - 1.0 - idx))[:, :, None], (heads, blk, dk))
    cdec = jnp.broadcast_to(jnp.exp(log_gamma * blk)[:, None, None], (heads, 1, dk))
    hp = MIXER_HEADS_PER_STEP
    groups = heads // hp
    assert (2 * heads * dk) % (hp * dv) == 0
    v_off = 2 * heads * dk // (hp * dv)
    g_off = v_off + groups
    row = lambda b, h, t: b * nblk + t
    return pl.pallas_call(
        _retention_kernel,
        grid=(batch, groups, nblk),
        in_specs=[pl.BlockSpec((blk, hp * dk), lambda b, h, t: (row(b, h, t), h)),
                  pl.BlockSpec((blk, hp * dk), lambda b, h, t: (row(b, h, t), groups + h)),
                  pl.BlockSpec((blk, hp * dv), lambda b, h, t: (row(b, h, t), v_off + h)),
                  pl.BlockSpec((blk, hp * dv), lambda b, h, t: (row(b, h, t), g_off + h)),
                  pl.BlockSpec((blk, hp * dv), lambda b, h, t: (row(b, h, t), h)),
                  pl.BlockSpec((blk, dk), lambda b, h, t: (row(b, h, t), 0)),
                  pl.BlockSpec((blk, dk), lambda b, h, t: (row(b, h, t), 0)),
                  pl.BlockSpec((hp, blk, blk), lambda b, h, t: (h, 0, 0)),
                  pl.BlockSpec((hp, blk, dk), lambda b, h, t: (h, 0, 0)),
                  pl.BlockSpec((hp, blk, dk), lambda b, h, t: (h, 0, 0)),
                  pl.BlockSpec((hp, 1, dk), lambda b, h, t: (h, 0, 0)),
                  pl.BlockSpec((1, hp * dv), lambda b, h, t: (0, h))],
        out_specs=pl.BlockSpec((blk, hp * dv), lambda b, h, t: (row(b, h, t), h)),
        out_shape=jax.ShapeDtypeStruct((batch * seq, heads * dv), BF16),
        scratch_shapes=[pltpu.VMEM((hp, dv, dk), F32)],
        compiler_params=_cparams("parallel", "parallel", "arbitrary"),
        name="retention",
    )(r, r, r, r, gates, cos2, sin2, dmat, qdec, kdec, cdec, norm_g.reshape(1, -1).astype(F32))


def _hgrn_kernel(f_ref, q_ref, i_ref, og_ref, mg_ref, y_ref, la_ref, l1_ref, oml_ref, gain_ref, o_ref,
                 st_ref):
    @pl.when(pl.program_id(2) == 0)
    def _():
        st_ref[...] = jnp.zeros_like(st_ref)

    blk = f_ref.shape[0]
    hp, dv, e = st_ref.shape
    c = HGRN_CHUNK
    nch = blk // c
    row = lax.broadcasted_iota(jnp.int32, (blk, e), 0)
    row_in_chunk = row % c
    chunk_of_row = row // c
    r2 = lax.broadcasted_iota(jnp.int32, (blk, blk), 0)
    c2 = lax.broadcasted_iota(jnp.int32, (blk, blk), 1)
    keep = (r2 // c == c2 // c) & (c2 <= r2)

    prep = []
    for hh in range(hp):
        e_cols = slice(hh * e, (hh + 1) * e)
        z = f_ref[:, e_cols]
        log_sig = jnp.minimum(z, 0.0) - jnp.log1p(jnp.exp(-jnp.abs(z)))
        la = la_ref[:, e_cols]
        u = l1_ref[:, e_cols] + log_sig
        log_f = jnp.maximum(la, u) + jnp.log1p(jnp.exp(-jnp.abs(la - u)))
        key = oml_ref[:, e_cols] * (1.0 / (1.0 + jnp.exp(z)))
        b = log_f
        shift = 1
        while shift < c:
            b = b + jnp.where(row_in_chunk >= shift, pltpu.roll(b, shift, 0), 0.0)
            shift *= 2
        b3 = b.reshape(nch, c, e)
        b_mid = jnp.broadcast_to(b3[:, c // 2 - 1:c // 2, :], (nch, c, e)).reshape(blk, e)
        b_end_rows = b3[:, c - 1:c, :]
        b_end = jnp.broadcast_to(b_end_rows, (nch, c, e)).reshape(blk, e)
        q = q_ref[:, e_cols].astype(F32)
        v = i_ref[:, hh * dv:(hh + 1) * dv]
        qf = (q * jnp.exp(b - b_mid)).astype(BF16)
        kf = (key * jnp.exp(b_mid - b)).astype(BF16)
        q_in = (q * jnp.exp(b)).astype(BF16)
        k_out = (key * jnp.exp(b_end - b)).astype(BF16)
        zero = jnp.zeros_like(q_in)
        k_wide = jnp.concatenate([jnp.where(chunk_of_row == j, k_out, zero) for j in range(nch)], axis=1)
        q_wide = jnp.concatenate([jnp.where(chunk_of_row == j, q_in, zero) for j in range(nch)], axis=1)
        v_t = v.astype(F32).T.astype(BF16)
        prep.append((qf, kf, k_wide, q_wide, v, v_t, jnp.exp(b_end_rows)))

    scores = [_dot_nt(p[0], p[1]) for p in prep]
    upds = [_dot(p[5], p[2]) for p in prep]
    for hh in range(hp):
        _, _, _, q_wide, v, _, dec = prep[hh]
        intra = _dot(jnp.where(keep, scores[hh], 0.0).astype(BF16), v)
        st = st_ref[hh]
        states = []
        for j in range(nch):
            states.append(st.astype(BF16))
            st = st * dec[j] + upds[hh][:, j * e:(j + 1) * e]
        st_ref[hh] = st
        inter = _dot_nt(q_wide, jnp.concatenate(states, axis=1))
        v_cols = slice(hh * dv, (hh + 1) * dv)
        y = _head_norm_gate(intra + inter, gain_ref[:, v_cols], og_ref[:, v_cols])
        o_ref[:, v_cols] = (y_ref[:, v_cols].astype(F32) + mg_ref[:, v_cols].astype(F32) * y).astype(o_ref.dtype)


def hgrn2(hf, hr, gates, y_prev, lb, norm_g, batch, seq):
    heads, e = HGRN_HEADS, HGRN_EXPAND
    dv = (hr.shape[1] - heads * e) // (2 * heads)
    blk = min(HGRN_BLOCK, seq)
    nblk = seq // blk
    lb = lb.astype(F32).reshape(1, -1)
    la, l1, oml = jnp.log(lb), jnp.log1p(-lb), 1.0 - lb
    hp = MIXER_HEADS_PER_STEP
    groups = heads // hp
    assert (heads * e) % (hp * dv) == 0
    i_off = heads * e // (hp * dv)
    og_off = i_off + groups
    row = lambda b, h, t: b * nblk + t
    vec = pl.BlockSpec((1, hp * e), lambda b, h, t: (0, h))
    return pl.pallas_call(
        _hgrn_kernel,
        grid=(batch, groups, nblk),
        in_specs=[pl.BlockSpec((blk, hp * e), lambda b, h, t: (row(b, h, t), h)),
                  pl.BlockSpec((blk, hp * e), lambda b, h, t: (row(b, h, t), h)),
                  pl.BlockSpec((blk, hp * dv), lambda b, h, t: (row(b, h, t), i_off + h)),
                  pl.BlockSpec((blk, hp * dv), lambda b, h, t: (row(b, h, t), og_off + h)),
                  pl.BlockSpec((blk, hp * dv), lambda b, h, t: (row(b, h, t), groups + h)),
                  pl.BlockSpec((blk, hp * dv), lambda b, h, t: (row(b, h, t), h)),
                  vec, vec, vec,
                  pl.BlockSpec((1, hp * dv), lambda b, h, t: (0, h))],
        out_specs=pl.BlockSpec((blk, hp * dv), lambda b, h, t: (row(b, h, t), h)),
        out_shape=jax.ShapeDtypeStruct((batch * seq, heads * dv), BF16),
        scratch_shapes=[pltpu.VMEM((hp, dv, e), F32)],
        compiler_params=_cparams("parallel", "parallel", "arbitrary"),
        name="hgrn2",
    )(hf, hr, hr, hr, gates, y_prev, la, l1, oml, norm_g.reshape(1, -1).astype(F32))


def _mla_prep_kernel(p_ref, qg_ref, kvg_ref, cm_ref, sm_ref, cq_ref, ckv_ref, kr_ref, *, q_rank, kv_rank):
    p = p_ref[...]

    def norm(x, g):
        return x * lax.rsqrt(jnp.mean(x * x, axis=-1, keepdims=True) + EPS) * g

    cq_ref[...] = norm(p[:, :q_rank], qg_ref[...]).astype(cq_ref.dtype)
    ckv_ref[...] = norm(p[:, q_rank:q_rank + kv_rank], kvg_ref[...]).astype(ckv_ref.dtype)
    kr = p[:, q_rank + kv_rank:q_rank + kv_rank + LANES]
    lane = lax.broadcasted_iota(jnp.int32, kr.shape, 1)
    half = MLA_ROPE_DIM // 2
    swapped = jnp.where(lane < MLA_ROPE_DIM + half, pltpu.roll(kr, half, 1),
                        pltpu.roll(kr, MLA_ROPE_DIM + half, 1))
    pair = jnp.where(lane < MLA_ROPE_DIM, kr, swapped)
    kr_ref[...] = (pair * cm_ref[...] + pltpu.roll(pair, MLA_ROPE_DIM, 1) * sm_ref[...]).astype(kr_ref.dtype)


def mla_prep(pm, q_norm, kv_norm, cm, sm):
    m = pm.shape[0]
    q_rank, kv_rank = q_norm.shape[0], kv_norm.shape[0]
    rows = min(ROW_TILE, m)
    full = lambda w: pl.BlockSpec((rows, w), lambda i: (i, 0))
    return pl.pallas_call(
        functools.partial(_mla_prep_kernel, q_rank=q_rank, kv_rank=kv_rank),
        grid=(m // rows,),
        in_specs=[full(pm.shape[1]),
                  pl.BlockSpec((1, q_rank), lambda i: (0, 0)),
                  pl.BlockSpec((1, kv_rank), lambda i: (0, 0)),
                  full(LANES), full(LANES)],
        out_specs=[full(q_rank), full(kv_rank), full(LANES)],
        out_shape=[jax.ShapeDtypeStruct((m, q_rank), BF16),
                   jax.ShapeDtypeStruct((m, kv_rank), BF16),
                   jax.ShapeDtypeStruct((m, LANES), BF16)],
        compiler_params=_cparams("parallel"),
        name="mla_prep",
    )(pm, q_norm.reshape(1, -1).astype(F32), kv_norm.reshape(1, -1).astype(F32), cm, sm)


def _attn_kernel(q_ref, kv_ref, kr_ref, mg_ref, y_ref, o_ref, kcat_ref, vt_ref, *, heads_per_step, tk):
    i = pl.program_id(2)
    tq = q_ref.shape[0]
    kv_w = MLA_NOPE_DIM + MLA_V_DIM
    q_w = MLA_NOPE_DIM + 2 * MLA_ROPE_DIM

    @pl.when(i == 0)
    def _():
        for hh in range(heads_per_step):
            kcat_ref[hh, :, :MLA_NOPE_DIM] = kv_ref[:, hh * kv_w:hh * kv_w + MLA_NOPE_DIM]
            kcat_ref[hh, :, MLA_NOPE_DIM:] = kr_ref[...]
            for jj in range(vt_ref.shape[1]):
                v = kv_ref[jj * tk:(jj + 1) * tk, hh * kv_w + MLA_NOPE_DIM:(hh + 1) * kv_w]
                vt_ref[hh, jj] = v.astype(F32).T.astype(BF16)

    qfs = [q_ref[:, hh * q_w:(hh + 1) * q_w] for hh in range(heads_per_step)]

    def step(j, carry, masked):
        rows = pl.ds(pl.multiple_of(j * tk, tk), tk)
        out = []
        scores = [_dot_nt(kcat_ref[hh, rows, :], qfs[hh]) for hh in range(heads_per_step)]
        for hh in range(heads_per_step):
            m, l, acc = carry[hh]
            s = scores[hh]
            if masked:
                kpos = lax.broadcasted_iota(jnp.int32, s.shape, 0) + j * tk
                qpos = lax.broadcasted_iota(jnp.int32, s.shape, 1) + i * tq
                s = jnp.where(kpos <= qpos, s, -jnp.inf)
            m_new = jnp.maximum(m, jnp.max(s, axis=0, keepdims=True))
            p = jnp.exp2(s - m_new)
            alpha = jnp.exp2(m - m_new)
            l = alpha * l + jnp.sum(p, axis=0, keepdims=True)
            acc = alpha * acc + _dot(vt_ref[hh, j], p.astype(BF16))
            out.append((m_new, l, acc))
        return tuple(out)

    init = tuple((jnp.full((1, tq), -jnp.inf, F32), jnp.zeros((1, tq), F32), jnp.zeros((MLA_V_DIM, tq), F32))
                 for _ in range(heads_per_step))
    per_q = tq // tk
    carry = lax.fori_loop(0, i * per_q, lambda j, cr: step(j, cr, False), init)
    for d in range(per_q):
        carry = step(i * per_q + d, carry, True)
    for hh in range(heads_per_step):
        _, l, acc = carry[hh]
        cols = slice(hh * MLA_V_DIM, (hh + 1) * MLA_V_DIM)
        y = y_ref[:, cols].astype(F32) + mg_ref[:, cols].astype(F32) * (acc / l).T
        o_ref[:, cols] = y.astype(o_ref.dtype)


def mla_attention(q, kv, kr, gates, y_prev, batch, seq):
    heads = MLA_HEADS
    hp = ATTN_HEADS_PER_STEP
    tq = min(ATTN_BLOCK, seq)
    tk = min(ATTN_KV_BLOCK, tq)
    nq = seq // tq
    qw = MLA_NOPE_DIM + 2 * MLA_ROPE_DIM
    kvw = MLA_NOPE_DIM + MLA_V_DIM
    return pl.pallas_call(
        functools.partial(_attn_kernel, heads_per_step=hp, tk=tk),
        grid=(batch, heads // hp, nq),
        in_specs=[pl.BlockSpec((tq, hp * qw), lambda b, h, i: (b * nq + i, h)),
                  pl.BlockSpec((seq, hp * kvw), lambda b, h, i: (b, h)),
                  pl.BlockSpec((seq, LANES), lambda b, h, i: (b, 0)),
                  pl.BlockSpec((tq, hp * MLA_V_DIM), lambda b, h, i: (b * nq + i, 2 * (heads // hp) + h)),
                  pl.BlockSpec((tq, hp * MLA_V_DIM), lambda b, h, i: (b * nq + i, h))],
        out_specs=pl.BlockSpec((tq, hp * MLA_V_DIM), lambda b, h, i: (b * nq + i, h)),
        out_shape=jax.ShapeDtypeStruct((batch * seq, heads * MLA_V_DIM), BF16),
        scratch_shapes=[pltpu.VMEM((hp, seq, MLA_NOPE_DIM + LANES), BF16),
                        pltpu.VMEM((hp, seq // tk, MLA_V_DIM, tk), BF16)],
        compiler_params=_cparams("parallel", "parallel", "arbitrary"),
        name="mla_attention",
    )(q, kv, kr, gates, y_prev)


def _router_kernel(x_ref, g_ref, r_ref, h_ref, idx_ref, w_ref):
    x = x_ref[...]
    h = x * lax.rsqrt(jnp.mean(x * x, axis=-1, keepdims=True) + EPS) * g_ref[...]
    half = h.shape[1] // 2
    lo = lax.bitcast_convert_type(h[:, :half].astype(BF16).astype(F32), jnp.int32)
    hi = lax.bitcast_convert_type(h[:, half:].astype(BF16).astype(F32), jnp.int32)
    h_ref[...] = lax.shift_right_logical(lo, 16) | (hi & HIGH_HALF_MASK)
    logits = jnp.dot(h, r_ref[...], preferred_element_type=F32, precision=lax.Precision.HIGHEST)
    lane = lax.broadcasted_iota(jnp.int32, logits.shape, 1)
    lg = jnp.where(lane < N_EXPERTS, logits, -jnp.inf)
    m1 = jnp.max(lg, axis=-1, keepdims=True)
    i1 = jnp.min(jnp.where(lg == m1, lane, LANES), axis=-1, keepdims=True)
    lg2 = jnp.where(lane == i1, -jnp.inf, lg)
    m2 = jnp.max(lg2, axis=-1, keepdims=True)
    i2 = jnp.min(jnp.where(lg2 == m2, lane, LANES), axis=-1, keepdims=True)
    e2 = jnp.exp(m2 - m1)
    w1 = 1.0 / (1.0 + e2)
    w2 = e2 / (1.0 + e2)
    idx_ref[...] = jnp.where(lane == 0, i1, i2)
    w_ref[...] = jnp.where(lane == 0, w1, w2)


def moe_router(x, g, router):
    m, d = x.shape
    rows = min(ROW_TILE, m)
    r_pad = jnp.zeros((d, LANES), F32).at[:, :N_EXPERTS].set(router.astype(F32))
    h, idx, w = pl.pallas_call(
        _router_kernel,
        grid=(m // rows,),
        in_specs=[pl.BlockSpec((rows, d), lambda i: (i, 0)),
                  pl.BlockSpec((1, d), lambda i: (0, 0)),
                  pl.BlockSpec((d, LANES), lambda i: (0, 0))],
        out_specs=[pl.BlockSpec((rows, d // 2), lambda i: (i, 0)),
                   pl.BlockSpec((rows, LANES), lambda i: (i, 0)),
                   pl.BlockSpec((rows, LANES), lambda i: (i, 0))],
        out_shape=[jax.ShapeDtypeStruct((m, d // 2), jnp.int32),
                   jax.ShapeDtypeStruct((m, LANES), jnp.int32),
                   jax.ShapeDtypeStruct((m, LANES), F32)],
        compiler_params=_cparams("parallel"),
        name="moe_router",
    )(x, g.reshape(1, d).astype(F32), r_pad)
    return h, idx[:, :TOP_K], w[:, :TOP_K]


def _routing_tables(idx, wts, tile):
    t = idx.shape[0]
    pairs = t * TOP_K
    e = idx.reshape(pairs)
    onehot = (e[:, None] == jnp.arange(N_EXPERTS, dtype=jnp.int32)[None, :]).astype(jnp.int32)
    csum = jnp.cumsum(onehot, axis=0)
    rank = jnp.sum(csum * onehot, axis=1) - 1
    counts = csum[-1]
    padded = ((counts + tile - 1) // tile) * tile
    ends = jnp.cumsum(padded)
    starts = ends - padded
    pos = (jnp.sum(onehot * starts[None, :], axis=1) + rank).astype(jnp.int32)
    rows = pairs + N_EXPERTS * tile
    row_tok = jnp.zeros((rows,), jnp.int32).at[pos].set(jnp.arange(pairs, dtype=jnp.int32) // TOP_K)
    row_w = jnp.zeros((rows,), F32).at[pos].set(wts.reshape(pairs))
    tile_start = jnp.arange(rows // tile, dtype=jnp.int32) * tile
    tile_e = jnp.minimum(jnp.sum((tile_start[:, None] >= ends[None, :]).astype(jnp.int32), axis=1),
                         N_EXPERTS - 1).astype(jnp.int32)
    n_used = (ends[-1] // tile).astype(jnp.int32).reshape(1)
    n_tiles = rows // tile
    t_idx = jnp.arange(n_tiles, dtype=jnp.int32)
    prev_e = jnp.concatenate([jnp.full((1,), -1, jnp.int32), tile_e[:-1]])
    first = ((tile_e != prev_e) & (t_idx < n_used[0])).astype(jnp.int32)
    later_first = (t_idx[None, :] > t_idx[:, None]) & (first[None, :] == 1)
    next_idx = jnp.min(jnp.where(later_first, t_idx[None, :], n_tiles), axis=1)
    nxt = jnp.where(next_idx < n_tiles, tile_e[jnp.minimum(next_idx, n_tiles - 1)], -1).astype(jnp.int32)
    return pos, row_tok, row_w.reshape(rows, 1), (tile_e, first, nxt, n_used)


def _row_copy(src_hbm, src_row, dst, dst_row, sem):
    return pltpu.make_async_copy(src_hbm.at[pl.ds(src_row, 1), :], dst.at[pl.ds(dst_row, 1), :], sem)


def _gather_kernel(tok_ref, used_ref, h_hbm, o_ref, buf, sem):
    tg = buf.shape[1]
    i = pl.program_id(0)
    slot = i % 2
    used_rows = used_ref[0]

    def issue(tile, dst_slot):
        def start(r, carry):
            _row_copy(h_hbm, tok_ref[tile * tg + r], buf.at[dst_slot], r, sem.at[dst_slot]).start()
            return carry
        lax.fori_loop(0, tg, start, 0, unroll=8)

    @pl.when((i == 0) & (used_rows > 0))
    def _():
        issue(0, 0)

    @pl.when((i + 1 < pl.num_programs(0)) & ((i + 1) * tg < used_rows))
    def _():
        issue(i + 1, 1 - slot)

    @pl.when(i * tg < used_rows)
    def _():
        def wait(r, carry):
            _row_copy(h_hbm, 0, buf.at[slot], r, sem.at[slot]).wait()
            return carry

        lax.fori_loop(0, tg, wait, 0, unroll=8)
        w = buf[slot]
        lo = lax.bitcast_convert_type(lax.shift_left(w, 16), F32)
        hi = lax.bitcast_convert_type(w & HIGH_HALF_MASK, F32)
        o_ref[...] = jnp.concatenate([lo, hi], axis=1).astype(o_ref.dtype)

    @pl.when(i * tg >= used_rows)
    def _():
        o_ref[...] = jnp.zeros_like(o_ref)


def moe_gather(h, row_tok, used_rows):
    rows = row_tok.shape[0]
    d = 2 * h.shape[1]
    tg = GATHER_TILE
    return pl.pallas_call(
        _gather_kernel,
        grid_spec=pltpu.PrefetchScalarGridSpec(
            num_scalar_prefetch=2,
            grid=(rows // tg,),
            in_specs=[pl.BlockSpec(memory_space=pl.ANY)],
            out_specs=pl.BlockSpec((tg, d), lambda i, tok, used: (i, 0)),
            scratch_shapes=[pltpu.VMEM((2, tg, d // 2), h.dtype), pltpu.SemaphoreType.DMA((2,))]),
        out_shape=jax.ShapeDtypeStruct((rows, d), BF16),
        compiler_params=_cparams("arbitrary"),
        name="moe_gather",
    )(row_tok, used_rows, h)


def _moe_mm_kernel(*refs, n_w, layer, bn, has_scale):
    te_ref, first_ref, nxt_ref, nu_ref, a_ref = refs[:5]
    w_hbm = refs[5:5 + n_w]
    rw_ref = refs[5 + n_w] if has_scale else None
    o_ref = refs[5 + n_w + int(has_scale)]
    stage = refs[-1 - 2 * n_w:-1 - n_w]
    wb = refs[-1 - n_w:-1]
    sem = refs[-1]
    j = pl.program_id(0)
    i = pl.program_id(1)

    def copy(e, jj, t):
        cols = pl.ds(pl.multiple_of(jj * bn, LANES), bn)
        return pltpu.make_async_copy(w_hbm[t].at[layer, e, :, cols], stage[t], sem.at[t])

    @pl.when(first_ref[i] == 1)
    def _():
        @pl.when((j == 0) & (i == 0))
        def _():
            for t in range(n_w):
                copy(te_ref[0], 0, t).start()

        for t in range(n_w):
            copy(te_ref[i], j, t).wait()
            wb[t][...] = stage[t][...].astype(BF16)

        nxt = nxt_ref[i]

        @pl.when(nxt >= 0)
        def _():
            for t in range(n_w):
                copy(nxt, j, t).start()

        @pl.when((nxt < 0) & (j + 1 < pl.num_programs(0)))
        def _():
            for t in range(n_w):
                copy(te_ref[0], j + 1, t).start()

    used = i < nu_ref[0]

    @pl.when(used)
    def _():
        a = a_ref[...]
        acc = _dot(a, wb[0][...])
        if n_w == 2:
            acc = _silu(acc) * _dot(a, wb[1][...])
        if has_scale:
            acc = rw_ref[...] * acc
        o_ref[...] = acc.astype(o_ref.dtype)

    @pl.when(jnp.logical_not(used))
    def _():
        o_ref[...] = jnp.zeros_like(o_ref)


def moe_matmul(a, ws, layer, tables, out_dtype, bn, row_scale=None):
    tile_e, first, nxt, n_used = tables
    rows, k = a.shape
    n = ws[0].shape[-1]
    n_w = len(ws)
    tm = MOE_TILE
    bn = _tile(n, bn)
    idx = lambda j, i, *_: (i, 0)
    in_specs = [pl.BlockSpec((tm, k), idx)] + [pl.BlockSpec(memory_space=pl.ANY)] * n_w
    args = [a, *ws]
    if row_scale is not None:
        in_specs.append(pl.BlockSpec((tm, 1), idx))
        args.append(row_scale)
    scratch = ([pltpu.VMEM((k, bn), F32)] * n_w + [pltpu.VMEM((k, bn), BF16)] * n_w
               + [pltpu.SemaphoreType.DMA((n_w,))])
    return pl.pallas_call(
        functools.partial(_moe_mm_kernel, n_w=n_w, layer=layer, bn=bn, has_scale=row_scale is not None),
        grid_spec=pltpu.PrefetchScalarGridSpec(
            num_scalar_prefetch=4,
            grid=(n // bn, rows // tm),
            in_specs=in_specs,
            out_specs=pl.BlockSpec((tm, bn), lambda j, i, *_: (i, j)),
            scratch_shapes=scratch),
        out_shape=jax.ShapeDtypeStruct((rows, n), out_dtype),
        compiler_params=_cparams("arbitrary", "arbitrary"),
        name="moe_matmul",
    )(tile_e, first, nxt, n_used, *args)


def _combine_kernel(pos_ref, x_ref, y_hbm, g_ref, o_ref, buf, sem, *, apply_norm):
    tb = x_ref.shape[0]
    i = pl.program_id(0)
    slot = i % 2

    def issue(tile, dst_slot):
        def start(r, carry):
            for s in range(TOP_K):
                _row_copy(y_hbm, pos_ref[(tile * tb + r) * TOP_K + s], buf.at[dst_slot, s], r,
                          sem.at[dst_slot]).start()
            return carry
        lax.fori_loop(0, tb, start, 0, unroll=4)

    @pl.when(i == 0)
    def _():
        issue(0, 0)

    @pl.when(i + 1 < pl.num_programs(0))
    def _():
        issue(i + 1, 1 - slot)

    def wait(r, carry):
        for s in range(TOP_K):
            _row_copy(y_hbm, 0, buf.at[slot, s], r, sem.at[slot]).wait()
        return carry

    lax.fori_loop(0, tb, wait, 0, unroll=4)
    x = x_ref[...]
    for s in range(TOP_K):
        x = x + buf[slot, s]
    if apply_norm:
        x = x * lax.rsqrt(jnp.mean(x * x, axis=-1, keepdims=True) + EPS) * g_ref[...]
    o_ref[...] = x.astype(o_ref.dtype)


def moe_combine(x, y, pos, final_g, apply_norm):
    m, d = x.shape
    tb = GATHER_TILE
    return pl.pallas_call(
        functools.partial(_combine_kernel, apply_norm=apply_norm),
        grid_spec=pltpu.PrefetchScalarGridSpec(
            num_scalar_prefetch=1,
            grid=(m // tb,),
            in_specs=[pl.BlockSpec((tb, d), lambda i, p: (i, 0)),
                      pl.BlockSpec(memory_space=pl.ANY),
                      pl.BlockSpec((1, d), lambda i, p: (0, 0))],
            out_specs=pl.BlockSpec((tb, d), lambda i, p: (i, 0)),
            scratch_shapes=[pltpu.VMEM((2, TOP_K, tb, d), y.dtype), pltpu.SemaphoreType.DMA((2,))]),
        out_shape=jax.ShapeDtypeStruct((m, d), x.dtype),
        compiler_params=_cparams("arbitrary"),
        name="moe_combine_norm",
    )(pos, x, y, final_g.reshape(1, d).astype(F32))


def _rope_tables(positions, dim, pad_to):
    inv = ROPE_THETA ** (-jnp.arange(0, dim, 2, dtype=F32) / dim)
    ang = positions.astype(F32).reshape(-1)[:, None] * inv
    cos, sin = jnp.cos(ang), jnp.sin(ang)
    pad = jnp.zeros((ang.shape[0], pad_to - dim), F32)
    return (jnp.concatenate([cos, cos, pad], axis=1), jnp.concatenate([-sin, sin, pad], axis=1))


def _swap_halves(w):
    half = w.shape[-1] // 2
    return jnp.concatenate([w[..., half:], w[..., :half]], axis=-1)


def kernel(x, positions, ln_mix, w_in, ret_norm, hgrn_norm, hgrn_lb_logits, mla_q_norm, mla_w_uq,
           mla_kv_norm, mla_w_ukv, w_out, ln_ffn, ffn_w1, ffn_w3, ffn_w2, moe_router_w, moe_w1,
           moe_w3, moe_w2, final_norm):
    batch, seq, d_model = x.shape
    depth = w_in.shape[0]
    q_rank, kv_rank = mla_q_norm.shape[1], mla_kv_norm.shape[1]
    ret_w = 2 * RET_HEADS * RET_QK_DIM + 2 * d_model
    hg_e = HGRN_HEADS * HGRN_EXPAND
    hg_w = 2 * hg_e + 2 * d_model
    mla_w = q_rank + kv_rank + MLA_ROPE_DIM
    mla_pad = -(-(q_rank + kv_rank + LANES) // (2 * LANES)) * (2 * LANES)
    w_in_t = jnp.swapaxes(w_in, 1, 2)

    cos_r, sin_r = _rope_tables(positions, RET_QK_DIM, RET_QK_DIM)
    cm, sm = _rope_tables(positions, MLA_ROPE_DIM, LANES)
    lb_all = jnp.cumsum(jax.nn.softmax(hgrn_lb_logits.astype(F32), axis=0), axis=0)
    lb_all = lb_all - lb_all[:1]

    xf = x.reshape(batch * seq, d_model)
    for l in range(depth):
        last = l == depth - 1
        o = ret_w + hg_w
        h = rmsnorm(xf, ln_mix[l], BF16)
        r = staged_matmul(h, [w_in_t], l, 0, ret_w, BF16, transposed=True)
        hf = staged_matmul(h, [w_in_t], l, ret_w, hg_e, F32, transposed=True)
        hr = staged_matmul(h, [w_in_t], l, ret_w + hg_e, hg_w - hg_e, BF16, transposed=True)
        pm = staged_matmul(h, [w_in_t], l, o, mla_pad, F32, transposed=True, bn=mla_pad // 2)
        gates = staged_matmul(h, [w_in_t], l, o + mla_w, N_BRANCHES * d_model, BF16, epilogue="sigmoid",
                              transposed=True)

        y = retention(r, gates, cos_r, sin_r, ret_norm[l], batch, seq)
        y = hgrn2(hf, hr, gates, y, lb_all[l], hgrn_norm[l], batch, seq)

        cqn, ckvn, kr = mla_prep(pm, mla_q_norm[l], mla_kv_norm[l], cm, sm)
        wq = mla_w_uq[l].reshape(q_rank, MLA_HEADS, MLA_NOPE_DIM + MLA_ROPE_DIM)
        wq_rope = wq[..., MLA_NOPE_DIM:]
        wq = jnp.concatenate([wq, _swap_halves(wq_rope)], axis=-1).reshape(q_rank, -1).astype(BF16)
        q_scale = (MLA_NOPE_DIM + MLA_ROPE_DIM) ** -0.5 * LOG2_E
        q = matmul(cqn, wq, BF16, bn=MLA_UP_BN, q_rope=(cm, sm, q_scale))
        kv = matmul(ckvn, mla_w_ukv[l].astype(BF16), BF16, bn=MLA_UP_BN)
        y = mla_attention(q, kv, kr, gates, y, batch, seq)
        xf = staged_matmul(y, [w_out], l, 0, d_model, F32, res=xf)

        if l % 2 == 0:
            j = l // 2
            h2 = rmsnorm(xf, ln_ffn[l], BF16)
            g = staged_matmul(h2, [ffn_w1, ffn_w3], j, 0, ffn_w1.shape[-1], BF16, epilogue="swiglu")
            xf = matmul_acc_res(g, ffn_w2[j].astype(BF16), xf, _tile(g.shape[1], FFN_DOWN_BK), bn=FFN_DOWN_BN)
            if last:
                xf = rmsnorm(xf, final_norm, x.dtype)
        else:
            j = l // 2
            h2, idx, wts = moe_router(xf, ln_ffn[l], moe_router_w[j])
            pos, row_tok, row_w, tables = _routing_tables(idx, wts, MOE_TILE)
            xs = moe_gather(h2, row_tok, tables[3] * MOE_TILE)
            g = moe_matmul(xs, [moe_w1, moe_w3], j, tables, BF16, MOE_UP_BN)
            yrows = moe_matmul(g, [moe_w2], j, tables, F32, MOE_DOWN_BN, row_scale=row_w)
            xf = moe_combine(xf, yrows, pos, final_norm, apply_norm=last)
    return xf.reshape(batch, seq, d_model)
```

```python
import functools

import jax
import jax.numpy as jnp
from jax import lax
from jax.experimental import pallas as pl
from jax.experimental.pallas import tpu as pltpu

F32 = jnp.float32
BF16 = jnp.bfloat16

RET_HEADS = 16
RET_QK_DIM = 128
HGRN_HEADS = 16
HGRN_EXPAND = 128
HGRN_CHUNK = 32
MLA_HEADS = 32
MLA_NOPE_DIM = 128
MLA_ROPE_DIM = 64
MLA_V_DIM = 128
N_BRANCHES = 3
N_EXPERTS = 8
TOP_K = 2
ROPE_THETA = 10000.0
EPS = 1e-6
LOG2_E = 1.4426950408889634
BF16_BITS = 16
HIGH_HALF_MASK = -65536

LANES = 128
SUBLANES = 8
VMEM_LIMIT_BYTES = 58 * 2 ** 20

ROW_TILE = 512
MM_BM = 1024
MM_BN = 512
MLA_UP_BN = 2048
CAST_CHUNK = 512
RET_BLOCK = 512
HGRN_BLOCK = 256
MIXER_HEADS_PER_STEP = 4
ATTN_BLOCK = 512
ATTN_KV_BLOCK = 512
ATTN_HEADS_PER_STEP = 4
MOE_TILE = 512
MOE_UP_BN = 512
MOE_DOWN_BN = 1024
FFN_DOWN_BK = 3584
FFN_DOWN_BN = 1024
GATHER_TILE = 256


def _cparams(*sem):
    return pltpu.CompilerParams(dimension_semantics=sem, vmem_limit_bytes=VMEM_LIMIT_BYTES)


def _tile(n, preferred):
    for t in range(min(preferred, n) // LANES * LANES, 0, -LANES):
        if n % t == 0:
            return t
    return n


def _dot(a, b):
    return jnp.dot(a, b, preferred_element_type=F32)


def _dot_nt(a, b):
    return lax.dot_general(a, b, (((1,), (1,)), ((), ())), preferred_element_type=F32)


def _sigmoid(x):
    return 0.5 * jnp.tanh(0.5 * x) + 0.5


def _silu(x):
    return x * _sigmoid(x)


def _pack_bf16_pairs(x):
    c = x.shape[1] // 2
    lo = lax.bitcast_convert_type(x[:, :c].astype(BF16).astype(F32), jnp.int32)
    hi = lax.bitcast_convert_type(x[:, c:].astype(BF16).astype(F32), jnp.int32)
    return lax.shift_right_logical(lo, BF16_BITS) | (hi & HIGH_HALF_MASK)


def _unpack_bf16_pairs(w):
    lo = lax.bitcast_convert_type(lax.shift_left(w, BF16_BITS), F32)
    hi = lax.bitcast_convert_type(w & HIGH_HALF_MASK, F32)
    return jnp.concatenate([lo, hi], axis=1)


def _rmsnorm_kernel(x_ref, g_ref, o_ref):
    x = x_ref[...].astype(F32)
    ms = jnp.mean(x * x, axis=-1, keepdims=True)
    o_ref[...] = (x * lax.rsqrt(ms + EPS) * g_ref[...]).astype(o_ref.dtype)


def rmsnorm(x, g, out_dtype):
    m, d = x.shape
    rows = min(ROW_TILE, m)
    return pl.pallas_call(
        _rmsnorm_kernel,
        grid=(m // rows,),
        in_specs=[pl.BlockSpec((rows, d), lambda i: (i, 0)),
                  pl.BlockSpec((1, d), lambda i: (0, 0))],
        out_specs=pl.BlockSpec((rows, d), lambda i: (i, 0)),
        out_shape=jax.ShapeDtypeStruct((m, d), out_dtype),
        compiler_params=_cparams("parallel"),
        name="rmsnorm",
    )(x, g.reshape(1, d).astype(F32))


def _mm_kernel(*refs, q_scale):
    a_ref, w_ref = refs[0], refs[1]
    o_ref = refs[-1]
    acc = _dot(a_ref[...], w_ref[...])
    if q_scale is None:
        o_ref[...] = acc.astype(o_ref.dtype)
        return
    cm, sm = refs[2][...], refs[3][...]
    head_w = MLA_NOPE_DIM + 2 * MLA_ROPE_DIM
    for c in range(0, acc.shape[1], head_w):
        qr = acc[:, c + MLA_NOPE_DIM:c + head_w]
        qr = qr * cm + pltpu.roll(qr, MLA_ROPE_DIM, 1) * sm
        o_ref[:, c:c + MLA_NOPE_DIM] = (acc[:, c:c + MLA_NOPE_DIM] * q_scale).astype(o_ref.dtype)
        o_ref[:, c + MLA_NOPE_DIM:c + head_w] = (qr * q_scale).astype(o_ref.dtype)


def matmul(a, w, out_dtype, bm=MM_BM, bn=MM_BN, q_rope=None):
    m, k = a.shape
    n = w.shape[1]
    bm = _tile(m, bm)
    bn = _tile(n, bn)
    in_specs = [pl.BlockSpec((bm, k), lambda j, i: (i, 0)),
                pl.BlockSpec((k, bn), lambda j, i: (0, j))]
    args = [a, w]
    if q_rope is not None:
        in_specs += [pl.BlockSpec((bm, LANES), lambda j, i: (i, 0))] * 2
        args += [q_rope[0], q_rope[1]]
    return pl.pallas_call(
        functools.partial(_mm_kernel, q_scale=None if q_rope is None else q_rope[2]),
        grid=(n // bn, m // bm),
        in_specs=in_specs,
        out_specs=pl.BlockSpec((bm, bn), lambda j, i: (i, j)),
        out_shape=jax.ShapeDtypeStruct((m, n), out_dtype),
        compiler_params=_cparams("parallel", "parallel"),
        name="matmul",
    )(*args)


def _staged_mm_kernel(*refs, n_w, layer, col0, bn, epilogue, has_res, transposed):
    a_ref = refs[0]
    w_hbm = refs[1:1 + n_w]
    res_ref = refs[1 + n_w] if has_res else None
    o_ref = refs[1 + n_w + int(has_res)]
    stage = refs[-1 - 2 * n_w:-1 - n_w]
    wb = refs[-1 - n_w:-1]
    sem = refs[-1]
    j = pl.program_id(0)
    i = pl.program_id(1)

    def copy(jj, t):
        if transposed:
            rows = pl.ds(pl.multiple_of(col0 + jj * bn, SUBLANES), bn)
            return pltpu.make_async_copy(w_hbm[t].at[layer, rows, :], stage[t], sem.at[t])
        cols = pl.ds(pl.multiple_of(col0 + jj * bn, LANES), bn)
        return pltpu.make_async_copy(w_hbm[t].at[layer, :, cols], stage[t], sem.at[t])

    def cast(t):
        if not transposed:
            wb[t][...] = stage[t][...].astype(BF16)
            return
        k = wb[t].shape[0]
        step = _tile(k, CAST_CHUNK)
        for c in range(0, k, step):
            wb[t][c:c + step, :] = stage[t][:, c:c + step].T.astype(BF16)

    @pl.when(i == 0)
    def _():
        @pl.when(j == 0)
        def _():
            for t in range(n_w):
                copy(0, t).start()

        for t in range(n_w):
            copy(j, t).wait()
            cast(t)

        @pl.when(j + 1 < pl.num_programs(0))
        def _():
            for t in range(n_w):
                copy(j + 1, t).start()

    a = a_ref[...]
    acc = _dot(a, wb[0][...])
    if epilogue == "swiglu":
        acc = _silu(acc) * _dot(a, wb[1][...])
    elif epilogue == "sigmoid":
        acc = _sigmoid(acc)
    if has_res:
        acc = res_ref[...] + acc
    o_ref[...] = acc.astype(o_ref.dtype)


def staged_matmul(a, ws, layer, col0, n, out_dtype, epilogue=None, res=None, transposed=False,
                  bm=MM_BM, bn=MM_BN):
    m, k = a.shape
    n_w = len(ws)
    bm = _tile(m, bm)
    bn = _tile(n, bn)
    assert col0 % (SUBLANES if transposed else LANES) == 0 and n % bn == 0
    in_specs = [pl.BlockSpec((bm, k), lambda j, i: (i, 0))]
    in_specs += [pl.BlockSpec(memory_space=pl.ANY)] * n_w
    args = [a, *ws]
    if res is not None:
        in_specs.append(pl.BlockSpec((bm, bn), lambda j, i: (i, j)))
        args.append(res)
    stage_shape = (bn, k) if transposed else (k, bn)
    scratch = ([pltpu.VMEM(stage_shape, F32)] * n_w + [pltpu.VMEM((k, bn), BF16)] * n_w
               + [pltpu.SemaphoreType.DMA((n_w,))])
    return pl.pallas_call(
        functools.partial(_staged_mm_kernel, n_w=n_w, layer=layer, col0=col0, bn=bn,
                          epilogue=epilogue, has_res=res is not None, transposed=transposed),
        grid=(n // bn, m // bm),
        in_specs=in_specs,
        out_specs=pl.BlockSpec((bm, bn), lambda j, i: (i, j)),
        out_shape=jax.ShapeDtypeStruct((m, n), out_dtype),
        scratch_shapes=scratch,
        compiler_params=_cparams("arbitrary", "arbitrary"),
        name="staged_matmul",
    )(*args)


def _mm_acc_kernel(a_ref, w_ref, res_ref, o_ref, acc_ref):
    kk = pl.program_id(2)

    @pl.when(kk == 0)
    def _():
        acc_ref[...] = jnp.zeros_like(acc_ref)

    acc_ref[...] += _dot(a_ref[...], w_ref[...])

    @pl.when(kk == pl.num_programs(2) - 1)
    def _():
        o_ref[...] = (res_ref[...] + acc_ref[...]).astype(o_ref.dtype)


def matmul_acc_res(a, w, res, bk, bm=MM_BM, bn=MM_BN):
    m, k = a.shape
    n = w.shape[1]
    bm = _tile(m, bm)
    bn = _tile(n, bn)
    return pl.pallas_call(
        _mm_acc_kernel,
        grid=(n // bn, m // bm, k // bk),
        in_specs=[pl.BlockSpec((bm, bk), lambda j, i, kk: (i, kk)),
                  pl.BlockSpec((bk, bn), lambda j, i, kk: (kk, j)),
                  pl.BlockSpec((bm, bn), lambda j, i, kk: (i, j))],
        out_specs=pl.BlockSpec((bm, bn), lambda j, i, kk: (i, j)),
        out_shape=jax.ShapeDtypeStruct((m, n), res.dtype),
        scratch_shapes=[pltpu.VMEM((bm, bn), F32)],
        compiler_params=_cparams("parallel", "parallel", "arbitrary"),
        name="matmul_acc_res",
    )(a, w, res)


def _head_norm_gate(o, gain, g):
    y = o * lax.rsqrt(jnp.mean(o * o, axis=-1, keepdims=True) + EPS) * gain
    return y * _silu(g.astype(F32))


def _retention_kernel(q_ref, k_ref, v_ref, g_ref, mg_ref, cos_ref, sin_ref, dmat_ref, qdec_ref, kdec_ref,
                      cdec_ref, gain_ref, o_ref, st_ref):
    @pl.when(pl.program_id(2) == 0)
    def _():
        st_ref[...] = jnp.zeros_like(st_ref)

    cos = cos_ref[...]
    sin = sin_ref[...]
    dk = RET_QK_DIM
    half = dk // 2
    hp = st_ref.shape[0]
    dv = st_ref.shape[1]
    for hh in range(hp):
        qk_cols = slice(hh * dk, (hh + 1) * dk)
        v_cols = slice(hh * dv, (hh + 1) * dv)
        q = q_ref[:, qk_cols].astype(F32)
        k = k_ref[:, qk_cols].astype(F32)
        qr = q * cos + pltpu.roll(q, half, 1) * sin
        kr = (k * cos + pltpu.roll(k, half, 1) * sin) * (dk ** -0.5)
        v = v_ref[:, v_cols]
        scores = _dot_nt(qr.astype(BF16), kr.astype(BF16))
        st = st_ref[hh]
        cross = _dot_nt((qr * qdec_ref[hh]).astype(BF16), st.astype(BF16))
        v_t = v.astype(F32).T.astype(BF16)
        st_ref[hh] = cdec_ref[hh] * st + _dot(v_t, (kr * kdec_ref[hh]).astype(BF16))
        intra = _dot((scores * dmat_ref[hh]).astype(BF16), v)
        y = _head_norm_gate(intra + cross, gain_ref[:, v_cols], g_ref[:, v_cols])
        o_ref[:, v_cols] = (mg_ref[:, v_cols].astype(F32) * y).astype(o_ref.dtype)


def retention(r, gates, cos2, sin2, norm_g, batch, seq):
    heads, dk = RET_HEADS, RET_QK_DIM
    dv = (r.shape[1] - 2 * heads * dk) // (2 * heads)
    blk = min(RET_BLOCK, seq)
    nblk = seq // blk
    log_gamma = jnp.log(1.0 - jnp.exp2(-5.0 - jnp.arange(heads, dtype=F32)))
    idx = jnp.arange(blk, dtype=F32)
    rel = idx[:, None] - idx[None, :]
    dmat = jnp.where(rel >= 0, jnp.exp(log_gamma[:, None, None] * jnp.maximum(rel, 0.0)), 0.0)
    qdec = jnp.broadcast_to(jnp.exp(log_gamma[:, None] * (idx + 1.0))[:, :, None], (heads, blk, dk))
    kdec = jnp.broadcast_to(jnp.exp(log_gamma[:, None] * (blk - 1.0 - idx))[:, :, None], (heads, blk, dk))
    cdec = jnp.broadcast_to(jnp.exp(log_gamma * blk)[:, None, None], (heads, 1, dk))
    hp = MIXER_HEADS_PER_STEP
    groups = heads // hp
    assert (2 * heads * dk) % (hp * dv) == 0
    v_off = 2 * heads * dk // (hp * dv)
    g_off = v_off + groups
    row = lambda b, h, t: b * nblk + t
    return pl.pallas_call(
        _retention_kernel,
        grid=(batch, groups, nblk),
        in_specs=[pl.BlockSpec((blk, hp * dk), lambda b, h, t: (row(b, h, t), h)),
                  pl.BlockSpec((blk, hp * dk), lambda b, h, t: (row(b, h, t), groups + h)),
                  pl.BlockSpec((blk, hp * dv), lambda b, h, t: (row(b, h, t), v_off + h)),
                  pl.BlockSpec((blk, hp * dv), lambda b, h, t: (row(b, h, t), g_off + h)),
                  pl.BlockSpec((blk, hp * dv), lambda b, h, t: (row(b, h, t), h)),
                  pl.BlockSpec((blk, dk), lambda b, h, t: (row(b, h, t), 0)),
                  pl.BlockSpec((blk, dk), lambda b, h, t: (row(b, h, t), 0)),
                  pl.BlockSpec((hp, blk, blk), lambda b, h, t: (h, 0, 0)),
                  pl.BlockSpec((hp, blk, dk), lambda b, h, t: (h, 0, 0)),
                  pl.BlockSpec((hp, blk, dk), lambda b, h, t: (h, 0, 0)),
                  pl.BlockSpec((hp, 1, dk), lambda b, h, t: (h, 0, 0)),
                  pl.BlockSpec((1, hp * dv), lambda b, h, t: (0, h))],
        out_specs=pl.BlockSpec((blk, hp * dv), lambda b, h, t: (row(b, h, t), h)),
        out_shape=jax.ShapeDtypeStruct((batch * seq, heads * dv), BF16),
        scratch_shapes=[pltpu.VMEM((hp, dv, dk), F32)],
        compiler_params=_cparams("parallel", "parallel", "arbitrary"),
        name="retention",
    )(r, r, r, r, gates, cos2, sin2, dmat, qdec, kdec, cdec, norm_g.reshape(1, -1).astype(F32))


def _hgrn_kernel(f_ref, q_ref, i_ref, og_ref, mg_ref, y_ref, la_ref, l1_ref, oml_ref, gain_ref, o_ref,
                 st_ref):
    @pl.when(pl.program_id(2) == 0)
    def _():
        st_ref[...] = jnp.zeros_like(st_ref)

    blk = f_ref.shape[0]
    hp, dv, e = st_ref.shape
    c = HGRN_CHUNK
    nch = blk // c
    row = lax.broadcasted_iota(jnp.int32, (blk, e), 0)
    row_in_chunk = row % c
    chunk_of_row = row // c
    r2 = lax.broadcasted_iota(jnp.int32, (blk, blk), 0)
    c2 = lax.broadcasted_iota(jnp.int32, (blk, blk), 1)
    keep = (r2 // c == c2 // c) & (c2 <= r2)

    prep = []
    for hh in range(hp):
        e_cols = slice(hh * e, (hh + 1) * e)
        z = f_ref[:, e_cols]
        log_sig = jnp.minimum(z, 0.0) - jnp.log1p(jnp.exp(-jnp.abs(z)))
        la = la_ref[:, e_cols]
        u = l1_ref[:, e_cols] + log_sig
        log_f = jnp.maximum(la, u) + jnp.log1p(jnp.exp(-jnp.abs(la - u)))
        key = oml_ref[:, e_cols] * (1.0 / (1.0 + jnp.exp(z)))
        b = log_f
        shift = 1
        while shift < c:
            b = b + jnp.where(row_in_chunk >= shift, pltpu.roll(b, shift, 0), 0.0)
            shift *= 2
        b3 = b.reshape(nch, c, e)
        b_mid = jnp.broadcast_to(b3[:, c // 2 - 1:c // 2, :], (nch, c, e)).reshape(blk, e)
        b_end_rows = b3[:, c - 1:c, :]
        b_end = jnp.broadcast_to(b_end_rows, (nch, c, e)).reshape(blk, e)
        q = q_ref[:, e_cols].astype(F32)
        v = i_ref[:, hh * dv:(hh + 1) * dv]
        qf = (q * jnp.exp(b - b_mid)).astype(BF16)
        kf = (key * jnp.exp(b_mid - b)).astype(BF16)
        q_in = (q * jnp.exp(b)).astype(BF16)
        k_out = (key * jnp.exp(b_end - b)).astype(BF16)
        zero = jnp.zeros_like(q_in)
        k_wide = jnp.concatenate([jnp.where(chunk_of_row == j, k_out, zero) for j in range(nch)], axis=1)
        q_wide = jnp.concatenate([jnp.where(chunk_of_row == j, q_in, zero) for j in range(nch)], axis=1)
        v_t = v.astype(F32).T.astype(BF16)
        prep.append((qf, kf, k_wide, q_wide, v, v_t, jnp.exp(b_end_rows)))

    scores = [_dot_nt(p[0], p[1]) for p in prep]
    upds = [_dot(p[5], p[2]) for p in prep]
    for hh in range(hp):
        _, _, _, q_wide, v, _, dec = prep[hh]
        intra = _dot(jnp.where(keep, scores[hh], 0.0).astype(BF16), v)
        st = st_ref[hh]
        states = []
        for j in range(nch):
            states.append(st.astype(BF16))
            st = st * dec[j] + upds[hh][:, j * e:(j + 1) * e]
        st_ref[hh] = st
        inter = _dot_nt(q_wide, jnp.concatenate(states, axis=1))
        v_cols = slice(hh * dv, (hh + 1) * dv)
        y = _head_norm_gate(intra + inter, gain_ref[:, v_cols], og_ref[:, v_cols])
        o_ref[:, v_cols] = (y_ref[:, v_cols].astype(F32) + mg_ref[:, v_cols].astype(F32) * y).astype(o_ref.dtype)


def hgrn2(hf, hr, gates, y_prev, lb, norm_g, batch, seq):
    heads, e = HGRN_HEADS, HGRN_EXPAND
    dv = (hr.shape[1] - heads * e) // (2 * heads)
    blk = min(HGRN_BLOCK, seq)
    nblk = seq // blk
    lb = lb.astype(F32).reshape(1, -1)
    la, l1, oml = jnp.log(lb), jnp.log1p(-lb), 1.0 - lb
    hp = MIXER_HEADS_PER_STEP
    groups = heads // hp
    assert (heads * e) % (hp * dv) == 0
    i_off = heads * e // (hp * dv)
    og_off = i_off + groups
    row = lambda b, h, t: b * nblk + t
    vec = pl.BlockSpec((1, hp * e), lambda b, h, t: (0, h))
    return pl.pallas_call(
        _hgrn_kernel,
        grid=(batch, groups, nblk),
        in_specs=[pl.BlockSpec((blk, hp * e), lambda b, h, t: (row(b, h, t), h)),
                  pl.BlockSpec((blk, hp * e), lambda b, h, t: (row(b, h, t), h)),
                  pl.BlockSpec((blk, hp * dv), lambda b, h, t: (row(b, h, t), i_off + h)),
                  pl.BlockSpec((blk, hp * dv), lambda b, h, t: (row(b, h, t), og_off + h)),
                  pl.BlockSpec((blk, hp * dv), lambda b, h, t: (row(b, h, t), groups + h)),
                  pl.BlockSpec((blk, hp * dv), lambda b, h, t: (row(b, h, t), h)),
                  vec, vec, vec,
                  pl.BlockSpec((1, hp * dv), lambda b, h, t: (0, h))],
        out_specs=pl.BlockSpec((blk, hp * dv), lambda b, h, t: (row(b, h, t), h)),
        out_shape=jax.ShapeDtypeStruct((batch * seq, heads * dv), BF16),
        scratch_shapes=[pltpu.VMEM((hp, dv, e), F32)],
        compiler_params=_cparams("parallel", "parallel", "arbitrary"),
        name="hgrn2",
    )(hf, hr, hr, hr, gates, y_prev, la, l1, oml, norm_g.reshape(1, -1).astype(F32))


def _mla_prep_kernel(p_ref, qg_ref, kvg_ref, cm_ref, sm_ref, cq_ref, ckv_ref, kr_ref, *, q_rank, kv_rank):
    p = p_ref[...]

    def norm(x, g):
        return x * lax.rsqrt(jnp.mean(x * x, axis=-1, keepdims=True) + EPS) * g

    cq_ref[...] = norm(p[:, :q_rank], qg_ref[...]).astype(cq_ref.dtype)
    ckv_ref[...] = norm(p[:, q_rank:q_rank + kv_rank], kvg_ref[...]).astype(ckv_ref.dtype)
    kr = p[:, q_rank + kv_rank:q_rank + kv_rank + LANES]
    lane = lax.broadcasted_iota(jnp.int32, kr.shape, 1)
    half = MLA_ROPE_DIM // 2
    swapped = jnp.where(lane < MLA_ROPE_DIM + half, pltpu.roll(kr, half, 1),
                        pltpu.roll(kr, MLA_ROPE_DIM + half, 1))
    pair = jnp.where(lane < MLA_ROPE_DIM, kr, swapped)
    kr_ref[...] = (pair * cm_ref[...] + pltpu.roll(pair, MLA_ROPE_DIM, 1) * sm_ref[...]).astype(kr_ref.dtype)


def mla_prep(pm, q_norm, kv_norm, cm, sm):
    m = pm.shape[0]
    q_rank, kv_rank = q_norm.shape[0], kv_norm.shape[0]
    rows = min(ROW_TILE, m)
    full = lambda w: pl.BlockSpec((rows, w), lambda i: (i, 0))
    return pl.pallas_call(
        functools.partial(_mla_prep_kernel, q_rank=q_rank, kv_rank=kv_rank),
        grid=(m // rows,),
        in_specs=[full(pm.shape[1]),
                  pl.BlockSpec((1, q_rank), lambda i: (0, 0)),
                  pl.BlockSpec((1, kv_rank), lambda i: (0, 0)),
                  full(LANES), full(LANES)],
        out_specs=[full(q_rank), full(kv_rank), full(LANES)],
        out_shape=[jax.ShapeDtypeStruct((m, q_rank), BF16),
                   jax.ShapeDtypeStruct((m, kv_rank), BF16),
                   jax.ShapeDtypeStruct((m, LANES), BF16)],
        compiler_params=_cparams("parallel"),
        name="mla_prep",
    )(pm, q_norm.reshape(1, -1).astype(F32), kv_norm.reshape(1, -1).astype(F32), cm, sm)


def _attn_kernel(q_ref, kv_ref, kr_ref, mg_ref, y_ref, o_ref, kcat_ref, vt_ref, *, heads_per_step, tk):
    i = pl.program_id(2)
    tq = q_ref.shape[0]
    kv_w = MLA_NOPE_DIM + MLA_V_DIM
    q_w = MLA_NOPE_DIM + 2 * MLA_ROPE_DIM

    @pl.when(i == 0)
    def _():
        for hh in range(heads_per_step):
            kcat_ref[hh, :, :MLA_NOPE_DIM] = kv_ref[:, hh * kv_w:hh * kv_w + MLA_NOPE_DIM]
            kcat_ref[hh, :, MLA_NOPE_DIM:] = kr_ref[...]
            for jj in range(vt_ref.shape[1]):
                v = kv_ref[jj * tk:(jj + 1) * tk, hh * kv_w + MLA_NOPE_DIM:(hh + 1) * kv_w]
                vt_ref[hh, jj] = v.astype(F32).T.astype(BF16)

    qfs = [q_ref[:, hh * q_w:(hh + 1) * q_w] for hh in range(heads_per_step)]

    def step(j, carry, masked):
        rows = pl.ds(pl.multiple_of(j * tk, tk), tk)
        out = []
        scores = [_dot_nt(kcat_ref[hh, rows, :], qfs[hh]) for hh in range(heads_per_step)]
        for hh in range(heads_per_step):
            m, l, acc = carry[hh]
            s = scores[hh]
            if masked:
                kpos = lax.broadcasted_iota(jnp.int32, s.shape, 0) + j * tk
                qpos = lax.broadcasted_iota(jnp.int32, s.shape, 1) + i * tq
                s = jnp.where(kpos <= qpos, s, -jnp.inf)
            m_new = jnp.maximum(m, jnp.max(s, axis=0, keepdims=True))
            p = jnp.exp2(s - m_new)
            alpha = jnp.exp2(m - m_new)
            l = alpha * l + jnp.sum(p, axis=0, keepdims=True)
            acc = alpha * acc + _dot(vt_ref[hh, j], p.astype(BF16))
            out.append((m_new, l, acc))
        return tuple(out)

    init = tuple((jnp.full((1, tq), -jnp.inf, F32), jnp.zeros((1, tq), F32), jnp.zeros((MLA_V_DIM, tq), F32))
                 for _ in range(heads_per_step))
    per_q = tq // tk
    carry = lax.fori_loop(0, i * per_q, lambda j, cr: step(j, cr, False), init)
    for d in range(per_q):
        carry = step(i * per_q + d, carry, True)
    for hh in range(heads_per_step):
        _, l, acc = carry[hh]
        cols = slice(hh * MLA_V_DIM, (hh + 1) * MLA_V_DIM)
        y = y_ref[:, cols].astype(F32) + mg_ref[:, cols].astype(F32) * (acc / l).T
        o_ref[:, cols] = y.astype(o_ref.dtype)


def mla_attention(q, kv, kr, gates, y_prev, batch, seq):
    heads = MLA_HEADS
    hp = ATTN_HEADS_PER_STEP
    tq = min(ATTN_BLOCK, seq)
    tk = min(ATTN_KV_BLOCK, tq)
    nq = seq // tq
    qw = MLA_NOPE_DIM + 2 * MLA_ROPE_DIM
    kvw = MLA_NOPE_DIM + MLA_V_DIM
    return pl.pallas_call(
        functools.partial(_attn_kernel, heads_per_step=hp, tk=tk),
        grid=(batch, heads // hp, nq),
        in_specs=[pl.BlockSpec((tq, hp * qw), lambda b, h, i: (b * nq + i, h)),
                  pl.BlockSpec((seq, hp * kvw), lambda b, h, i: (b, h)),
                  pl.BlockSpec((seq, LANES), lambda b, h, i: (b, 0)),
                  pl.BlockSpec((tq, hp * MLA_V_DIM), lambda b, h, i: (b * nq + i, 2 * (heads // hp) + h)),
                  pl.BlockSpec((tq, hp * MLA_V_DIM), lambda b, h, i: (b * nq + i, h))],
        out_specs=pl.BlockSpec((tq, hp * MLA_V_DIM), lambda b, h, i: (b * nq + i, h)),
        out_shape=jax.ShapeDtypeStruct((batch * seq, heads * MLA_V_DIM), BF16),
        scratch_shapes=[pltpu.VMEM((hp, seq, MLA_NOPE_DIM + LANES), BF16),
                        pltpu.VMEM((hp, seq // tk, MLA_V_DIM, tk), BF16)],
        compiler_params=_cparams("parallel", "parallel", "arbitrary"),
        name="mla_attention",
    )(q, kv, kr, gates, y_prev)


def _router_kernel(x_ref, g_ref, r_ref, h_ref, idx_ref, w_ref):
    x = x_ref[...]
    h = x * lax.rsqrt(jnp.mean(x * x, axis=-1, keepdims=True) + EPS) * g_ref[...]
    h_ref[...] = _pack_bf16_pairs(h)
    logits = jnp.dot(h, r_ref[...], preferred_element_type=F32, precision=lax.Precision.HIGHEST)
    lane = lax.broadcasted_iota(jnp.int32, logits.shape, 1)
    lg = jnp.where(lane < N_EXPERTS, logits, -jnp.inf)
    m1 = jnp.max(lg, axis=-1, keepdims=True)
    i1 = jnp.min(jnp.where(lg == m1, lane, LANES), axis=-1, keepdims=True)
    lg2 = jnp.where(lane == i1, -jnp.inf, lg)
    m2 = jnp.max(lg2, axis=-1, keepdims=True)
    i2 = jnp.min(jnp.where(lg2 == m2, lane, LANES), axis=-1, keepdims=True)
    e2 = jnp.exp(m2 - m1)
    w1 = 1.0 / (1.0 + e2)
    w2 = e2 / (1.0 + e2)
    idx_ref[...] = jnp.where(lane == 0, i1, i2)
    w_ref[...] = jnp.where(lane == 0, w1, w2)


def moe_router(x, g, router):
    m, d = x.shape
    rows = min(ROW_TILE, m)
    r_pad = jnp.zeros((d, LANES), F32).at[:, :N_EXPERTS].set(router.astype(F32))
    h, idx, w = pl.pallas_call(
        _router_kernel,
        grid=(m // rows,),
        in_specs=[pl.BlockSpec((rows, d), lambda i: (i, 0)),
                  pl.BlockSpec((1, d), lambda i: (0, 0)),
                  pl.BlockSpec((d, LANES), lambda i: (0, 0))],
        out_specs=[pl.BlockSpec((rows, d // 2), lambda i: (i, 0)),
                   pl.BlockSpec((rows, LANES), lambda i: (i, 0)),
                   pl.BlockSpec((rows, LANES), lambda i: (i, 0))],
        out_shape=[jax.ShapeDtypeStruct((m, d // 2), jnp.int32),
                   jax.ShapeDtypeStruct((m, LANES), jnp.int32),
                   jax.ShapeDtypeStruct((m, LANES), F32)],
        compiler_params=_cparams("parallel"),
        name="moe_router",
    )(x, g.reshape(1, d).astype(F32), r_pad)
    return h, idx[:, :TOP_K], w[:, :TOP_K]


def _routing_tables(idx, wts, tile):
    t = idx.shape[0]
    pairs = t * TOP_K
    e = idx.reshape(pairs)
    onehot = (e[:, None] == jnp.arange(N_EXPERTS, dtype=jnp.int32)[None, :]).astype(jnp.int32)
    csum = jnp.cumsum(onehot, axis=0)
    rank = jnp.sum(csum * onehot, axis=1) - 1
    counts = csum[-1]
    padded = ((counts + tile - 1) // tile) * tile
    ends = jnp.cumsum(padded)
    starts = ends - padded
    pos = (jnp.sum(onehot * starts[None, :], axis=1) + rank).astype(jnp.int32)
    rows = pairs + N_EXPERTS * tile
    row_tok = jnp.zeros((rows,), jnp.int32).at[pos].set(jnp.arange(pairs, dtype=jnp.int32) // TOP_K)
    row_w = jnp.zeros((rows,), F32).at[pos].set(wts.reshape(pairs))
    tile_start = jnp.arange(rows // tile, dtype=jnp.int32) * tile
    tile_e = jnp.minimum(jnp.sum((tile_start[:, None] >= ends[None, :]).astype(jnp.int32), axis=1),
                         N_EXPERTS - 1).astype(jnp.int32)
    n_used = (ends[-1] // tile).astype(jnp.int32).reshape(1)
    n_tiles = rows // tile
    t_idx = jnp.arange(n_tiles, dtype=jnp.int32)
    prev_e = jnp.concatenate([jnp.full((1,), -1, jnp.int32), tile_e[:-1]])
    first = ((tile_e != prev_e) & (t_idx < n_used[0])).astype(jnp.int32)
    later_first = (t_idx[None, :] > t_idx[:, None]) & (first[None, :] == 1)
    next_idx = jnp.min(jnp.where(later_first, t_idx[None, :], n_tiles), axis=1)
    nxt = jnp.where(next_idx < n_tiles, tile_e[jnp.minimum(next_idx, n_tiles - 1)], -1).astype(jnp.int32)
    valid = jnp.clip((starts + counts)[tile_e] - tile_start, 0, tile).astype(jnp.int32)
    used_rows = (n_used * tile).astype(jnp.int32)
    return pos, row_tok, row_w.reshape(rows, 1), used_rows, (tile_e, first, nxt, valid)


def _row_copy(src_hbm, src_row, dst, dst_row, sem):
    return pltpu.make_async_copy(src_hbm.at[pl.ds(src_row, 1), :], dst.at[pl.ds(dst_row, 1), :], sem)


def _gather_kernel(tok_ref, used_ref, h_hbm, o_ref, buf, sem):
    tg = buf.shape[1]
    i = pl.program_id(0)
    slot = i % 2
    used_rows = used_ref[0]

    def issue(tile, dst_slot):
        def start(r, carry):
            _row_copy(h_hbm, tok_ref[tile * tg + r], buf.at[dst_slot], r, sem.at[dst_slot]).start()
            return carry
        lax.fori_loop(0, tg, start, 0, unroll=8)

    @pl.when((i == 0) & (used_rows > 0))
    def _():
        issue(0, 0)

    @pl.when((i + 1 < pl.num_programs(0)) & ((i + 1) * tg < used_rows))
    def _():
        issue(i + 1, 1 - slot)

    @pl.when(i * tg < used_rows)
    def _():
        def wait(r, carry):
            _row_copy(h_hbm, 0, buf.at[slot], r, sem.at[slot]).wait()
            return carry

        lax.fori_loop(0, tg, wait, 0, unroll=8)
        o_ref[...] = _unpack_bf16_pairs(buf[slot]).astype(o_ref.dtype)

    @pl.when(i * tg >= used_rows)
    def _():
        o_ref[...] = jnp.zeros_like(o_ref)


def moe_gather(h, row_tok, used_rows):
    rows = row_tok.shape[0]
    d = 2 * h.shape[1]
    tg = GATHER_TILE
    return pl.pallas_call(
        _gather_kernel,
        grid_spec=pltpu.PrefetchScalarGridSpec(
            num_scalar_prefetch=2,
            grid=(rows // tg,),
            in_specs=[pl.BlockSpec(memory_space=pl.ANY)],
            out_specs=pl.BlockSpec((tg, d), lambda i, tok, used: (i, 0)),
            scratch_shapes=[pltpu.VMEM((2, tg, d // 2), h.dtype), pltpu.SemaphoreType.DMA((2,))]),
        out_shape=jax.ShapeDtypeStruct((rows, d), BF16),
        compiler_params=_cparams("arbitrary"),
        name="moe_gather",
    )(row_tok, used_rows, h)


def _moe_mm_kernel(*refs, n_w, layer, bn, has_scale):
    te_ref, first_ref, nxt_ref, valid_ref, a_ref = refs[:5]
    w_hbm = refs[5:5 + n_w]
    rw_ref = refs[5 + n_w] if has_scale else None
    o_ref = refs[5 + n_w + int(has_scale)]
    stage = refs[-1 - 2 * n_w:-1 - n_w]
    wb = refs[-1 - n_w:-1]
    sem = refs[-1]
    j = pl.program_id(0)
    i = pl.program_id(1)

    def copy(e, jj, t):
        cols = pl.ds(pl.multiple_of(jj * bn, LANES), bn)
        return pltpu.make_async_copy(w_hbm[t].at[layer, e, :, cols], stage[t], sem.at[t])

    @pl.when(first_ref[i] == 1)
    def _():
        @pl.when((j == 0) & (i == 0))
        def _():
            for t in range(n_w):
                copy(te_ref[0], 0, t).start()

        for t in range(n_w):
            copy(te_ref[i], j, t).wait()
            wb[t][...] = stage[t][...].astype(BF16)

        nxt = nxt_ref[i]

        @pl.when(nxt >= 0)
        def _():
            for t in range(n_w):
                copy(nxt, j, t).start()

        @pl.when((nxt < 0) & (j + 1 < pl.num_programs(0)))
        def _():
            for t in range(n_w):
                copy(te_ref[0], j + 1, t).start()

    tm = a_ref.shape[0]
    half = tm // 2
    valid = valid_ref[i]

    def rows_out(rows):
        a = a_ref[rows, :]
        acc = _dot(a, wb[0][...])
        if n_w == 2:
            acc = _silu(acc) * _dot(a, wb[1][...])
        if has_scale:
            o_ref[rows, :] = _pack_bf16_pairs(rw_ref[rows, :] * acc)
        else:
            o_ref[rows, :] = acc.astype(o_ref.dtype)

    @pl.when(valid > half)
    def _():
        rows_out(slice(0, tm))

    @pl.when((valid > 0) & (valid <= half))
    def _():
        rows_out(slice(0, half))
        o_ref[half:, :] = jnp.zeros((tm - half, o_ref.shape[1]), o_ref.dtype)

    @pl.when(valid == 0)
    def _():
        o_ref[...] = jnp.zeros_like(o_ref)


def moe_matmul(a, ws, layer, tables, bn, row_scale=None):
    tile_e, first, nxt, valid = tables
    rows, k = a.shape
    n = ws[0].shape[-1]
    n_w = len(ws)
    tm = MOE_TILE
    bn = _tile(n, bn)
    packed = row_scale is not None
    out_bn, out_n, out_dtype = (bn // 2, n // 2, jnp.int32) if packed else (bn, n, BF16)
    idx = lambda j, i, *_: (i, 0)
    in_specs = [pl.BlockSpec((tm, k), idx)] + [pl.BlockSpec(memory_space=pl.ANY)] * n_w
    args = [a, *ws]
    if row_scale is not None:
        in_specs.append(pl.BlockSpec((tm, 1), idx))
        args.append(row_scale)
    scratch = ([pltpu.VMEM((k, bn), F32)] * n_w + [pltpu.VMEM((k, bn), BF16)] * n_w
               + [pltpu.SemaphoreType.DMA((n_w,))])
    return pl.pallas_call(
        functools.partial(_moe_mm_kernel, n_w=n_w, layer=layer, bn=bn, has_scale=row_scale is not None),
        grid_spec=pltpu.PrefetchScalarGridSpec(
            num_scalar_prefetch=4,
            grid=(n // bn, rows // tm),
            in_specs=in_specs,
            out_specs=pl.BlockSpec((tm, out_bn), lambda j, i, *_: (i, j)),
            scratch_shapes=scratch),
        out_shape=jax.ShapeDtypeStruct((rows, out_n), out_dtype),
        compiler_params=_cparams("arbitrary", "arbitrary"),
        name="moe_matmul",
    )(tile_e, first, nxt, valid, *args)


def _combine_kernel(pos_ref, x_ref, y_hbm, g_ref, o_ref, buf, sem, *, apply_norm, pack_tile):
    tb = x_ref.shape[0]
    i = pl.program_id(0)
    slot = i % 2

    def issue(tile, dst_slot):
        def start(r, carry):
            for s in range(TOP_K):
                _row_copy(y_hbm, pos_ref[(tile * tb + r) * TOP_K + s], buf.at[dst_slot, s], r,
                          sem.at[dst_slot]).start()
            return carry
        lax.fori_loop(0, tb, start, 0, unroll=4)

    @pl.when(i == 0)
    def _():
        issue(0, 0)

    @pl.when(i + 1 < pl.num_programs(0))
    def _():
        issue(i + 1, 1 - slot)

    def wait(r, carry):
        for s in range(TOP_K):
            _row_copy(y_hbm, 0, buf.at[slot, s], r, sem.at[slot]).wait()
        return carry

    lax.fori_loop(0, tb, wait, 0, unroll=4)
    x = x_ref[...]
    half_tile = pack_tile // 2
    for s in range(TOP_K):
        w = buf[slot, s]
        x = x + jnp.concatenate([_unpack_bf16_pairs(w[:, c:c + half_tile])
                                 for c in range(0, w.shape[1], half_tile)], axis=1)
    if apply_norm:
        x = x * lax.rsqrt(jnp.mean(x * x, axis=-1, keepdims=True) + EPS) * g_ref[...]
    o_ref[...] = x.astype(o_ref.dtype)


def moe_combine(x, y, pos, final_g, apply_norm, pack_tile):
    m, d = x.shape
    tb = GATHER_TILE
    return pl.pallas_call(
        functools.partial(_combine_kernel, apply_norm=apply_norm, pack_tile=pack_tile),
        grid_spec=pltpu.PrefetchScalarGridSpec(
            num_scalar_prefetch=1,
            grid=(m // tb,),
            in_specs=[pl.BlockSpec((tb, d), lambda i, p: (i, 0)),
                      pl.BlockSpec(memory_space=pl.ANY),
                      pl.BlockSpec((1, d), lambda i, p: (0, 0))],
            out_specs=pl.BlockSpec((tb, d), lambda i, p: (i, 0)),
            scratch_shapes=[pltpu.VMEM((2, TOP_K, tb, d // 2), y.dtype), pltpu.SemaphoreType.DMA((2,))]),
        out_shape=jax.ShapeDtypeStruct((m, d), x.dtype),
        compiler_params=_cparams("arbitrary"),
        name="moe_combine_norm",
    )(pos, x, y, final_g.reshape(1, d).astype(F32))


def _rope_tables(positions, dim, pad_to):
    inv = ROPE_THETA ** (-jnp.arange(0, dim, 2, dtype=F32) / dim)
    ang = positions.astype(F32).reshape(-1)[:, None] * inv
    cos, sin = jnp.cos(ang), jnp.sin(ang)
    pad = jnp.zeros((ang.shape[0], pad_to - dim), F32)
    return (jnp.concatenate([cos, cos, pad], axis=1), jnp.concatenate([-sin, sin, pad], axis=1))


def _swap_halves(w):
    half = w.shape[-1] // 2
    return jnp.concatenate([w[..., half:], w[..., :half]], axis=-1)


def kernel(x, positions, ln_mix, w_in, ret_norm, hgrn_norm, hgrn_lb_logits, mla_q_norm, mla_w_uq,
           mla_kv_norm, mla_w_ukv, w_out, ln_ffn, ffn_w1, ffn_w3, ffn_w2, moe_router_w, moe_w1,
           moe_w3, moe_w2, final_norm):
    batch, seq, d_model = x.shape
    depth = w_in.shape[0]
    q_rank, kv_rank = mla_q_norm.shape[1], mla_kv_norm.shape[1]
    ret_w = 2 * RET_HEADS * RET_QK_DIM + 2 * d_model
    hg_e = HGRN_HEADS * HGRN_EXPAND
    hg_w = 2 * hg_e + 2 * d_model
    mla_w = q_rank + kv_rank + MLA_ROPE_DIM
    mla_pad = -(-(q_rank + kv_rank + LANES) // (2 * LANES)) * (2 * LANES)
    w_in_t = jnp.swapaxes(w_in, 1, 2)

    cos_r, sin_r = _rope_tables(positions, RET_QK_DIM, RET_QK_DIM)
    cm, sm = _rope_tables(positions, MLA_ROPE_DIM, LANES)
    lb_all = jnp.cumsum(jax.nn.softmax(hgrn_lb_logits.astype(F32), axis=0), axis=0)
    lb_all = lb_all - lb_all[:1]

    xf = x.reshape(batch * seq, d_model)
    for l in range(depth):
        last = l == depth - 1
        o = ret_w + hg_w
        h = rmsnorm(xf, ln_mix[l], BF16)
        r = staged_matmul(h, [w_in_t], l, 0, ret_w, BF16, transposed=True)
        hf = staged_matmul(h, [w_in_t], l, ret_w, hg_e, F32, transposed=True)
        hr = staged_matmul(h, [w_in_t], l, ret_w + hg_e, hg_w - hg_e, BF16, transposed=True)
        pm = staged_matmul(h, [w_in_t], l, o, mla_pad, F32, transposed=True, bn=mla_pad // 2)
        gates = staged_matmul(h, [w_in_t], l, o + mla_w, N_BRANCHES * d_model, BF16, epilogue="sigmoid",
                              transposed=True)

        y = retention(r, gates, cos_r, sin_r, ret_norm[l], batch, seq)
        y = hgrn2(hf, hr, gates, y, lb_all[l], hgrn_norm[l], batch, seq)

        cqn, ckvn, kr = mla_prep(pm, mla_q_norm[l], mla_kv_norm[l], cm, sm)
        wq = mla_w_uq[l].reshape(q_rank, MLA_HEADS, MLA_NOPE_DIM + MLA_ROPE_DIM)
        wq_rope = wq[..., MLA_NOPE_DIM:]
        wq = jnp.concatenate([wq, _swap_halves(wq_rope)], axis=-1).reshape(q_rank, -1).astype(BF16)
        q_scale = (MLA_NOPE_DIM + MLA_ROPE_DIM) ** -0.5 * LOG2_E
        q = matmul(cqn, wq, BF16, bn=MLA_UP_BN, q_rope=(cm, sm, q_scale))
        kv = matmul(ckvn, mla_w_ukv[l].astype(BF16), BF16, bn=MLA_UP_BN)
        y = mla_attention(q, kv, kr, gates, y, batch, seq)
        xf = staged_matmul(y, [w_out], l, 0, d_model, F32, res=xf)

        if l % 2 == 0:
            j = l // 2
            h2 = rmsnorm(xf, ln_ffn[l], BF16)
            g = staged_matmul(h2, [ffn_w1, ffn_w3], j, 0, ffn_w1.shape[-1], BF16, epilogue="swiglu")
            xf = matmul_acc_res(g, ffn_w2[j].astype(BF16), xf, _tile(g.shape[1], FFN_DOWN_BK), bn=FFN_DOWN_BN)
            if last:
                xf = rmsnorm(xf, final_norm, x.dtype)
        else:
            j = l // 2
            h2, idx, wts = moe_router(xf, ln_ffn[l], moe_router_w[j])
            pos, row_tok, row_w, used_rows, tables = _routing_tables(idx, wts, MOE_TILE)
            xs = moe_gather(h2, row_tok, used_rows)
            g = moe_matmul(xs, [moe_w1, moe_w3], j, tables, MOE_UP_BN)
            down_bn = _tile(d_model, MOE_DOWN_BN)
            yrows = moe_matmul(g, [moe_w2], j, tables, down_bn, row_scale=row_w)
            xf = moe_combine(xf, yrows, pos, final_norm, apply_norm=last, pack_tile=down_bn)
    return xf.reshape(batch, seq, d_model)
```

```python
import functools

import jax
import jax.numpy as jnp
from jax import lax
from jax.experimental import pallas as pl
from jax.experimental.pallas import tpu as pltpu

F32 = jnp.float32
BF16 = jnp.bfloat16

RET_HEADS = 16
RET_QK_DIM = 128
HGRN_HEADS = 16
HGRN_EXPAND = 128
HGRN_CHUNK = 32
MLA_HEADS = 32
MLA_NOPE_DIM = 128
MLA_ROPE_DIM = 64
MLA_V_DIM = 128
N_BRANCHES = 3
N_EXPERTS = 8
TOP_K = 2
ROPE_THETA = 10000.0
EPS = 1e-6
LOG2_E = 1.4426950408889634
BF16_BITS = 16
HIGH_HALF_MASK = -65536

LANES = 128
SUBLANES = 8
VMEM_LIMIT_BYTES = 58 * 2 ** 20

ROW_TILE = 512
MM_BM = 1024
MM_BN = 512
IN_PROJ_BM = 2048
MLA_UP_BN = 2048
CAST_CHUNK = 512
RET_BLOCK = 512
HGRN_BLOCK = 256
MIXER_HEADS_PER_STEP = 4
ATTN_BLOCK = 512
ATTN_KV_BLOCK = 512
ATTN_HEADS_PER_STEP = 4
MOE_TILE = 512
MOE_UP_BN = 512
MOE_DOWN_BN = 1024
FFN_DOWN_BK = 3584
FFN_DOWN_BN = 1024
GATHER_TILE = 256


def _cparams(*sem):
    return pltpu.CompilerParams(dimension_semantics=sem, vmem_limit_bytes=VMEM_LIMIT_BYTES)


def _tile(n, preferred):
    for t in range(min(preferred, n) // LANES * LANES, 0, -LANES):
        if n % t == 0:
            return t
    return n


def _dot(a, b):
    return jnp.dot(a, b, preferred_element_type=F32)


def _dot_nt(a, b):
    return lax.dot_general(a, b, (((1,), (1,)), ((), ())), preferred_element_type=F32)


def _sigmoid(x):
    return 0.5 * jnp.tanh(0.5 * x) + 0.5


def _silu(x):
    return x * _sigmoid(x)


def _pack_bf16_pairs(x):
    c = x.shape[1] // 2
    lo = lax.bitcast_convert_type(x[:, :c].astype(BF16).astype(F32), jnp.int32)
    hi = lax.bitcast_convert_type(x[:, c:].astype(BF16).astype(F32), jnp.int32)
    return lax.shift_right_logical(lo, BF16_BITS) | (hi & HIGH_HALF_MASK)


def _unpack_bf16_pairs(w):
    lo = lax.bitcast_convert_type(lax.shift_left(w, BF16_BITS), F32)
    hi = lax.bitcast_convert_type(w & HIGH_HALF_MASK, F32)
    return jnp.concatenate([lo, hi], axis=1)


def _rmsnorm_kernel(x_ref, g_ref, o_ref):
    x = x_ref[...].astype(F32)
    ms = jnp.mean(x * x, axis=-1, keepdims=True)
    o_ref[...] = (x * lax.rsqrt(ms + EPS) * g_ref[...]).astype(o_ref.dtype)


def rmsnorm(x, g, out_dtype):
    m, d = x.shape
    rows = min(ROW_TILE, m)
    return pl.pallas_call(
        _rmsnorm_kernel,
        grid=(m // rows,),
        in_specs=[pl.BlockSpec((rows, d), lambda i: (i, 0)),
                  pl.BlockSpec((1, d), lambda i: (0, 0))],
        out_specs=pl.BlockSpec((rows, d), lambda i: (i, 0)),
        out_shape=jax.ShapeDtypeStruct((m, d), out_dtype),
        compiler_params=_cparams("parallel"),
        name="rmsnorm",
    )(x, g.reshape(1, d).astype(F32))


def _mm_kernel(*refs, q_scale):
    a_ref, w_ref = refs[0], refs[1]
    o_ref = refs[-1]
    acc = _dot(a_ref[...], w_ref[...])
    if q_scale is None:
        o_ref[...] = acc.astype(o_ref.dtype)
        return
    cm, sm = refs[2][...], refs[3][...]
    head_w = MLA_NOPE_DIM + 2 * MLA_ROPE_DIM
    for c in range(0, acc.shape[1], head_w):
        qr = acc[:, c + MLA_NOPE_DIM:c + head_w]
        qr = qr * cm + pltpu.roll(qr, MLA_ROPE_DIM, 1) * sm
        o_ref[:, c:c + MLA_NOPE_DIM] = (acc[:, c:c + MLA_NOPE_DIM] * q_scale).astype(o_ref.dtype)
        o_ref[:, c + MLA_NOPE_DIM:c + head_w] = (qr * q_scale).astype(o_ref.dtype)


def matmul(a, w, out_dtype, bm=MM_BM, bn=MM_BN, q_rope=None):
    m, k = a.shape
    n = w.shape[1]
    bm = _tile(m, bm)
    bn = _tile(n, bn)
    in_specs = [pl.BlockSpec((bm, k), lambda j, i: (i, 0)),
                pl.BlockSpec((k, bn), lambda j, i: (0, j))]
    args = [a, w]
    if q_rope is not None:
        in_specs += [pl.BlockSpec((bm, LANES), lambda j, i: (i, 0))] * 2
        args += [q_rope[0], q_rope[1]]
    return pl.pallas_call(
        functools.partial(_mm_kernel, q_scale=None if q_rope is None else q_rope[2]),
        grid=(n // bn, m // bm),
        in_specs=in_specs,
        out_specs=pl.BlockSpec((bm, bn), lambda j, i: (i, j)),
        out_shape=jax.ShapeDtypeStruct((m, n), out_dtype),
        compiler_params=_cparams("parallel", "parallel"),
        name="matmul",
    )(*args)


def _staged_mm_kernel(*refs, n_w, layer, col0, bn, epilogue, has_res, transposed):
    a_ref = refs[0]
    w_hbm = refs[1:1 + n_w]
    res_ref = refs[1 + n_w] if has_res else None
    o_ref = refs[1 + n_w + int(has_res)]
    stage = refs[-1 - 2 * n_w:-1 - n_w]
    wb = refs[-1 - n_w:-1]
    sem = refs[-1]
    j = pl.program_id(0)
    i = pl.program_id(1)

    def copy(jj, t):
        if transposed:
            rows = pl.ds(pl.multiple_of(col0 + jj * bn, SUBLANES), bn)
            return pltpu.make_async_copy(w_hbm[t].at[layer, rows, :], stage[t], sem.at[t])
        cols = pl.ds(pl.multiple_of(col0 + jj * bn, LANES), bn)
        return pltpu.make_async_copy(w_hbm[t].at[layer, :, cols], stage[t], sem.at[t])

    def cast(t):
        if not transposed:
            wb[t][...] = stage[t][...].astype(BF16)
            return
        k = wb[t].shape[0]
        step = _tile(k, CAST_CHUNK)
        for c in range(0, k, step):
            wb[t][c:c + step, :] = stage[t][:, c:c + step].T.astype(BF16)

    @pl.when(i == 0)
    def _():
        @pl.when(j == 0)
        def _():
            for t in range(n_w):
                copy(0, t).start()

        for t in range(n_w):
            copy(j, t).wait()
            cast(t)

        @pl.when(j + 1 < pl.num_programs(0))
        def _():
            for t in range(n_w):
                copy(j + 1, t).start()

    a = a_ref[...]
    acc = _dot(a, wb[0][...])
    if epilogue == "swiglu":
        acc = _silu(acc) * _dot(a, wb[1][...])
    elif epilogue == "sigmoid":
        acc = _sigmoid(acc)
    if has_res:
        acc = res_ref[...] + acc
    o_ref[...] = acc.astype(o_ref.dtype)


def staged_matmul(a, ws, layer, col0, n, out_dtype, epilogue=None, res=None, transposed=False,
                  bm=MM_BM, bn=MM_BN):
    m, k = a.shape
    n_w = len(ws)
    bm = _tile(m, bm)
    bn = _tile(n, bn)
    assert col0 % (SUBLANES if transposed else LANES) == 0 and n % bn == 0
    in_specs = [pl.BlockSpec((bm, k), lambda j, i: (i, 0))]
    in_specs += [pl.BlockSpec(memory_space=pl.ANY)] * n_w
    args = [a, *ws]
    if res is not None:
        in_specs.append(pl.BlockSpec((bm, bn), lambda j, i: (i, j)))
        args.append(res)
    stage_shape = (bn, k) if transposed else (k, bn)
    scratch = ([pltpu.VMEM(stage_shape, F32)] * n_w + [pltpu.VMEM((k, bn), BF16)] * n_w
               + [pltpu.SemaphoreType.DMA((n_w,))])
    return pl.pallas_call(
        functools.partial(_staged_mm_kernel, n_w=n_w, layer=layer, col0=col0, bn=bn,
                          epilogue=epilogue, has_res=res is not None, transposed=transposed),
        grid=(n // bn, m // bm),
        in_specs=in_specs,
        out_specs=pl.BlockSpec((bm, bn), lambda j, i: (i, j)),
        out_shape=jax.ShapeDtypeStruct((m, n), out_dtype),
        scratch_shapes=scratch,
        compiler_params=_cparams("arbitrary", "arbitrary"),
        name="staged_matmul",
    )(*args)


def _mm_acc_kernel(a_ref, w_ref, res_ref, o_ref, acc_ref):
    kk = pl.program_id(2)

    @pl.when(kk == 0)
    def _():
        acc_ref[...] = jnp.zeros_like(acc_ref)

    acc_ref[...] += _dot(a_ref[...], w_ref[...])

    @pl.when(kk == pl.num_programs(2) - 1)
    def _():
        o_ref[...] = (res_ref[...] + acc_ref[...]).astype(o_ref.dtype)


def matmul_acc_res(a, w, res, bk, bm=MM_BM, bn=MM_BN):
    m, k = a.shape
    n = w.shape[1]
    bm = _tile(m, bm)
    bn = _tile(n, bn)
    return pl.pallas_call(
        _mm_acc_kernel,
        grid=(n // bn, m // bm, k // bk),
        in_specs=[pl.BlockSpec((bm, bk), lambda j, i, kk: (i, kk)),
                  pl.BlockSpec((bk, bn), lambda j, i, kk: (kk, j)),
                  pl.BlockSpec((bm, bn), lambda j, i, kk: (i, j))],
        out_specs=pl.BlockSpec((bm, bn), lambda j, i, kk: (i, j)),
        out_shape=jax.ShapeDtypeStruct((m, n), res.dtype),
        scratch_shapes=[pltpu.VMEM((bm, bn), F32)],
        compiler_params=_cparams("parallel", "parallel", "arbitrary"),
        name="matmul_acc_res",
    )(a, w, res)


def _head_norm_gate(o, gain, g):
    y = o * lax.rsqrt(jnp.mean(o * o, axis=-1, keepdims=True) + EPS) * gain
    return y * _silu(g.astype(F32))


def _retention_kernel(q_ref, k_ref, v_ref, g_ref, mg_ref, cos_ref, sin_ref, dmat_ref, qdec_ref, kdec_ref,
                      cdec_ref, gain_ref, o_ref, st_ref):
    @pl.when(pl.program_id(2) == 0)
    def _():
        st_ref[...] = jnp.zeros_like(st_ref)

    cos = cos_ref[...]
    sin = sin_ref[...]
    dk = RET_QK_DIM
    half = dk // 2
    hp = st_ref.shape[0]
    dv = st_ref.shape[1]
    for hh in range(hp):
        qk_cols = slice(hh * dk, (hh + 1) * dk)
        v_cols = slice(hh * dv, (hh + 1) * dv)
        q = q_ref[:, qk_cols].astype(F32)
        k = k_ref[:, qk_cols].astype(F32)
        qr = q * cos + pltpu.roll(q, half, 1) * sin
        kr = (k * cos + pltpu.roll(k, half, 1) * sin) * (dk ** -0.5)
        v = v_ref[:, v_cols]
        scores = _dot_nt(qr.astype(BF16), kr.astype(BF16))
        st = st_ref[hh]
        cross = _dot_nt((qr * qdec_ref[hh]).astype(BF16), st.astype(BF16))
        v_t = v.astype(F32).T.astype(BF16)
        st_ref[hh] = cdec_ref[hh] * st + _dot(v_t, (kr * kdec_ref[hh]).astype(BF16))
        intra = _dot((scores * dmat_ref[hh]).astype(BF16), v)
        y = _head_norm_gate(intra + cross, gain_ref[:, v_cols], g_ref[:, v_cols])
        o_ref[:, v_cols] = (mg_ref[:, v_cols].astype(F32) * y).astype(o_ref.dtype)


def retention(r, gates, cos2, sin2, norm_g, batch, seq):
    heads, dk = RET_HEADS, RET_QK_DIM
    dv = (r.shape[1] - 2 * heads * dk) // (2 * heads)
    blk = min(RET_BLOCK, seq)
    nblk = seq // blk
    log_gamma = jnp.log(1.0 - jnp.exp2(-5.0 - jnp.arange(heads, dtype=F32)))
    idx = jnp.arange(blk, dtype=F32)
    rel = idx[:, None] - idx[None, :]
    dmat = jnp.where(rel >= 0, jnp.exp(log_gamma[:, None, None] * jnp.maximum(rel, 0.0)), 0.0)
    qdec = jnp.broadcast_to(jnp.exp(log_gamma[:, None] * (idx + 1.0))[:, :, None], (heads, blk, dk))
    kdec = jnp.broadcast_to(jnp.exp(log_gamma[:, None] * (blk - 1.0 - idx))[:, :, None], (heads, blk, dk))
    cdec = jnp.broadcast_to(jnp.exp(log_gamma * blk)[:, None, None], (heads, 1, dk))
    hp = MIXER_HEADS_PER_STEP
    groups = heads // hp
    assert (2 * heads * dk) % (hp * dv) == 0
    v_off = 2 * heads * dk // (hp * dv)
    g_off = v_off + groups
    row = lambda b, h, t: b * nblk + t
    return pl.pallas_call(
        _retention_kernel,
        grid=(batch, groups, nblk),
        in_specs=[pl.BlockSpec((blk, hp * dk), lambda b, h, t: (row(b, h, t), h)),
                  pl.BlockSpec((blk, hp * dk), lambda b, h, t: (row(b, h, t), groups + h)),
                  pl.BlockSpec((blk, hp * dv), lambda b, h, t: (row(b, h, t), v_off + h)),
                  pl.BlockSpec((blk, hp * dv), lambda b, h, t: (row(b, h, t), g_off + h)),
                  pl.BlockSpec((blk, hp * dv), lambda b, h, t: (row(b, h, t), h)),
                  pl.BlockSpec((blk, dk), lambda b, h, t: (row(b, h, t), 0)),
                  pl.BlockSpec((blk, dk), lambda b, h, t: (row(b, h, t), 0)),
                  pl.BlockSpec((hp, blk, blk), lambda b, h, t: (h, 0, 0)),
                  pl.BlockSpec((hp, blk, dk), lambda b, h, t: (h, 0, 0)),
                  pl.BlockSpec((hp, blk, dk), lambda b, h, t: (h, 0, 0)),
                  pl.BlockSpec((hp, 1, dk), lambda b, h, t: (h, 0, 0)),
                  pl.BlockSpec((1, hp * dv), lambda b, h, t: (0, h))],
        out_specs=pl.BlockSpec((blk, hp * dv), lambda b, h, t: (row(b, h, t), h)),
        out_shape=jax.ShapeDtypeStruct((batch * seq, heads * dv), BF16),
        scratch_shapes=[pltpu.VMEM((hp, dv, dk), F32)],
        compiler_params=_cparams("parallel", "parallel", "arbitrary"),
        name="retention",
    )(r, r, r, r, gates, cos2, sin2, dmat, qdec, kdec, cdec, norm_g.reshape(1, -1).astype(F32))


def _hgrn_kernel(f_ref, q_ref, i_ref, og_ref, mg_ref, y_ref, la_ref, l1_ref, oml_ref, gain_ref, o_ref,
                 st_ref):
    @pl.when(pl.program_id(2) == 0)
    def _():
        st_ref[...] = jnp.zeros_like(st_ref)

    blk = f_ref.shape[0]
    hp, dv, e = st_ref.shape
    c = HGRN_CHUNK
    nch = blk // c
    row = lax.broadcasted_iota(jnp.int32, (blk, e), 0)
    row_in_chunk = row % c
    chunk_of_row = row // c
    r2 = lax.broadcasted_iota(jnp.int32, (blk, blk), 0)
    c2 = lax.broadcasted_iota(jnp.int32, (blk, blk), 1)
    keep = (r2 // c == c2 // c) & (c2 <= r2)

    prep = []
    for hh in range(hp):
        e_cols = slice(hh * e, (hh + 1) * e)
        z = f_ref[:, e_cols]
        log_sig = jnp.minimum(z, 0.0) - jnp.log1p(jnp.exp(-jnp.abs(z)))
        la = la_ref[:, e_cols]
        u = l1_ref[:, e_cols] + log_sig
        log_f = jnp.maximum(la, u) + jnp.log1p(jnp.exp(-jnp.abs(la - u)))
        key = oml_ref[:, e_cols] * (1.0 / (1.0 + jnp.exp(z)))
        b = log_f
        shift = 1
        while shift < c:
            b = b + jnp.where(row_in_chunk >= shift, pltpu.roll(b, shift, 0), 0.0)
            shift *= 2
        b3 = b.reshape(nch, c, e)
        b_mid = jnp.broadcast_to(b3[:, c // 2 - 1:c // 2, :], (nch, c, e)).reshape(blk, e)
        b_end_rows = b3[:, c - 1:c, :]
        b_end = jnp.broadcast_to(b_end_rows, (nch, c, e)).reshape(blk, e)
        q = q_ref[:, e_cols].astype(F32)
        v = i_ref[:, hh * dv:(hh + 1) * dv]
        qf = (q * jnp.exp(b - b_mid)).astype(BF16)
        kf = (key * jnp.exp(b_mid - b)).astype(BF16)
        q_in = (q * jnp.exp(b)).astype(BF16)
        k_out = (key * jnp.exp(b_end - b)).astype(BF16)
        zero = jnp.zeros_like(q_in)
        k_wide = jnp.concatenate([jnp.where(chunk_of_row == j, k_out, zero) for j in range(nch)], axis=1)
        q_wide = jnp.concatenate([jnp.where(chunk_of_row == j, q_in, zero) for j in range(nch)], axis=1)
        v_t = v.astype(F32).T.astype(BF16)
        prep.append((qf, kf, k_wide, q_wide, v, v_t, jnp.exp(b_end_rows)))

    scores = [_dot_nt(p[0], p[1]) for p in prep]
    upds = [_dot(p[5], p[2]) for p in prep]
    for hh in range(hp):
        _, _, _, q_wide, v, _, dec = prep[hh]
        intra = _dot(jnp.where(keep, scores[hh], 0.0).astype(BF16), v)
        st = st_ref[hh]
        states = []
        for j in range(nch):
            states.append(st.astype(BF16))
            st = st * dec[j] + upds[hh][:, j * e:(j + 1) * e]
        st_ref[hh] = st
        inter = _dot_nt(q_wide, jnp.concatenate(states, axis=1))
        v_cols = slice(hh * dv, (hh + 1) * dv)
        y = _head_norm_gate(intra + inter, gain_ref[:, v_cols], og_ref[:, v_cols])
        o_ref[:, v_cols] = (y_ref[:, v_cols].astype(F32) + mg_ref[:, v_cols].astype(F32) * y).astype(o_ref.dtype)


def hgrn2(hf, hr, gates, y_prev, lb, norm_g, batch, seq):
    heads, e = HGRN_HEADS, HGRN_EXPAND
    dv = (hr.shape[1] - heads * e) // (2 * heads)
    blk = min(HGRN_BLOCK, seq)
    nblk = seq // blk
    lb = lb.astype(F32).reshape(1, -1)
    la, l1, oml = jnp.log(lb), jnp.log1p(-lb), 1.0 - lb
    hp = MIXER_HEADS_PER_STEP
    groups = heads // hp
    assert (heads * e) % (hp * dv) == 0
    i_off = heads * e // (hp * dv)
    og_off = i_off + groups
    row = lambda b, h, t: b * nblk + t
    vec = pl.BlockSpec((1, hp * e), lambda b, h, t: (0, h))
    return pl.pallas_call(
        _hgrn_kernel,
        grid=(batch, groups, nblk),
        in_specs=[pl.BlockSpec((blk, hp * e), lambda b, h, t: (row(b, h, t), h)),
                  pl.BlockSpec((blk, hp * e), lambda b, h, t: (row(b, h, t), h)),
                  pl.BlockSpec((blk, hp * dv), lambda b, h, t: (row(b, h, t), i_off + h)),
                  pl.BlockSpec((blk, hp * dv), lambda b, h, t: (row(b, h, t), og_off + h)),
                  pl.BlockSpec((blk, hp * dv), lambda b, h, t: (row(b, h, t), groups + h)),
                  pl.BlockSpec((blk, hp * dv), lambda b, h, t: (row(b, h, t), h)),
                  vec, vec, vec,
                  pl.BlockSpec((1, hp * dv), lambda b, h, t: (0, h))],
        out_specs=pl.BlockSpec((blk, hp * dv), lambda b, h, t: (row(b, h, t), h)),
        out_shape=jax.ShapeDtypeStruct((batch * seq, heads * dv), BF16),
        scratch_shapes=[pltpu.VMEM((hp, dv, e), F32)],
        compiler_params=_cparams("parallel", "parallel", "arbitrary"),
        name="hgrn2",
    )(hf, hr, hr, hr, gates, y_prev, la, l1, oml, norm_g.reshape(1, -1).astype(F32))


def _mla_prep_kernel(p_ref, qg_ref, kvg_ref, cm_ref, sm_ref, cq_ref, ckv_ref, kr_ref, *, q_rank, kv_rank):
    p = p_ref[...]

    def norm(x, g):
        return x * lax.rsqrt(jnp.mean(x * x, axis=-1, keepdims=True) + EPS) * g

    cq_ref[...] = norm(p[:, :q_rank], qg_ref[...]).astype(cq_ref.dtype)
    ckv_ref[...] = norm(p[:, q_rank:q_rank + kv_rank], kvg_ref[...]).astype(ckv_ref.dtype)
    kr = p[:, q_rank + kv_rank:q_rank + kv_rank + LANES]
    lane = lax.broadcasted_iota(jnp.int32, kr.shape, 1)
    half = MLA_ROPE_DIM // 2
    swapped = jnp.where(lane < MLA_ROPE_DIM + half, pltpu.roll(kr, half, 1),
                        pltpu.roll(kr, MLA_ROPE_DIM + half, 1))
    pair = jnp.where(lane < MLA_ROPE_DIM, kr, swapped)
    kr_ref[...] = (pair * cm_ref[...] + pltpu.roll(pair, MLA_ROPE_DIM, 1) * sm_ref[...]).astype(kr_ref.dtype)


def mla_prep(pm, q_norm, kv_norm, cm, sm):
    m = pm.shape[0]
    q_rank, kv_rank = q_norm.shape[0], kv_norm.shape[0]
    rows = min(ROW_TILE, m)
    full = lambda w: pl.BlockSpec((rows, w), lambda i: (i, 0))
    return pl.pallas_call(
        functools.partial(_mla_prep_kernel, q_rank=q_rank, kv_rank=kv_rank),
        grid=(m // rows,),
        in_specs=[full(pm.shape[1]),
                  pl.BlockSpec((1, q_rank), lambda i: (0, 0)),
                  pl.BlockSpec((1, kv_rank), lambda i: (0, 0)),
                  full(LANES), full(LANES)],
        out_specs=[full(q_rank), full(kv_rank), full(LANES)],
        out_shape=[jax.ShapeDtypeStruct((m, q_rank), BF16),
                   jax.ShapeDtypeStruct((m, kv_rank), BF16),
                   jax.ShapeDtypeStruct((m, LANES), BF16)],
        compiler_params=_cparams("parallel"),
        name="mla_prep",
    )(pm, q_norm.reshape(1, -1).astype(F32), kv_norm.reshape(1, -1).astype(F32), cm, sm)


def _attn_kernel(q_ref, kv_ref, kr_ref, mg_ref, y_ref, o_ref, kcat_ref, vt_ref, *, heads_per_step, tk):
    i = pl.program_id(2)
    tq = q_ref.shape[0]
    kv_w = MLA_NOPE_DIM + MLA_V_DIM
    q_w = MLA_NOPE_DIM + 2 * MLA_ROPE_DIM

    @pl.when(i == 0)
    def _():
        for hh in range(heads_per_step):
            kcat_ref[hh, :, :MLA_NOPE_DIM] = kv_ref[:, hh * kv_w:hh * kv_w + MLA_NOPE_DIM]
            kcat_ref[hh, :, MLA_NOPE_DIM:] = kr_ref[...]
            for jj in range(vt_ref.shape[1]):
                v = kv_ref[jj * tk:(jj + 1) * tk, hh * kv_w + MLA_NOPE_DIM:(hh + 1) * kv_w]
                vt_ref[hh, jj] = v.astype(F32).T.astype(BF16)

    qfs = [q_ref[:, hh * q_w:(hh + 1) * q_w] for hh in range(heads_per_step)]

    def step(j, carry, masked):
        rows = pl.ds(pl.multiple_of(j * tk, tk), tk)
        out = []
        scores = [_dot_nt(kcat_ref[hh, rows, :], qfs[hh]) for hh in range(heads_per_step)]
        for hh in range(heads_per_step):
            m, l, acc = carry[hh]
            s = scores[hh]
            if masked:
                kpos = lax.broadcasted_iota(jnp.int32, s.shape, 0) + j * tk
                qpos = lax.broadcasted_iota(jnp.int32, s.shape, 1) + i * tq
                s = jnp.where(kpos <= qpos, s, -jnp.inf)
            m_new = jnp.maximum(m, jnp.max(s, axis=0, keepdims=True))
            p = jnp.exp2(s - m_new)
            alpha = jnp.exp2(m - m_new)
            l = alpha * l + jnp.sum(p, axis=0, keepdims=True)
            acc = alpha * acc + _dot(vt_ref[hh, j], p.astype(BF16))
            out.append((m_new, l, acc))
        return tuple(out)

    init = tuple((jnp.full((1, tq), -jnp.inf, F32), jnp.zeros((1, tq), F32), jnp.zeros((MLA_V_DIM, tq), F32))
                 for _ in range(heads_per_step))
    per_q = tq // tk
    carry = lax.fori_loop(0, i * per_q, lambda j, cr: step(j, cr, False), init)
    for d in range(per_q):
        carry = step(i * per_q + d, carry, True)
    for hh in range(heads_per_step):
        _, l, acc = carry[hh]
        cols = slice(hh * MLA_V_DIM, (hh + 1) * MLA_V_DIM)
        y = y_ref[:, cols].astype(F32) + mg_ref[:, cols].astype(F32) * (acc / l).T
        o_ref[:, cols] = y.astype(o_ref.dtype)


def mla_attention(q, kv, kr, gates, y_prev, batch, seq):
    heads = MLA_HEADS
    hp = ATTN_HEADS_PER_STEP
    tq = min(ATTN_BLOCK, seq)
    tk = min(ATTN_KV_BLOCK, tq)
    nq = seq // tq
    qw = MLA_NOPE_DIM + 2 * MLA_ROPE_DIM
    kvw = MLA_NOPE_DIM + MLA_V_DIM
    return pl.pallas_call(
        functools.partial(_attn_kernel, heads_per_step=hp, tk=tk),
        grid=(batch, heads // hp, nq),
        in_specs=[pl.BlockSpec((tq, hp * qw), lambda b, h, i: (b * nq + i, h)),
                  pl.BlockSpec((seq, hp * kvw), lambda b, h, i: (b, h)),
                  pl.BlockSpec((seq, LANES), lambda b, h, i: (b, 0)),
                  pl.BlockSpec((tq, hp * MLA_V_DIM), lambda b, h, i: (b * nq + i, 2 * (heads // hp) + h)),
                  pl.BlockSpec((tq, hp * MLA_V_DIM), lambda b, h, i: (b * nq + i, h))],
        out_specs=pl.BlockSpec((tq, hp * MLA_V_DIM), lambda b, h, i: (b * nq + i, h)),
        out_shape=jax.ShapeDtypeStruct((batch * seq, heads * MLA_V_DIM), BF16),
        scratch_shapes=[pltpu.VMEM((hp, seq, MLA_NOPE_DIM + LANES), BF16),
                        pltpu.VMEM((hp, seq // tk, MLA_V_DIM, tk), BF16)],
        compiler_params=_cparams("parallel", "parallel", "arbitrary"),
        name="mla_attention",
    )(q, kv, kr, gates, y_prev)


def _router_kernel(x_ref, g_ref, r_ref, h_ref, idx_ref, w_ref):
    x = x_ref[...]
    h = x * lax.rsqrt(jnp.mean(x * x, axis=-1, keepdims=True) + EPS) * g_ref[...]
    h_ref[...] = _pack_bf16_pairs(h)
    logits = jnp.dot(h, r_ref[...], preferred_element_type=F32, precision=lax.Precision.HIGHEST)
    lane = lax.broadcasted_iota(jnp.int32, logits.shape, 1)
    lg = jnp.where(lane < N_EXPERTS, logits, -jnp.inf)
    m1 = jnp.max(lg, axis=-1, keepdims=True)
    i1 = jnp.min(jnp.where(lg == m1, lane, LANES), axis=-1, keepdims=True)
    lg2 = jnp.where(lane == i1, -jnp.inf, lg)
    m2 = jnp.max(lg2, axis=-1, keepdims=True)
    i2 = jnp.min(jnp.where(lg2 == m2, lane, LANES), axis=-1, keepdims=True)
    e2 = jnp.exp(m2 - m1)
    w1 = 1.0 / (1.0 + e2)
    w2 = e2 / (1.0 + e2)
    idx_ref[...] = jnp.where(lane == 0, i1, i2)
    w_ref[...] = jnp.where(lane == 0, w1, w2)


def moe_router(x, g, router):
    m, d = x.shape
    rows = min(ROW_TILE, m)
    r_pad = jnp.zeros((d, LANES), F32).at[:, :N_EXPERTS].set(router.astype(F32))
    h, idx, w = pl.pallas_call(
        _router_kernel,
        grid=(m // rows,),
        in_specs=[pl.BlockSpec((rows, d), lambda i: (i, 0)),
                  pl.BlockSpec((1, d), lambda i: (0, 0)),
                  pl.BlockSpec((d, LANES), lambda i: (0, 0))],
        out_specs=[pl.BlockSpec((rows, d // 2), lambda i: (i, 0)),
                   pl.BlockSpec((rows, LANES), lambda i: (i, 0)),
                   pl.BlockSpec((rows, LANES), lambda i: (i, 0))],
        out_shape=[jax.ShapeDtypeStruct((m, d // 2), jnp.int32),
                   jax.ShapeDtypeStruct((m, LANES), jnp.int32),
                   jax.ShapeDtypeStruct((m, LANES), F32)],
        compiler_params=_cparams("parallel"),
        name="moe_router",
    )(x, g.reshape(1, d).astype(F32), r_pad)
    return h, idx[:, :TOP_K], w[:, :TOP_K]


def _routing_tables(idx, wts, tile):
    t = idx.shape[0]
    pairs = t * TOP_K
    e = idx.reshape(pairs)
    onehot = (e[:, None] == jnp.arange(N_EXPERTS, dtype=jnp.int32)[None, :]).astype(jnp.int32)
    csum = jnp.cumsum(onehot, axis=0)
    rank = jnp.sum(csum * onehot, axis=1) - 1
    counts = csum[-1]
    padded = ((counts + tile - 1) // tile) * tile
    ends = jnp.cumsum(padded)
    starts = ends - padded
    pos = (jnp.sum(onehot * starts[None, :], axis=1) + rank).astype(jnp.int32)
    rows = pairs + N_EXPERTS * tile
    row_tok = jnp.zeros((rows,), jnp.int32).at[pos].set(jnp.arange(pairs, dtype=jnp.int32) // TOP_K)
    row_w = jnp.zeros((rows,), F32).at[pos].set(wts.reshape(pairs))
    tile_start = jnp.arange(rows // tile, dtype=jnp.int32) * tile
    tile_e = jnp.minimum(jnp.sum((tile_start[:, None] >= ends[None, :]).astype(jnp.int32), axis=1),
                         N_EXPERTS - 1).astype(jnp.int32)
    n_used = (ends[-1] // tile).astype(jnp.int32).reshape(1)
    n_tiles = rows // tile
    t_idx = jnp.arange(n_tiles, dtype=jnp.int32)
    prev_e = jnp.concatenate([jnp.full((1,), -1, jnp.int32), tile_e[:-1]])
    first = ((tile_e != prev_e) & (t_idx < n_used[0])).astype(jnp.int32)
    later_first = (t_idx[None, :] > t_idx[:, None]) & (first[None, :] == 1)
    next_idx = jnp.min(jnp.where(later_first, t_idx[None, :], n_tiles), axis=1)
    nxt = jnp.where(next_idx < n_tiles, tile_e[jnp.minimum(next_idx, n_tiles - 1)], -1).astype(jnp.int32)
    valid = jnp.clip((starts + counts)[tile_e] - tile_start, 0, tile).astype(jnp.int32)
    used_rows = (n_used * tile).astype(jnp.int32)
    return pos, row_tok, row_w.reshape(rows, 1), used_rows, (tile_e, first, nxt, valid)


def _row_copy(src_hbm, src_row, dst, dst_row, sem):
    return pltpu.make_async_copy(src_hbm.at[pl.ds(src_row, 1), :], dst.at[pl.ds(dst_row, 1), :], sem)


def _gather_kernel(tok_ref, used_ref, h_hbm, o_ref, buf, sem):
    tg = buf.shape[1]
    i = pl.program_id(0)
    slot = i % 2
    used_rows = used_ref[0]

    def issue(tile, dst_slot):
        def start(r, carry):
            _row_copy(h_hbm, tok_ref[tile * tg + r], buf.at[dst_slot], r, sem.at[dst_slot]).start()
            return carry
        lax.fori_loop(0, tg, start, 0, unroll=8)

    @pl.when((i == 0) & (used_rows > 0))
    def _():
        issue(0, 0)

    @pl.when((i + 1 < pl.num_programs(0)) & ((i + 1) * tg < used_rows))
    def _():
        issue(i + 1, 1 - slot)

    @pl.when(i * tg < used_rows)
    def _():
        def wait(r, carry):
            _row_copy(h_hbm, 0, buf.at[slot], r, sem.at[slot]).wait()
            return carry

        lax.fori_loop(0, tg, wait, 0, unroll=8)
        o_ref[...] = _unpack_bf16_pairs(buf[slot]).astype(o_ref.dtype)

    @pl.when(i * tg >= used_rows)
    def _():
        o_ref[...] = jnp.zeros_like(o_ref)


def moe_gather(h, row_tok, used_rows):
    rows = row_tok.shape[0]
    d = 2 * h.shape[1]
    tg = GATHER_TILE
    return pl.pallas_call(
        _gather_kernel,
        grid_spec=pltpu.PrefetchScalarGridSpec(
            num_scalar_prefetch=2,
            grid=(rows // tg,),
            in_specs=[pl.BlockSpec(memory_space=pl.ANY)],
            out_specs=pl.BlockSpec((tg, d), lambda i, tok, used: (i, 0)),
            scratch_shapes=[pltpu.VMEM((2, tg, d // 2), h.dtype), pltpu.SemaphoreType.DMA((2,))]),
        out_shape=jax.ShapeDtypeStruct((rows, d), BF16),
        compiler_params=_cparams("arbitrary"),
        name="moe_gather",
    )(row_tok, used_rows, h)


def _moe_mm_kernel(*refs, n_w, layer, bn, has_scale):
    te_ref, first_ref, nxt_ref, valid_ref, a_ref = refs[:5]
    w_hbm = refs[5:5 + n_w]
    rw_ref = refs[5 + n_w] if has_scale else None
    o_ref = refs[5 + n_w + int(has_scale)]
    stage = refs[-1 - 2 * n_w:-1 - n_w]
    wb = refs[-1 - n_w:-1]
    sem = refs[-1]
    j = pl.program_id(0)
    i = pl.program_id(1)

    def copy(e, jj, t):
        cols = pl.ds(pl.multiple_of(jj * bn, LANES), bn)
        return pltpu.make_async_copy(w_hbm[t].at[layer, e, :, cols], stage[t], sem.at[t])

    @pl.when(first_ref[i] == 1)
    def _():
        @pl.when((j == 0) & (i == 0))
        def _():
            for t in range(n_w):
                copy(te_ref[0], 0, t).start()

        for t in range(n_w):
            copy(te_ref[i], j, t).wait()
            wb[t][...] = stage[t][...].astype(BF16)

        nxt = nxt_ref[i]

        @pl.when(nxt >= 0)
        def _():
            for t in range(n_w):
                copy(nxt, j, t).start()

        @pl.when((nxt < 0) & (j + 1 < pl.num_programs(0)))
        def _():
            for t in range(n_w):
                copy(te_ref[0], j + 1, t).start()

    tm = a_ref.shape[0]
    half = tm // 2
    valid = valid_ref[i]

    def rows_out(rows):
        a = a_ref[rows, :]
        acc = _dot(a, wb[0][...])
        if n_w == 2:
            acc = _silu(acc) * _dot(a, wb[1][...])
        if has_scale:
            o_ref[rows, :] = _pack_bf16_pairs(rw_ref[rows, :] * acc)
        else:
            o_ref[rows, :] = acc.astype(o_ref.dtype)

    @pl.when(valid > half)
    def _():
        rows_out(slice(0, tm))

    @pl.when((valid > 0) & (valid <= half))
    def _():
        rows_out(slice(0, half))
        o_ref[half:, :] = jnp.zeros((tm - half, o_ref.shape[1]), o_ref.dtype)

    @pl.when(valid == 0)
    def _():
        o_ref[...] = jnp.zeros_like(o_ref)


def moe_matmul(a, ws, layer, tables, bn, row_scale=None):
    tile_e, first, nxt, valid = tables
    rows, k = a.shape
    n = ws[0].shape[-1]
    n_w = len(ws)
    tm = MOE_TILE
    bn = _tile(n, bn)
    packed = row_scale is not None
    out_bn, out_n, out_dtype = (bn // 2, n // 2, jnp.int32) if packed else (bn, n, BF16)
    idx = lambda j, i, *_: (i, 0)
    in_specs = [pl.BlockSpec((tm, k), idx)] + [pl.BlockSpec(memory_space=pl.ANY)] * n_w
    args = [a, *ws]
    if row_scale is not None:
        in_specs.append(pl.BlockSpec((tm, 1), idx))
        args.append(row_scale)
    scratch = ([pltpu.VMEM((k, bn), F32)] * n_w + [pltpu.VMEM((k, bn), BF16)] * n_w
               + [pltpu.SemaphoreType.DMA((n_w,))])
    return pl.pallas_call(
        functools.partial(_moe_mm_kernel, n_w=n_w, layer=layer, bn=bn, has_scale=row_scale is not None),
        grid_spec=pltpu.PrefetchScalarGridSpec(
            num_scalar_prefetch=4,
            grid=(n // bn, rows // tm),
            in_specs=in_specs,
            out_specs=pl.BlockSpec((tm, out_bn), lambda j, i, *_: (i, j)),
            scratch_shapes=scratch),
        out_shape=jax.ShapeDtypeStruct((rows, out_n), out_dtype),
        compiler_params=_cparams("arbitrary", "arbitrary"),
        name="moe_matmul",
    )(tile_e, first, nxt, valid, *args)


def _combine_kernel(pos_ref, x_ref, y_hbm, g_ref, o_ref, buf, sem, *, apply_norm, pack_tile):
    tb = x_ref.shape[0]
    i = pl.program_id(0)
    slot = i % 2

    def issue(tile, dst_slot):
        def start(r, carry):
            for s in range(TOP_K):
                _row_copy(y_hbm, pos_ref[(tile * tb + r) * TOP_K + s], buf.at[dst_slot, s], r,
                          sem.at[dst_slot]).start()
            return carry
        lax.fori_loop(0, tb, start, 0, unroll=4)

    @pl.when(i == 0)
    def _():
        issue(0, 0)

    @pl.when(i + 1 < pl.num_programs(0))
    def _():
        issue(i + 1, 1 - slot)

    def wait(r, carry):
        for s in range(TOP_K):
            _row_copy(y_hbm, 0, buf.at[slot, s], r, sem.at[slot]).wait()
        return carry

    lax.fori_loop(0, tb, wait, 0, unroll=4)
    x = x_ref[...]
    half_tile = pack_tile // 2
    for s in range(TOP_K):
        w = buf[slot, s]
        x = x + jnp.concatenate([_unpack_bf16_pairs(w[:, c:c + half_tile])
                                 for c in range(0, w.shape[1], half_tile)], axis=1)
    if apply_norm:
        x = x * lax.rsqrt(jnp.mean(x * x, axis=-1, keepdims=True) + EPS) * g_ref[...]
    o_ref[...] = x.astype(o_ref.dtype)


def moe_combine(x, y, pos, final_g, apply_norm, pack_tile):
    m, d = x.shape
    tb = GATHER_TILE
    return pl.pallas_call(
        functools.partial(_combine_kernel, apply_norm=apply_norm, pack_tile=pack_tile),
        grid_spec=pltpu.PrefetchScalarGridSpec(
            num_scalar_prefetch=1,
            grid=(m // tb,),
            in_specs=[pl.BlockSpec((tb, d), lambda i, p: (i, 0)),
                      pl.BlockSpec(memory_space=pl.ANY),
                      pl.BlockSpec((1, d), lambda i, p: (0, 0))],
            out_specs=pl.BlockSpec((tb, d), lambda i, p: (i, 0)),
            scratch_shapes=[pltpu.VMEM((2, TOP_K, tb, d // 2), y.dtype), pltpu.SemaphoreType.DMA((2,))]),
        out_shape=jax.ShapeDtypeStruct((m, d), x.dtype),
        compiler_params=_cparams("arbitrary"),
        name="moe_combine_norm",
    )(pos, x, y, final_g.reshape(1, d).astype(F32))


def _rope_tables(positions, dim, pad_to):
    inv = ROPE_THETA ** (-jnp.arange(0, dim, 2, dtype=F32) / dim)
    ang = positions.astype(F32).reshape(-1)[:, None] * inv
    cos, sin = jnp.cos(ang), jnp.sin(ang)
    pad = jnp.zeros((ang.shape[0], pad_to - dim), F32)
    return (jnp.concatenate([cos, cos, pad], axis=1), jnp.concatenate([-sin, sin, pad], axis=1))


def _swap_halves(w):
    half = w.shape[-1] // 2
    return jnp.concatenate([w[..., half:], w[..., :half]], axis=-1)


def kernel(x, positions, ln_mix, w_in, ret_norm, hgrn_norm, hgrn_lb_logits, mla_q_norm, mla_w_uq,
           mla_kv_norm, mla_w_ukv, w_out, ln_ffn, ffn_w1, ffn_w3, ffn_w2, moe_router_w, moe_w1,
           moe_w3, moe_w2, final_norm):
    batch, seq, d_model = x.shape
    depth = w_in.shape[0]
    q_rank, kv_rank = mla_q_norm.shape[1], mla_kv_norm.shape[1]
    ret_w = 2 * RET_HEADS * RET_QK_DIM + 2 * d_model
    hg_e = HGRN_HEADS * HGRN_EXPAND
    hg_w = 2 * hg_e + 2 * d_model
    mla_w = q_rank + kv_rank + MLA_ROPE_DIM
    mla_pad = -(-(q_rank + kv_rank + LANES) // (2 * LANES)) * (2 * LANES)
    w_in_t = jnp.swapaxes(w_in, 1, 2)

    cos_r, sin_r = _rope_tables(positions, RET_QK_DIM, RET_QK_DIM)
    cm, sm = _rope_tables(positions, MLA_ROPE_DIM, LANES)
    lb_all = jnp.cumsum(jax.nn.softmax(hgrn_lb_logits.astype(F32), axis=0), axis=0)
    lb_all = lb_all - lb_all[:1]

    xf = x.reshape(batch * seq, d_model)
    for l in range(depth):
        last = l == depth - 1
        o = ret_w + hg_w
        h = rmsnorm(xf, ln_mix[l], BF16)
        r = staged_matmul(h, [w_in_t], l, 0, ret_w, BF16, transposed=True, bm=IN_PROJ_BM)
        hf = staged_matmul(h, [w_in_t], l, ret_w, hg_e, F32, transposed=True, bm=IN_PROJ_BM)
        hr = staged_matmul(h, [w_in_t], l, ret_w + hg_e, hg_w - hg_e, BF16, transposed=True, bm=IN_PROJ_BM)
        pm = staged_matmul(h, [w_in_t], l, o, mla_pad, F32, transposed=True, bn=mla_pad // 2)
        gates = staged_matmul(h, [w_in_t], l, o + mla_w, N_BRANCHES * d_model, BF16, epilogue="sigmoid",
                              transposed=True, bm=IN_PROJ_BM)

        y = retention(r, gates, cos_r, sin_r, ret_norm[l], batch, seq)
        y = hgrn2(hf, hr, gates, y, lb_all[l], hgrn_norm[l], batch, seq)

        cqn, ckvn, kr = mla_prep(pm, mla_q_norm[l], mla_kv_norm[l], cm, sm)
        wq = mla_w_uq[l].reshape(q_rank, MLA_HEADS, MLA_NOPE_DIM + MLA_ROPE_DIM)
        wq_rope = wq[..., MLA_NOPE_DIM:]
        wq = jnp.concatenate([wq, _swap_halves(wq_rope)], axis=-1).reshape(q_rank, -1).astype(BF16)
        q_scale = (MLA_NOPE_DIM + MLA_ROPE_DIM) ** -0.5 * LOG2_E
        q = matmul(cqn, wq, BF16, bn=MLA_UP_BN, q_rope=(cm, sm, q_scale))
        kv = matmul(ckvn, mla_w_ukv[l].astype(BF16), BF16, bn=MLA_UP_BN)
        y = mla_attention(q, kv, kr, gates, y, batch, seq)
        xf = staged_matmul(y, [w_out], l, 0, d_model, F32, res=xf)

        if l % 2 == 0:
            j = l // 2
            h2 = rmsnorm(xf, ln_ffn[l], BF16)
            g = staged_matmul(h2, [ffn_w1, ffn_w3], j, 0, ffn_w1.shape[-1], BF16, epilogue="swiglu")
            xf = matmul_acc_res(g, ffn_w2[j].astype(BF16), xf, _tile(g.shape[1], FFN_DOWN_BK), bn=FFN_DOWN_BN)
            if last:
                xf = rmsnorm(xf, final_norm, x.dtype)
        else:
            j = l // 2
            h2, idx, wts = moe_router(xf, ln_ffn[l], moe_router_w[j])
            pos, row_tok, row_w, used_rows, tables = _routing_tables(idx, wts, MOE_TILE)
            xs = moe_gather(h2, row_tok, used_rows)
            g = moe_matmul(xs, [moe_w1, moe_w3], j, tables, MOE_UP_BN)
            down_bn = _tile(d_model, MOE_DOWN_BN)
            yrows = moe_matmul(g, [moe_w2], j, tables, down_bn, row_scale=row_w)
            xf = moe_combine(xf, yrows, pos, final_norm, apply_norm=last, pack_tile=down_bn)
    return xf.reshape(batch, seq, d_model)
```

```python
import functools

import jax
import jax.numpy as jnp
from jax import lax
from jax.experimental import pallas as pl
from jax.experimental.pallas import tpu as pltpu

F32 = jnp.float32
BF16 = jnp.bfloat16

RET_HEADS = 16
RET_QK_DIM = 128
HGRN_HEADS = 16
HGRN_EXPAND = 128
HGRN_CHUNK = 32
MLA_HEADS = 32
MLA_NOPE_DIM = 128
MLA_ROPE_DIM = 64
MLA_V_DIM = 128
N_BRANCHES = 3
N_EXPERTS = 8
TOP_K = 2
ROPE_THETA = 10000.0
EPS = 1e-6
LOG2_E = 1.4426950408889634
BF16_BITS = 16
HIGH_HALF_MASK = -65536

LANES = 128
SUBLANES = 8
VMEM_LIMIT_BYTES = 58 * 2 ** 20

ROW_TILE = 512
MM_BM = 1024
MM_BN = 512
IN_PROJ_BM = 2048
MLA_UP_BN = 2048
CAST_CHUNK = 512
RET_BLOCK = 512
HGRN_BLOCK = 256
MIXER_HEADS_PER_STEP = 4
ATTN_BLOCK = 512
ATTN_KV_BLOCK = 512
ATTN_HEADS_PER_STEP = 4
MOE_TILE = 512
MOE_TILE_PARTS = 4
MOE_UP_BN = 512
MOE_DOWN_BN = 1024
FFN_DOWN_BK = 3584
FFN_DOWN_BN = 1024
GATHER_TILE = 256


def _cparams(*sem):
    return pltpu.CompilerParams(dimension_semantics=sem, vmem_limit_bytes=VMEM_LIMIT_BYTES)


def _tile(n, preferred):
    for t in range(min(preferred, n) // LANES * LANES, 0, -LANES):
        if n % t == 0:
            return t
    return n


def _dot(a, b):
    return jnp.dot(a, b, preferred_element_type=F32)


def _dot_nt(a, b):
    return lax.dot_general(a, b, (((1,), (1,)), ((), ())), preferred_element_type=F32)


def _sigmoid(x):
    return 0.5 * jnp.tanh(0.5 * x) + 0.5


def _silu(x):
    return x * _sigmoid(x)


def _pack_bf16_pairs(x):
    c = x.shape[1] // 2
    lo = lax.bitcast_convert_type(x[:, :c].astype(BF16).astype(F32), jnp.int32)
    hi = lax.bitcast_convert_type(x[:, c:].astype(BF16).astype(F32), jnp.int32)
    return lax.shift_right_logical(lo, BF16_BITS) | (hi & HIGH_HALF_MASK)


def _unpack_bf16_pairs(w):
    lo = lax.bitcast_convert_type(lax.shift_left(w, BF16_BITS), F32)
    hi = lax.bitcast_convert_type(w & HIGH_HALF_MASK, F32)
    return jnp.concatenate([lo, hi], axis=1)


def _rmsnorm_kernel(x_ref, g_ref, o_ref):
    x = x_ref[...].astype(F32)
    ms = jnp.mean(x * x, axis=-1, keepdims=True)
    o_ref[...] = (x * lax.rsqrt(ms + EPS) * g_ref[...]).astype(o_ref.dtype)


def rmsnorm(x, g, out_dtype):
    m, d = x.shape
    rows = min(ROW_TILE, m)
    return pl.pallas_call(
        _rmsnorm_kernel,
        grid=(m // rows,),
        in_specs=[pl.BlockSpec((rows, d), lambda i: (i, 0)),
                  pl.BlockSpec((1, d), lambda i: (0, 0))],
        out_specs=pl.BlockSpec((rows, d), lambda i: (i, 0)),
        out_shape=jax.ShapeDtypeStruct((m, d), out_dtype),
        compiler_params=_cparams("parallel"),
        name="rmsnorm",
    )(x, g.reshape(1, d).astype(F32))


def _mm_kernel(*refs, q_scale):
    a_ref, w_ref = refs[0], refs[1]
    o_ref = refs[-1]
    acc = _dot(a_ref[...], w_ref[...])
    if q_scale is None:
        o_ref[...] = acc.astype(o_ref.dtype)
        return
    cm, sm = refs[2][...], refs[3][...]
    head_w = MLA_NOPE_DIM + 2 * MLA_ROPE_DIM
    for c in range(0, acc.shape[1], head_w):
        qr = acc[:, c + MLA_NOPE_DIM:c + head_w]
        qr = qr * cm + pltpu.roll(qr, MLA_ROPE_DIM, 1) * sm
        o_ref[:, c:c + MLA_NOPE_DIM] = (acc[:, c:c + MLA_NOPE_DIM] * q_scale).astype(o_ref.dtype)
        o_ref[:, c + MLA_NOPE_DIM:c + head_w] = (qr * q_scale).astype(o_ref.dtype)


def matmul(a, w, out_dtype, bm=MM_BM, bn=MM_BN, q_rope=None):
    m, k = a.shape
    n = w.shape[1]
    bm = _tile(m, bm)
    bn = _tile(n, bn)
    in_specs = [pl.BlockSpec((bm, k), lambda j, i: (i, 0)),
                pl.BlockSpec((k, bn), lambda j, i: (0, j))]
    args = [a, w]
    if q_rope is not None:
        in_specs += [pl.BlockSpec((bm, LANES), lambda j, i: (i, 0))] * 2
        args += [q_rope[0], q_rope[1]]
    return pl.pallas_call(
        functools.partial(_mm_kernel, q_scale=None if q_rope is None else q_rope[2]),
        grid=(n // bn, m // bm),
        in_specs=in_specs,
        out_specs=pl.BlockSpec((bm, bn), lambda j, i: (i, j)),
        out_shape=jax.ShapeDtypeStruct((m, n), out_dtype),
        compiler_params=_cparams("parallel", "parallel"),
        name="matmul",
    )(*args)


def _staged_mm_kernel(*refs, n_w, layer, col0, bn, epilogue, has_res, transposed):
    a_ref = refs[0]
    w_hbm = refs[1:1 + n_w]
    res_ref = refs[1 + n_w] if has_res else None
    o_ref = refs[1 + n_w + int(has_res)]
    stage = refs[-1 - 2 * n_w:-1 - n_w]
    wb = refs[-1 - n_w:-1]
    sem = refs[-1]
    j = pl.program_id(0)
    i = pl.program_id(1)

    def copy(jj, t):
        if transposed:
            rows = pl.ds(pl.multiple_of(col0 + jj * bn, SUBLANES), bn)
            return pltpu.make_async_copy(w_hbm[t].at[layer, rows, :], stage[t], sem.at[t])
        cols = pl.ds(pl.multiple_of(col0 + jj * bn, LANES), bn)
        return pltpu.make_async_copy(w_hbm[t].at[layer, :, cols], stage[t], sem.at[t])

    def cast(t):
        if not transposed:
            wb[t][...] = stage[t][...].astype(BF16)
            return
        k = wb[t].shape[0]
        step = _tile(k, CAST_CHUNK)
        for c in range(0, k, step):
            wb[t][c:c + step, :] = stage[t][:, c:c + step].T.astype(BF16)

    @pl.when(i == 0)
    def _():
        @pl.when(j == 0)
        def _():
            for t in range(n_w):
                copy(0, t).start()

        for t in range(n_w):
            copy(j, t).wait()
            cast(t)

        @pl.when(j + 1 < pl.num_programs(0))
        def _():
            for t in range(n_w):
                copy(j + 1, t).start()

    a = a_ref[...]
    acc = _dot(a, wb[0][...])
    if epilogue == "swiglu":
        acc = _silu(acc) * _dot(a, wb[1][...])
    elif epilogue == "sigmoid":
        acc = _sigmoid(acc)
    if has_res:
        acc = res_ref[...] + acc
    o_ref[...] = acc.astype(o_ref.dtype)


def staged_matmul(a, ws, layer, col0, n, out_dtype, epilogue=None, res=None, transposed=False,
                  bm=MM_BM, bn=MM_BN):
    m, k = a.shape
    n_w = len(ws)
    bm = _tile(m, bm)
    bn = _tile(n, bn)
    assert col0 % (SUBLANES if transposed else LANES) == 0 and n % bn == 0
    in_specs = [pl.BlockSpec((bm, k), lambda j, i: (i, 0))]
    in_specs += [pl.BlockSpec(memory_space=pl.ANY)] * n_w
    args = [a, *ws]
    if res is not None:
        in_specs.append(pl.BlockSpec((bm, bn), lambda j, i: (i, j)))
        args.append(res)
    stage_shape = (bn, k) if transposed else (k, bn)
    scratch = ([pltpu.VMEM(stage_shape, F32)] * n_w + [pltpu.VMEM((k, bn), BF16)] * n_w
               + [pltpu.SemaphoreType.DMA((n_w,))])
    return pl.pallas_call(
        functools.partial(_staged_mm_kernel, n_w=n_w, layer=layer, col0=col0, bn=bn,
                          epilogue=epilogue, has_res=res is not None, transposed=transposed),
        grid=(n // bn, m // bm),
        in_specs=in_specs,
        out_specs=pl.BlockSpec((bm, bn), lambda j, i: (i, j)),
        out_shape=jax.ShapeDtypeStruct((m, n), out_dtype),
        scratch_shapes=scratch,
        compiler_params=_cparams("arbitrary", "arbitrary"),
        name="staged_matmul",
    )(*args)


def _mm_acc_kernel(a_ref, w_ref, res_ref, o_ref, acc_ref):
    kk = pl.program_id(2)

    @pl.when(kk == 0)
    def _():
        acc_ref[...] = jnp.zeros_like(acc_ref)

    acc_ref[...] += _dot(a_ref[...], w_ref[...])

    @pl.when(kk == pl.num_programs(2) - 1)
    def _():
        o_ref[...] = (res_ref[...] + acc_ref[...]).astype(o_ref.dtype)


def matmul_acc_res(a, w, res, bk, bm=MM_BM, bn=MM_BN):
    m, k = a.shape
    n = w.shape[1]
    bm = _tile(m, bm)
    bn = _tile(n, bn)
    return pl.pallas_call(
        _mm_acc_kernel,
        grid=(n // bn, m // bm, k // bk),
        in_specs=[pl.BlockSpec((bm, bk), lambda j, i, kk: (i, kk)),
                  pl.BlockSpec((bk, bn), lambda j, i, kk: (kk, j)),
                  pl.BlockSpec((bm, bn), lambda j, i, kk: (i, j))],
        out_specs=pl.BlockSpec((bm, bn), lambda j, i, kk: (i, j)),
        out_shape=jax.ShapeDtypeStruct((m, n), res.dtype),
        scratch_shapes=[pltpu.VMEM((bm, bn), F32)],
        compiler_params=_cparams("parallel", "parallel", "arbitrary"),
        name="matmul_acc_res",
    )(a, w, res)


def _head_norm_gate(o, gain, g):
    y = o * lax.rsqrt(jnp.mean(o * o, axis=-1, keepdims=True) + EPS) * gain
    return y * _silu(g.astype(F32))


def _retention_kernel(q_ref, k_ref, v_ref, g_ref, mg_ref, cos_ref, sin_ref, dmat_ref, qdec_ref, kdec_ref,
                      cdec_ref, gain_ref, o_ref, st_ref):
    @pl.when(pl.program_id(2) == 0)
    def _():
        st_ref[...] = jnp.zeros_like(st_ref)

    cos = cos_ref[...]
    sin = sin_ref[...]
    dk = RET_QK_DIM
    half = dk // 2
    hp = st_ref.shape[0]
    dv = st_ref.shape[1]
    for hh in range(hp):
        qk_cols = slice(hh * dk, (hh + 1) * dk)
        v_cols = slice(hh * dv, (hh + 1) * dv)
        q = q_ref[:, qk_cols].astype(F32)
        k = k_ref[:, qk_cols].astype(F32)
        qr = q * cos + pltpu.roll(q, half, 1) * sin
        kr = (k * cos + pltpu.roll(k, half, 1) * sin) * (dk ** -0.5)
        v = v_ref[:, v_cols]
        scores = _dot_nt(qr.astype(BF16), kr.astype(BF16))
        st = st_ref[hh]
        cross = _dot_nt((qr * qdec_ref[hh]).astype(BF16), st.astype(BF16))
        v_t = v.astype(F32).T.astype(BF16)
        st_ref[hh] = cdec_ref[hh] * st + _dot(v_t, (kr * kdec_ref[hh]).astype(BF16))
        intra = _dot((scores * dmat_ref[hh]).astype(BF16), v)
        y = _head_norm_gate(intra + cross, gain_ref[:, v_cols], g_ref[:, v_cols])
        o_ref[:, v_cols] = (mg_ref[:, v_cols].astype(F32) * y).astype(o_ref.dtype)


def retention(r, gates, cos2, sin2, norm_g, batch, seq):
    heads, dk = RET_HEADS, RET_QK_DIM
    dv = (r.shape[1] - 2 * heads * dk) // (2 * heads)
    blk = min(RET_BLOCK, seq)
    nblk = seq // blk
    log_gamma = jnp.log(1.0 - jnp.exp2(-5.0 - jnp.arange(heads, dtype=F32)))
    idx = jnp.arange(blk, dtype=F32)
    rel = idx[:, None] - idx[None, :]
    dmat = jnp.where(rel >= 0, jnp.exp(log_gamma[:, None, None] * jnp.maximum(rel, 0.0)), 0.0)
    qdec = jnp.broadcast_to(jnp.exp(log_gamma[:, None] * (idx + 1.0))[:, :, None], (heads, blk, dk))
    kdec = jnp.broadcast_to(jnp.exp(log_gamma[:, None] * (blk - 1.0 - idx))[:, :, None], (heads, blk, dk))
    cdec = jnp.broadcast_to(jnp.exp(log_gamma * blk)[:, None, None], (heads, 1, dk))
    hp = MIXER_HEADS_PER_STEP
    groups = heads // hp
    assert (2 * heads * dk) % (hp * dv) == 0
    v_off = 2 * heads * dk // (hp * dv)
    g_off = v_off + groups
    row = lambda b, h, t: b * nblk + t
    return pl.pallas_call(
        _retention_kernel,
        grid=(batch, groups, nblk),
        in_specs=[pl.BlockSpec((blk, hp * dk), lambda b, h, t: (row(b, h, t), h)),
                  pl.BlockSpec((blk, hp * dk), lambda b, h, t: (row(b, h, t), groups + h)),
                  pl.BlockSpec((blk, hp * dv), lambda b, h, t: (row(b, h, t), v_off + h)),
                  pl.BlockSpec((blk, hp * dv), lambda b, h, t: (row(b, h, t), g_off + h)),
                  pl.BlockSpec((blk, hp * dv), lambda b, h, t: (row(b, h, t), h)),
                  pl.BlockSpec((blk, dk), lambda b, h, t: (row(b, h, t), 0)),
                  pl.BlockSpec((blk, dk), lambda b, h, t: (row(b, h, t), 0)),
                  pl.BlockSpec((hp, blk, blk), lambda b, h, t: (h, 0, 0)),
                  pl.BlockSpec((hp, blk, dk), lambda b, h, t: (h, 0, 0)),
                  pl.BlockSpec((hp, blk, dk), lambda b, h, t: (h, 0, 0)),
                  pl.BlockSpec((hp, 1, dk), lambda b, h, t: (h, 0, 0)),
                  pl.BlockSpec((1, hp * dv), lambda b, h, t: (0, h))],
        out_specs=pl.BlockSpec((blk, hp * dv), lambda b, h, t: (row(b, h, t), h)),
        out_shape=jax.ShapeDtypeStruct((batch * seq, heads * dv), BF16),
        scratch_shapes=[pltpu.VMEM((hp, dv, dk), F32)],
        compiler_params=_cparams("parallel", "parallel", "arbitrary"),
        name="retention",
    )(r, r, r, r, gates, cos2, sin2, dmat, qdec, kdec, cdec, norm_g.reshape(1, -1).astype(F32))


def _hgrn_kernel(f_ref, q_ref, i_ref, og_ref, mg_ref, y_ref, la_ref, l1_ref, oml_ref, gain_ref, o_ref,
                 st_ref):
    @pl.when(pl.program_id(2) == 0)
    def _():
        st_ref[...] = jnp.zeros_like(st_ref)

    blk = f_ref.shape[0]
    hp, dv, e = st_ref.shape
    c = HGRN_CHUNK
    nch = blk // c
    row = lax.broadcasted_iota(jnp.int32, (blk, e), 0)
    row_in_chunk = row % c
    chunk_of_row = row // c
    r2 = lax.broadcasted_iota(jnp.int32, (blk, blk), 0)
    c2 = lax.broadcasted_iota(jnp.int32, (blk, blk), 1)
    keep = (r2 // c == c2 // c) & (c2 <= r2)

    prep = []
    for hh in range(hp):
        e_cols = slice(hh * e, (hh + 1) * e)
        z = f_ref[:, e_cols]
        log_sig = jnp.minimum(z, 0.0) - jnp.log1p(jnp.exp(-jnp.abs(z)))
        la = la_ref[:, e_cols]
        u = l1_ref[:, e_cols] + log_sig
        log_f = jnp.maximum(la, u) + jnp.log1p(jnp.exp(-jnp.abs(la - u)))
        key = oml_ref[:, e_cols] * (1.0 / (1.0 + jnp.exp(z)))
        b = log_f
        shift = 1
        while shift < c:
            b = b + jnp.where(row_in_chunk >= shift, pltpu.roll(b, shift, 0), 0.0)
            shift *= 2
        b3 = b.reshape(nch, c, e)
        b_mid = jnp.broadcast_to(b3[:, c // 2 - 1:c // 2, :], (nch, c, e)).reshape(blk, e)
        b_end_rows = b3[:, c - 1:c, :]
        b_end = jnp.broadcast_to(b_end_rows, (nch, c, e)).reshape(blk, e)
        q = q_ref[:, e_cols].astype(F32)
        v = i_ref[:, hh * dv:(hh + 1) * dv]
        qf = (q * jnp.exp(b - b_mid)).astype(BF16)
        kf = (key * jnp.exp(b_mid - b)).astype(BF16)
        q_in = (q * jnp.exp(b)).astype(BF16)
        k_out = (key * jnp.exp(b_end - b)).astype(BF16)
        zero = jnp.zeros_like(q_in)
        k_wide = jnp.concatenate([jnp.where(chunk_of_row == j, k_out, zero) for j in range(nch)], axis=1)
        q_wide = jnp.concatenate([jnp.where(chunk_of_row == j, q_in, zero) for j in range(nch)], axis=1)
        v_t = v.astype(F32).T.astype(BF16)
        prep.append((qf, kf, k_wide, q_wide, v, v_t, jnp.exp(b_end_rows)))

    scores = [_dot_nt(p[0], p[1]) for p in prep]
    upds = [_dot(p[5], p[2]) for p in prep]
    for hh in range(hp):
        _, _, _, q_wide, v, _, dec = prep[hh]
        intra = _dot(jnp.where(keep, scores[hh], 0.0).astype(BF16), v)
        st = st_ref[hh]
        states = []
        for j in range(nch):
            states.append(st.astype(BF16))
            st = st * dec[j] + upds[hh][:, j * e:(j + 1) * e]
        st_ref[hh] = st
        inter = _dot_nt(q_wide, jnp.concatenate(states, axis=1))
        v_cols = slice(hh * dv, (hh + 1) * dv)
        y = _head_norm_gate(intra + inter, gain_ref[:, v_cols], og_ref[:, v_cols])
        o_ref[:, v_cols] = (y_ref[:, v_cols].astype(F32) + mg_ref[:, v_cols].astype(F32) * y).astype(o_ref.dtype)


def hgrn2(hf, hr, gates, y_prev, lb, norm_g, batch, seq):
    heads, e = HGRN_HEADS, HGRN_EXPAND
    dv = (hr.shape[1] - heads * e) // (2 * heads)
    blk = min(HGRN_BLOCK, seq)
    nblk = seq // blk
    lb = lb.astype(F32).reshape(1, -1)
    la, l1, oml = jnp.log(lb), jnp.log1p(-lb), 1.0 - lb
    hp = MIXER_HEADS_PER_STEP
    groups = heads // hp
    assert (heads * e) % (hp * dv) == 0
    i_off = heads * e // (hp * dv)
    og_off = i_off + groups
    row = lambda b, h, t: b * nblk + t
    vec = pl.BlockSpec((1, hp * e), lambda b, h, t: (0, h))
    return pl.pallas_call(
        _hgrn_kernel,
        grid=(batch, groups, nblk),
        in_specs=[pl.BlockSpec((blk, hp * e), lambda b, h, t: (row(b, h, t), h)),
                  pl.BlockSpec((blk, hp * e), lambda b, h, t: (row(b, h, t), h)),
                  pl.BlockSpec((blk, hp * dv), lambda b, h, t: (row(b, h, t), i_off + h)),
                  pl.BlockSpec((blk, hp * dv), lambda b, h, t: (row(b, h, t), og_off + h)),
                  pl.BlockSpec((blk, hp * dv), lambda b, h, t: (row(b, h, t), groups + h)),
                  pl.BlockSpec((blk, hp * dv), lambda b, h, t: (row(b, h, t), h)),
                  vec, vec, vec,
                  pl.BlockSpec((1, hp * dv), lambda b, h, t: (0, h))],
        out_specs=pl.BlockSpec((blk, hp * dv), lambda b, h, t: (row(b, h, t), h)),
        out_shape=jax.ShapeDtypeStruct((batch * seq, heads * dv), BF16),
        scratch_shapes=[pltpu.VMEM((hp, dv, e), F32)],
        compiler_params=_cparams("parallel", "parallel", "arbitrary"),
        name="hgrn2",
    )(hf, hr, hr, hr, gates, y_prev, la, l1, oml, norm_g.reshape(1, -1).astype(F32))


def _mla_prep_kernel(p_ref, qg_ref, kvg_ref, cm_ref, sm_ref, cq_ref, ckv_ref, kr_ref, *, q_rank, kv_rank):
    p = p_ref[...]

    def norm(x, g):
        return x * lax.rsqrt(jnp.mean(x * x, axis=-1, keepdims=True) + EPS) * g

    cq_ref[...] = norm(p[:, :q_rank], qg_ref[...]).astype(cq_ref.dtype)
    ckv_ref[...] = norm(p[:, q_rank:q_rank + kv_rank], kvg_ref[...]).astype(ckv_ref.dtype)
    kr = p[:, q_rank + kv_rank:q_rank + kv_rank + LANES]
    lane = lax.broadcasted_iota(jnp.int32, kr.shape, 1)
    half = MLA_ROPE_DIM // 2
    swapped = jnp.where(lane < MLA_ROPE_DIM + half, pltpu.roll(kr, half, 1),
                        pltpu.roll(kr, MLA_ROPE_DIM + half, 1))
    pair = jnp.where(lane < MLA_ROPE_DIM, kr, swapped)
    kr_ref[...] = (pair * cm_ref[...] + pltpu.roll(pair, MLA_ROPE_DIM, 1) * sm_ref[...]).astype(kr_ref.dtype)


def mla_prep(pm, q_norm, kv_norm, cm, sm):
    m = pm.shape[0]
    q_rank, kv_rank = q_norm.shape[0], kv_norm.shape[0]
    rows = min(ROW_TILE, m)
    full = lambda w: pl.BlockSpec((rows, w), lambda i: (i, 0))
    return pl.pallas_call(
        functools.partial(_mla_prep_kernel, q_rank=q_rank, kv_rank=kv_rank),
        grid=(m // rows,),
        in_specs=[full(pm.shape[1]),
                  pl.BlockSpec((1, q_rank), lambda i: (0, 0)),
                  pl.BlockSpec((1, kv_rank), lambda i: (0, 0)),
                  full(LANES), full(LANES)],
        out_specs=[full(q_rank), full(kv_rank), full(LANES)],
        out_shape=[jax.ShapeDtypeStruct((m, q_rank), BF16),
                   jax.ShapeDtypeStruct((m, kv_rank), BF16),
                   jax.ShapeDtypeStruct((m, LANES), BF16)],
        compiler_params=_cparams("parallel"),
        name="mla_prep",
    )(pm, q_norm.reshape(1, -1).astype(F32), kv_norm.reshape(1, -1).astype(F32), cm, sm)


def _attn_kernel(q_ref, kv_ref, kr_ref, mg_ref, y_ref, o_ref, kcat_ref, vt_ref, *, heads_per_step, tk):
    i = pl.program_id(2)
    tq = q_ref.shape[0]
    kv_w = MLA_NOPE_DIM + MLA_V_DIM
    q_w = MLA_NOPE_DIM + 2 * MLA_ROPE_DIM

    @pl.when(i == 0)
    def _():
        for hh in range(heads_per_step):
            kcat_ref[hh, :, :MLA_NOPE_DIM] = kv_ref[:, hh * kv_w:hh * kv_w + MLA_NOPE_DIM]
            kcat_ref[hh, :, MLA_NOPE_DIM:] = kr_ref[...]
            for jj in range(vt_ref.shape[1]):
                v = kv_ref[jj * tk:(jj + 1) * tk, hh * kv_w + MLA_NOPE_DIM:(hh + 1) * kv_w]
                vt_ref[hh, jj] = v.astype(F32).T.astype(BF16)

    qfs = [q_ref[:, hh * q_w:(hh + 1) * q_w] for hh in range(heads_per_step)]

    def step(j, carry, masked):
        rows = pl.ds(pl.multiple_of(j * tk, tk), tk)
        out = []
        scores = [_dot_nt(kcat_ref[hh, rows, :], qfs[hh]) for hh in range(heads_per_step)]
        for hh in range(heads_per_step):
            m, l, acc = carry[hh]
            s = scores[hh]
            if masked:
                kpos = lax.broadcasted_iota(jnp.int32, s.shape, 0) + j * tk
                qpos = lax.broadcasted_iota(jnp.int32, s.shape, 1) + i * tq
                s = jnp.where(kpos <= qpos, s, -jnp.inf)
            m_new = jnp.maximum(m, jnp.max(s, axis=0, keepdims=True))
            p = jnp.exp2(s - m_new)
            alpha = jnp.exp2(m - m_new)
            l = alpha * l + jnp.sum(p, axis=0, keepdims=True)
            acc = alpha * acc + _dot(vt_ref[hh, j], p.astype(BF16))
            out.append((m_new, l, acc))
        return tuple(out)

    init = tuple((jnp.full((1, tq), -jnp.inf, F32), jnp.zeros((1, tq), F32), jnp.zeros((MLA_V_DIM, tq), F32))
                 for _ in range(heads_per_step))
    per_q = tq // tk
    carry = lax.fori_loop(0, i * per_q, lambda j, cr: step(j, cr, False), init)
    for d in range(per_q):
        carry = step(i * per_q + d, carry, True)
    for hh in range(heads_per_step):
        _, l, acc = carry[hh]
        cols = slice(hh * MLA_V_DIM, (hh + 1) * MLA_V_DIM)
        y = y_ref[:, cols].astype(F32) + mg_ref[:, cols].astype(F32) * (acc / l).T
        o_ref[:, cols] = y.astype(o_ref.dtype)


def mla_attention(q, kv, kr, gates, y_prev, batch, seq):
    heads = MLA_HEADS
    hp = ATTN_HEADS_PER_STEP
    tq = min(ATTN_BLOCK, seq)
    tk = min(ATTN_KV_BLOCK, tq)
    nq = seq // tq
    qw = MLA_NOPE_DIM + 2 * MLA_ROPE_DIM
    kvw = MLA_NOPE_DIM + MLA_V_DIM
    return pl.pallas_call(
        functools.partial(_attn_kernel, heads_per_step=hp, tk=tk),
        grid=(batch, heads // hp, nq),
        in_specs=[pl.BlockSpec((tq, hp * qw), lambda b, h, i: (b * nq + i, h)),
                  pl.BlockSpec((seq, hp * kvw), lambda b, h, i: (b, h)),
                  pl.BlockSpec((seq, LANES), lambda b, h, i: (b, 0)),
                  pl.BlockSpec((tq, hp * MLA_V_DIM), lambda b, h, i: (b * nq + i, 2 * (heads // hp) + h)),
                  pl.BlockSpec((tq, hp * MLA_V_DIM), lambda b, h, i: (b * nq + i, h))],
        out_specs=pl.BlockSpec((tq, hp * MLA_V_DIM), lambda b, h, i: (b * nq + i, h)),
        out_shape=jax.ShapeDtypeStruct((batch * seq, heads * MLA_V_DIM), BF16),
        scratch_shapes=[pltpu.VMEM((hp, seq, MLA_NOPE_DIM + LANES), BF16),
                        pltpu.VMEM((hp, seq // tk, MLA_V_DIM, tk), BF16)],
        compiler_params=_cparams("parallel", "parallel", "arbitrary"),
        name="mla_attention",
    )(q, kv, kr, gates, y_prev)


def _router_kernel(x_ref, g_ref, r_ref, h_ref, idx_ref, w_ref):
    x = x_ref[...]
    h = x * lax.rsqrt(jnp.mean(x * x, axis=-1, keepdims=True) + EPS) * g_ref[...]
    h_ref[...] = _pack_bf16_pairs(h)
    logits = jnp.dot(h, r_ref[...], preferred_element_type=F32, precision=lax.Precision.HIGHEST)
    lane = lax.broadcasted_iota(jnp.int32, logits.shape, 1)
    lg = jnp.where(lane < N_EXPERTS, logits, -jnp.inf)
    m1 = jnp.max(lg, axis=-1, keepdims=True)
    i1 = jnp.min(jnp.where(lg == m1, lane, LANES), axis=-1, keepdims=True)
    lg2 = jnp.where(lane == i1, -jnp.inf, lg)
    m2 = jnp.max(lg2, axis=-1, keepdims=True)
    i2 = jnp.min(jnp.where(lg2 == m2, lane, LANES), axis=-1, keepdims=True)
    e2 = jnp.exp(m2 - m1)
    w1 = 1.0 / (1.0 + e2)
    w2 = e2 / (1.0 + e2)
    idx_ref[...] = jnp.where(lane == 0, i1, i2)
    w_ref[...] = jnp.where(lane == 0, w1, w2)


def moe_router(x, g, router):
    m, d = x.shape
    rows = min(ROW_TILE, m)
    r_pad = jnp.zeros((d, LANES), F32).at[:, :N_EXPERTS].set(router.astype(F32))
    h, idx, w = pl.pallas_call(
        _router_kernel,
        grid=(m // rows,),
        in_specs=[pl.BlockSpec((rows, d), lambda i: (i, 0)),
                  pl.BlockSpec((1, d), lambda i: (0, 0)),
                  pl.BlockSpec((d, LANES), lambda i: (0, 0))],
        out_specs=[pl.BlockSpec((rows, d // 2), lambda i: (i, 0)),
                   pl.BlockSpec((rows, LANES), lambda i: (i, 0)),
                   pl.BlockSpec((rows, LANES), lambda i: (i, 0))],
        out_shape=[jax.ShapeDtypeStruct((m, d // 2), jnp.int32),
                   jax.ShapeDtypeStruct((m, LANES), jnp.int32),
                   jax.ShapeDtypeStruct((m, LANES), F32)],
        compiler_params=_cparams("parallel"),
        name="moe_router",
    )(x, g.reshape(1, d).astype(F32), r_pad)
    return h, idx[:, :TOP_K], w[:, :TOP_K]


def _routing_tables(idx, wts, tile):
    t = idx.shape[0]
    pairs = t * TOP_K
    e = idx.reshape(pairs)
    onehot = (e[:, None] == jnp.arange(N_EXPERTS, dtype=jnp.int32)[None, :]).astype(jnp.int32)
    csum = jnp.cumsum(onehot, axis=0)
    rank = jnp.sum(csum * onehot, axis=1) - 1
    counts = csum[-1]
    padded = ((counts + tile - 1) // tile) * tile
    ends = jnp.cumsum(padded)
    starts = ends - padded
    pos = (jnp.sum(onehot * starts[None, :], axis=1) + rank).astype(jnp.int32)
    rows = pairs + N_EXPERTS * tile
    row_tok = jnp.zeros((rows,), jnp.int32).at[pos].set(jnp.arange(pairs, dtype=jnp.int32) // TOP_K)
    row_w = jnp.zeros((rows,), F32).at[pos].set(wts.reshape(pairs))
    tile_start = jnp.arange(rows // tile, dtype=jnp.int32) * tile
    tile_e = jnp.minimum(jnp.sum((tile_start[:, None] >= ends[None, :]).astype(jnp.int32), axis=1),
                         N_EXPERTS - 1).astype(jnp.int32)
    n_used = (ends[-1] // tile).astype(jnp.int32).reshape(1)
    n_tiles = rows // tile
    t_idx = jnp.arange(n_tiles, dtype=jnp.int32)
    prev_e = jnp.concatenate([jnp.full((1,), -1, jnp.int32), tile_e[:-1]])
    first = ((tile_e != prev_e) & (t_idx < n_used[0])).astype(jnp.int32)
    later_first = (t_idx[None, :] > t_idx[:, None]) & (first[None, :] == 1)
    next_idx = jnp.min(jnp.where(later_first, t_idx[None, :], n_tiles), axis=1)
    nxt = jnp.where(next_idx < n_tiles, tile_e[jnp.minimum(next_idx, n_tiles - 1)], -1).astype(jnp.int32)
    valid = jnp.clip((starts + counts)[tile_e] - tile_start, 0, tile).astype(jnp.int32)
    used_rows = (n_used * tile).astype(jnp.int32)
    return pos, row_tok, row_w.reshape(rows, 1), used_rows, (tile_e, first, nxt, valid)


def _row_copy(src_hbm, src_row, dst, dst_row, sem):
    return pltpu.make_async_copy(src_hbm.at[pl.ds(src_row, 1), :], dst.at[pl.ds(dst_row, 1), :], sem)


def _gather_kernel(tok_ref, used_ref, h_hbm, o_ref, buf, sem):
    tg = buf.shape[1]
    i = pl.program_id(0)
    slot = i % 2
    used_rows = used_ref[0]

    def issue(tile, dst_slot):
        def start(r, carry):
            _row_copy(h_hbm, tok_ref[tile * tg + r], buf.at[dst_slot], r, sem.at[dst_slot]).start()
            return carry
        lax.fori_loop(0, tg, start, 0, unroll=8)

    @pl.when((i == 0) & (used_rows > 0))
    def _():
        issue(0, 0)

    @pl.when((i + 1 < pl.num_programs(0)) & ((i + 1) * tg < used_rows))
    def _():
        issue(i + 1, 1 - slot)

    @pl.when(i * tg < used_rows)
    def _():
        def wait(r, carry):
            _row_copy(h_hbm, 0, buf.at[slot], r, sem.at[slot]).wait()
            return carry

        lax.fori_loop(0, tg, wait, 0, unroll=8)
        o_ref[...] = _unpack_bf16_pairs(buf[slot]).astype(o_ref.dtype)

    @pl.when(i * tg >= used_rows)
    def _():
        o_ref[...] = jnp.zeros_like(o_ref)


def moe_gather(h, row_tok, used_rows):
    rows = row_tok.shape[0]
    d = 2 * h.shape[1]
    tg = GATHER_TILE
    return pl.pallas_call(
        _gather_kernel,
        grid_spec=pltpu.PrefetchScalarGridSpec(
            num_scalar_prefetch=2,
            grid=(rows // tg,),
            in_specs=[pl.BlockSpec(memory_space=pl.ANY)],
            out_specs=pl.BlockSpec((tg, d), lambda i, tok, used: (i, 0)),
            scratch_shapes=[pltpu.VMEM((2, tg, d // 2), h.dtype), pltpu.SemaphoreType.DMA((2,))]),
        out_shape=jax.ShapeDtypeStruct((rows, d), BF16),
        compiler_params=_cparams("arbitrary"),
        name="moe_gather",
    )(row_tok, used_rows, h)


def _moe_mm_kernel(*refs, n_w, layer, bn, has_scale):
    te_ref, first_ref, nxt_ref, valid_ref, a_ref = refs[:5]
    w_hbm = refs[5:5 + n_w]
    rw_ref = refs[5 + n_w] if has_scale else None
    o_ref = refs[5 + n_w + int(has_scale)]
    stage = refs[-1 - 2 * n_w:-1 - n_w]
    wb = refs[-1 - n_w:-1]
    sem = refs[-1]
    j = pl.program_id(0)
    i = pl.program_id(1)

    def copy(e, jj, t):
        cols = pl.ds(pl.multiple_of(jj * bn, LANES), bn)
        return pltpu.make_async_copy(w_hbm[t].at[layer, e, :, cols], stage[t], sem.at[t])

    @pl.when(first_ref[i] == 1)
    def _():
        @pl.when((j == 0) & (i == 0))
        def _():
            for t in range(n_w):
                copy(te_ref[0], 0, t).start()

        for t in range(n_w):
            copy(te_ref[i], j, t).wait()
            wb[t][...] = stage[t][...].astype(BF16)

        nxt = nxt_ref[i]

        @pl.when(nxt >= 0)
        def _():
            for t in range(n_w):
                copy(nxt, j, t).start()

        @pl.when((nxt < 0) & (j + 1 < pl.num_programs(0)))
        def _():
            for t in range(n_w):
                copy(te_ref[0], j + 1, t).start()

    tm = a_ref.shape[0]
    part = tm // MOE_TILE_PARTS
    valid = valid_ref[i]

    def rows_out(rows):
        a = a_ref[rows, :]
        acc = _dot(a, wb[0][...])
        if n_w == 2:
            acc = _silu(acc) * _dot(a, wb[1][...])
        if has_scale:
            o_ref[rows, :] = _pack_bf16_pairs(rw_ref[rows, :] * acc)
        else:
            o_ref[rows, :] = acc.astype(o_ref.dtype)

    for n_parts in range(1, MOE_TILE_PARTS + 1):
        rows_used = n_parts * part

        @pl.when((valid > rows_used - part) & (valid <= rows_used))
        def _(rows_used=rows_used):
            rows_out(slice(0, rows_used))
            if rows_used < tm:
                o_ref[rows_used:, :] = jnp.zeros((tm - rows_used, o_ref.shape[1]), o_ref.dtype)

    @pl.when(valid == 0)
    def _():
        o_ref[...] = jnp.zeros_like(o_ref)


def moe_matmul(a, ws, layer, tables, bn, row_scale=None):
    tile_e, first, nxt, valid = tables
    rows, k = a.shape
    n = ws[0].shape[-1]
    n_w = len(ws)
    tm = MOE_TILE
    bn = _tile(n, bn)
    packed = row_scale is not None
    out_bn, out_n, out_dtype = (bn // 2, n // 2, jnp.int32) if packed else (bn, n, BF16)
    idx = lambda j, i, *_: (i, 0)
    in_specs = [pl.BlockSpec((tm, k), idx)] + [pl.BlockSpec(memory_space=pl.ANY)] * n_w
    args = [a, *ws]
    if row_scale is not None:
        in_specs.append(pl.BlockSpec((tm, 1), idx))
        args.append(row_scale)
    scratch = ([pltpu.VMEM((k, bn), F32)] * n_w + [pltpu.VMEM((k, bn), BF16)] * n_w
               + [pltpu.SemaphoreType.DMA((n_w,))])
    return pl.pallas_call(
        functools.partial(_moe_mm_kernel, n_w=n_w, layer=layer, bn=bn, has_scale=row_scale is not None),
        grid_spec=pltpu.PrefetchScalarGridSpec(
            num_scalar_prefetch=4,
            grid=(n // bn, rows // tm),
            in_specs=in_specs,
            out_specs=pl.BlockSpec((tm, out_bn), lambda j, i, *_: (i, j)),
            scratch_shapes=scratch),
        out_shape=jax.ShapeDtypeStruct((rows, out_n), out_dtype),
        compiler_params=_cparams("arbitrary", "arbitrary"),
        name="moe_matmul",
    )(tile_e, first, nxt, valid, *args)


def _combine_kernel(pos_ref, x_ref, y_hbm, g_ref, o_ref, buf, sem, *, apply_norm, pack_tile):
    tb = x_ref.shape[0]
    i = pl.program_id(0)
    slot = i % 2

    def issue(tile, dst_slot):
        def start(r, carry):
            for s in range(TOP_K):
                _row_copy(y_hbm, pos_ref[(tile * tb + r) * TOP_K + s], buf.at[dst_slot, s], r,
                          sem.at[dst_slot]).start()
            return carry
        lax.fori_loop(0, tb, start, 0, unroll=4)

    @pl.when(i == 0)
    def _():
        issue(0, 0)

    @pl.when(i + 1 < pl.num_programs(0))
    def _():
        issue(i + 1, 1 - slot)

    def wait(r, carry):
        for s in range(TOP_K):
            _row_copy(y_hbm, 0, buf.at[slot, s], r, sem.at[slot]).wait()
        return carry

    lax.fori_loop(0, tb, wait, 0, unroll=4)
    x = x_ref[...]
    half_tile = pack_tile // 2
    for s in range(TOP_K):
        w = buf[slot, s]
        x = x + jnp.concatenate([_unpack_bf16_pairs(w[:, c:c + half_tile])
                                 for c in range(0, w.shape[1], half_tile)], axis=1)
    if apply_norm:
        x = x * lax.rsqrt(jnp.mean(x * x, axis=-1, keepdims=True) + EPS) * g_ref[...]
    o_ref[...] = x.astype(o_ref.dtype)


def moe_combine(x, y, pos, final_g, apply_norm, pack_tile):
    m, d = x.shape
    tb = GATHER_TILE
    return pl.pallas_call(
        functools.partial(_combine_kernel, apply_norm=apply_norm, pack_tile=pack_tile),
        grid_spec=pltpu.PrefetchScalarGridSpec(
            num_scalar_prefetch=1,
            grid=(m // tb,),
            in_specs=[pl.BlockSpec((tb, d), lambda i, p: (i, 0)),
                      pl.BlockSpec(memory_space=pl.ANY),
                      pl.BlockSpec((1, d), lambda i, p: (0, 0))],
            out_specs=pl.BlockSpec((tb, d), lambda i, p: (i, 0)),
            scratch_shapes=[pltpu.VMEM((2, TOP_K, tb, d // 2), y.dtype), pltpu.SemaphoreType.DMA((2,))]),
        out_shape=jax.ShapeDtypeStruct((m, d), x.dtype),
        compiler_params=_cparams("arbitrary"),
        name="moe_combine_norm",
    )(pos, x, y, final_g.reshape(1, d).astype(F32))


def _rope_tables(positions, dim, pad_to):
    inv = ROPE_THETA ** (-jnp.arange(0, dim, 2, dtype=F32) / dim)
    ang = positions.astype(F32).reshape(-1)[:, None] * inv
    cos, sin = jnp.cos(ang), jnp.sin(ang)
    pad = jnp.zeros((ang.shape[0], pad_to - dim), F32)
    return (jnp.concatenate([cos, cos, pad], axis=1), jnp.concatenate([-sin, sin, pad], axis=1))


def _swap_halves(w):
    half = w.shape[-1] // 2
    return jnp.concatenate([w[..., half:], w[..., :half]], axis=-1)


def kernel(x, positions, ln_mix, w_in, ret_norm, hgrn_norm, hgrn_lb_logits, mla_q_norm, mla_w_uq,
           mla_kv_norm, mla_w_ukv, w_out, ln_ffn, ffn_w1, ffn_w3, ffn_w2, moe_router_w, moe_w1,
           moe_w3, moe_w2, final_norm):
    batch, seq, d_model = x.shape
    depth = w_in.shape[0]
    q_rank, kv_rank = mla_q_norm.shape[1], mla_kv_norm.shape[1]
    ret_w = 2 * RET_HEADS * RET_QK_DIM + 2 * d_model
    hg_e = HGRN_HEADS * HGRN_EXPAND
    hg_w = 2 * hg_e + 2 * d_model
    mla_w = q_rank + kv_rank + MLA_ROPE_DIM
    mla_pad = -(-(q_rank + kv_rank + LANES) // (2 * LANES)) * (2 * LANES)
    w_in_t = jnp.swapaxes(w_in, 1, 2)

    cos_r, sin_r = _rope_tables(positions, RET_QK_DIM, RET_QK_DIM)
    cm, sm = _rope_tables(positions, MLA_ROPE_DIM, LANES)
    lb_all = jnp.cumsum(jax.nn.softmax(hgrn_lb_logits.astype(F32), axis=0), axis=0)
    lb_all = lb_all - lb_all[:1]

    xf = x.reshape(batch * seq, d_model)
    for l in range(depth):
        last = l == depth - 1
        o = ret_w + hg_w
        h = rmsnorm(xf, ln_mix[l], BF16)
        r = staged_matmul(h, [w_in_t], l, 0, ret_w, BF16, transposed=True, bm=IN_PROJ_BM)
        hf = staged_matmul(h, [w_in_t], l, ret_w, hg_e, F32, transposed=True, bm=IN_PROJ_BM)
        hr = staged_matmul(h, [w_in_t], l, ret_w + hg_e, hg_w - hg_e, BF16, transposed=True, bm=IN_PROJ_BM)
        pm = staged_matmul(h, [w_in_t], l, o, mla_pad, F32, transposed=True, bn=mla_pad // 2)
        gates = staged_matmul(h, [w_in_t], l, o + mla_w, N_BRANCHES * d_model, BF16, epilogue="sigmoid",
                              transposed=True, bm=IN_PROJ_BM)

        y = retention(r, gates, cos_r, sin_r, ret_norm[l], batch, seq)
        y = hgrn2(hf, hr, gates, y, lb_all[l], hgrn_norm[l], batch, seq)

        cqn, ckvn, kr = mla_prep(pm, mla_q_norm[l], mla_kv_norm[l], cm, sm)
        wq = mla_w_uq[l].reshape(q_rank, MLA_HEADS, MLA_NOPE_DIM + MLA_ROPE_DIM)
        wq_rope = wq[..., MLA_NOPE_DIM:]
        wq = jnp.concatenate([wq, _swap_halves(wq_rope)], axis=-1).reshape(q_rank, -1).astype(BF16)
        q_scale = (MLA_NOPE_DIM + MLA_ROPE_DIM) ** -0.5 * LOG2_E
        q = matmul(cqn, wq, BF16, bn=MLA_UP_BN, q_rope=(cm, sm, q_scale))
        kv = matmul(ckvn, mla_w_ukv[l].astype(BF16), BF16, bn=MLA_UP_BN)
        y = mla_attention(q, kv, kr, gates, y, batch, seq)
        xf = staged_matmul(y, [w_out], l, 0, d_model, F32, res=xf)

        if l % 2 == 0:
            j = l // 2
            h2 = rmsnorm(xf, ln_ffn[l], BF16)
            g = staged_matmul(h2, [ffn_w1, ffn_w3], j, 0, ffn_w1.shape[-1], BF16, epilogue="swiglu")
            xf = matmul_acc_res(g, ffn_w2[j].astype(BF16), xf, _tile(g.shape[1], FFN_DOWN_BK), bn=FFN_DOWN_BN)
            if last:
                xf = rmsnorm(xf, final_norm, x.dtype)
        else:
            j = l // 2
            h2, idx, wts = moe_router(xf, ln_ffn[l], moe_router_w[j])
            pos, row_tok, row_w, used_rows, tables = _routing_tables(idx, wts, MOE_TILE)
            xs = moe_gather(h2, row_tok, used_rows)
            g = moe_matmul(xs, [moe_w1, moe_w3], j, tables, MOE_UP_BN)
            down_bn = _tile(d_model, MOE_DOWN_BN)
            yrows = moe_matmul(g, [moe_w2], j, tables, down_bn, row_scale=row_w)
            xf = moe_combine(xf, yrows, pos, final_norm, apply_norm=last, pack_tile=down_bn)
    return xf.reshape(batch, seq, d_model)
```

```python
import functools

import jax
import jax.numpy as jnp
from jax import lax
from jax.experimental import pallas as pl
from jax.experimental.pallas import tpu as pltpu

F32 = jnp.float32
BF16 = jnp.bfloat16

RET_HEADS = 16
RET_QK_DIM = 128
HGRN_HEADS = 16
HGRN_EXPAND = 128
HGRN_CHUNK = 32
MLA_HEADS = 32
MLA_NOPE_DIM = 128
MLA_ROPE_DIM = 64
MLA_V_DIM = 128
N_BRANCHES = 3
N_EXPERTS = 8
TOP_K = 2
ROPE_THETA = 10000.0
EPS = 1e-6
LOG2_E = 1.4426950408889634
BF16_BITS = 16
HIGH_HALF_MASK = -65536

LANES = 128
SUBLANES = 8
VMEM_LIMIT_BYTES = 58 * 2 ** 20

ROW_TILE = 512
MM_BM = 1024
MM_BN = 512
IN_PROJ_BM = 2048
MLA_UP_BN = 2048
CAST_CHUNK = 512
RET_BLOCK = 512
HGRN_BLOCK = 256
MIXER_HEADS_PER_STEP = 4
ATTN_BLOCK = 512
ATTN_KV_BLOCK = 512
ATTN_HEADS_PER_STEP = 4
MOE_TILE = 512
MOE_UP_BN = 512
MOE_DOWN_BN = 1024
FFN_DOWN_BK = 3584
FFN_DOWN_BN = 1024
GATHER_TILE = 256


def _cparams(*sem):
    return pltpu.CompilerParams(dimension_semantics=sem, vmem_limit_bytes=VMEM_LIMIT_BYTES)


def _tile(n, preferred):
    for t in range(min(preferred, n) // LANES * LANES, 0, -LANES):
        if n % t == 0:
            return t
    return n


def _dot(a, b):
    return jnp.dot(a, b, preferred_element_type=F32)


def _dot_nt(a, b):
    return lax.dot_general(a, b, (((1,), (1,)), ((), ())), preferred_element_type=F32)


def _sigmoid(x):
    return 0.5 * jnp.tanh(0.5 * x) + 0.5


def _silu(x):
    return x * _sigmoid(x)


def _pack_bf16_pairs(x):
    c = x.shape[1] // 2
    lo = lax.bitcast_convert_type(x[:, :c].astype(BF16).astype(F32), jnp.int32)
    hi = lax.bitcast_convert_type(x[:, c:].astype(BF16).astype(F32), jnp.int32)
    return lax.shift_right_logical(lo, BF16_BITS) | (hi & HIGH_HALF_MASK)


def _unpack_bf16_pairs(w):
    lo = lax.bitcast_convert_type(lax.shift_left(w, BF16_BITS), F32)
    hi = lax.bitcast_convert_type(w & HIGH_HALF_MASK, F32)
    return jnp.concatenate([lo, hi], axis=1)


def _rmsnorm_kernel(x_ref, g_ref, o_ref):
    x = x_ref[...].astype(F32)
    ms = jnp.mean(x * x, axis=-1, keepdims=True)
    o_ref[...] = (x * lax.rsqrt(ms + EPS) * g_ref[...]).astype(o_ref.dtype)


def rmsnorm(x, g, out_dtype):
    m, d = x.shape
    rows = min(ROW_TILE, m)
    return pl.pallas_call(
        _rmsnorm_kernel,
        grid=(m // rows,),
        in_specs=[pl.BlockSpec((rows, d), lambda i: (i, 0)),
                  pl.BlockSpec((1, d), lambda i: (0, 0))],
        out_specs=pl.BlockSpec((rows, d), lambda i: (i, 0)),
        out_shape=jax.ShapeDtypeStruct((m, d), out_dtype),
        compiler_params=_cparams("parallel"),
        name="rmsnorm",
    )(x, g.reshape(1, d).astype(F32))


def _mm_kernel(*refs, q_scale):
    a_ref, w_ref = refs[0], refs[1]
    o_ref = refs[-1]
    acc = _dot(a_ref[...], w_ref[...])
    if q_scale is None:
        o_ref[...] = acc.astype(o_ref.dtype)
        return
    cm, sm = refs[2][...], refs[3][...]
    head_w = MLA_NOPE_DIM + 2 * MLA_ROPE_DIM
    for c in range(0, acc.shape[1], head_w):
        qr = acc[:, c + MLA_NOPE_DIM:c + head_w]
        qr = qr * cm + pltpu.roll(qr, MLA_ROPE_DIM, 1) * sm
        o_ref[:, c:c + MLA_NOPE_DIM] = (acc[:, c:c + MLA_NOPE_DIM] * q_scale).astype(o_ref.dtype)
        o_ref[:, c + MLA_NOPE_DIM:c + head_w] = (qr * q_scale).astype(o_ref.dtype)


def matmul(a, w, out_dtype, bm=MM_BM, bn=MM_BN, q_rope=None):
    m, k = a.shape
    n = w.shape[1]
    bm = _tile(m, bm)
    bn = _tile(n, bn)
    in_specs = [pl.BlockSpec((bm, k), lambda j, i: (i, 0)),
                pl.BlockSpec((k, bn), lambda j, i: (0, j))]
    args = [a, w]
    if q_rope is not None:
        in_specs += [pl.BlockSpec((bm, LANES), lambda j, i: (i, 0))] * 2
        args += [q_rope[0], q_rope[1]]
    return pl.pallas_call(
        functools.partial(_mm_kernel, q_scale=None if q_rope is None else q_rope[2]),
        grid=(n // bn, m // bm),
        in_specs=in_specs,
        out_specs=pl.BlockSpec((bm, bn), lambda j, i: (i, j)),
        out_shape=jax.ShapeDtypeStruct((m, n), out_dtype),
        compiler_params=_cparams("parallel", "parallel"),
        name="matmul",
    )(*args)


def _staged_mm_kernel(*refs, n_w, layer, col0, bn, epilogue, has_res, transposed):
    a_ref = refs[0]
    w_hbm = refs[1:1 + n_w]
    res_ref = refs[1 + n_w] if has_res else None
    o_ref = refs[1 + n_w + int(has_res)]
    stage = refs[-1 - 2 * n_w:-1 - n_w]
    wb = refs[-1 - n_w:-1]
    sem = refs[-1]
    j = pl.program_id(0)
    i = pl.program_id(1)

    def copy(jj, t):
        if transposed:
            rows = pl.ds(pl.multiple_of(col0 + jj * bn, SUBLANES), bn)
            return pltpu.make_async_copy(w_hbm[t].at[layer, rows, :], stage[t], sem.at[t])
        cols = pl.ds(pl.multiple_of(col0 + jj * bn, LANES), bn)
        return pltpu.make_async_copy(w_hbm[t].at[layer, :, cols], stage[t], sem.at[t])

    def cast(t):
        if not transposed:
            wb[t][...] = stage[t][...].astype(BF16)
            return
        k = wb[t].shape[0]
        step = _tile(k, CAST_CHUNK)
        for c in range(0, k, step):
            wb[t][c:c + step, :] = stage[t][:, c:c + step].T.astype(BF16)

    @pl.when(i == 0)
    def _():
        @pl.when(j == 0)
        def _():
            for t in range(n_w):
                copy(0, t).start()

        for t in range(n_w):
            copy(j, t).wait()
            cast(t)

        @pl.when(j + 1 < pl.num_programs(0))
        def _():
            for t in range(n_w):
                copy(j + 1, t).start()

    a = a_ref[...]
    acc = _dot(a, wb[0][...])
    if epilogue == "swiglu":
        acc = _silu(acc) * _dot(a, wb[1][...])
    elif epilogue == "sigmoid":
        acc = _sigmoid(acc)
    if has_res:
        acc = res_ref[...] + acc
    o_ref[...] = acc.astype(o_ref.dtype)


def staged_matmul(a, ws, layer, col0, n, out_dtype, epilogue=None, res=None, transposed=False,
                  bm=MM_BM, bn=MM_BN):
    m, k = a.shape
    n_w = len(ws)
    bm = _tile(m, bm)
    bn = _tile(n, bn)
    assert col0 % (SUBLANES if transposed else LANES) == 0 and n % bn == 0
    in_specs = [pl.BlockSpec((bm, k), lambda j, i: (i, 0))]
    in_specs += [pl.BlockSpec(memory_space=pl.ANY)] * n_w
    args = [a, *ws]
    if res is not None:
        in_specs.append(pl.BlockSpec((bm, bn), lambda j, i: (i, j)))
        args.append(res)
    stage_shape = (bn, k) if transposed else (k, bn)
    scratch = ([pltpu.VMEM(stage_shape, F32)] * n_w + [pltpu.VMEM((k, bn), BF16)] * n_w
               + [pltpu.SemaphoreType.DMA((n_w,))])
    return pl.pallas_call(
        functools.partial(_staged_mm_kernel, n_w=n_w, layer=layer, col0=col0, bn=bn,
                          epilogue=epilogue, has_res=res is not None, transposed=transposed),
        grid=(n // bn, m // bm),
        in_specs=in_specs,
        out_specs=pl.BlockSpec((bm, bn), lambda j, i: (i, j)),
        out_shape=jax.ShapeDtypeStruct((m, n), out_dtype),
        scratch_shapes=scratch,
        compiler_params=_cparams("arbitrary", "arbitrary"),
        name="staged_matmul",
    )(*args)


def _mm_acc_kernel(a_ref, w_ref, res_ref, o_ref, acc_ref):
    kk = pl.program_id(2)

    @pl.when(kk == 0)
    def _():
        acc_ref[...] = jnp.zeros_like(acc_ref)

    acc_ref[...] += _dot(a_ref[...], w_ref[...])

    @pl.when(kk == pl.num_programs(2) - 1)
    def _():
        o_ref[...] = (res_ref[...] + acc_ref[...]).astype(o_ref.dtype)


def matmul_acc_res(a, w, res, bk, bm=MM_BM, bn=MM_BN):
    m, k = a.shape
    n = w.shape[1]
    bm = _tile(m, bm)
    bn = _tile(n, bn)
    return pl.pallas_call(
        _mm_acc_kernel,
        grid=(n // bn, m // bm, k // bk),
        in_specs=[pl.BlockSpec((bm, bk), lambda j, i, kk: (i, kk)),
                  pl.BlockSpec((bk, bn), lambda j, i, kk: (kk, j)),
                  pl.BlockSpec((bm, bn), lambda j, i, kk: (i, j))],
        out_specs=pl.BlockSpec((bm, bn), lambda j, i, kk: (i, j)),
        out_shape=jax.ShapeDtypeStruct((m, n), res.dtype),
        scratch_shapes=[pltpu.VMEM((bm, bn), F32)],
        compiler_params=_cparams("parallel", "parallel", "arbitrary"),
        name="matmul_acc_res",
    )(a, w, res)


def _head_norm_gate(o, gain, g):
    y = o * lax.rsqrt(jnp.mean(o * o, axis=-1, keepdims=True) + EPS) * gain
    return y * _silu(g.astype(F32))


def _retention_kernel(q_ref, k_ref, v_ref, g_ref, mg_ref, cos_ref, sin_ref, dmat_ref, qdec_ref, kdec_ref,
                      cdec_ref, gain_ref, o_ref, st_ref):
    @pl.when(pl.program_id(2) == 0)
    def _():
        st_ref[...] = jnp.zeros_like(st_ref)

    cos = cos_ref[...]
    sin = sin_ref[...]
    dk = RET_QK_DIM
    half = dk // 2
    hp = st_ref.shape[0]
    dv = st_ref.shape[1]
    for hh in range(hp):
        qk_cols = slice(hh * dk, (hh + 1) * dk)
        v_cols = slice(hh * dv, (hh + 1) * dv)
        q = q_ref[:, qk_cols].astype(F32)
        k = k_ref[:, qk_cols].astype(F32)
        qr = q * cos + pltpu.roll(q, half, 1) * sin
        kr = (k * cos + pltpu.roll(k, half, 1) * sin) * (dk ** -0.5)
        v = v_ref[:, v_cols]
        scores = _dot_nt(qr.astype(BF16), kr.astype(BF16))
        st = st_ref[hh]
        cross = _dot_nt((qr * qdec_ref[hh]).astype(BF16), st.astype(BF16))
        v_t = v.astype(F32).T.astype(BF16)
        st_ref[hh] = cdec_ref[hh] * st + _dot(v_t, (kr * kdec_ref[hh]).astype(BF16))
        intra = _dot((scores * dmat_ref[hh]).astype(BF16), v)
        y = _head_norm_gate(intra + cross, gain_ref[:, v_cols], g_ref[:, v_cols])
        o_ref[:, v_cols] = (mg_ref[:, v_cols].astype(F32) * y).astype(o_ref.dtype)


def retention(r, gates, cos2, sin2, norm_g, batch, seq):
    heads, dk = RET_HEADS, RET_QK_DIM
    dv = (r.shape[1] - 2 * heads * dk) // (2 * heads)
    blk = min(RET_BLOCK, seq)
    nblk = seq // blk
    log_gamma = jnp.log(1.0 - jnp.exp2(-5.0 - jnp.arange(heads, dtype=F32)))
    idx = jnp.arange(blk, dtype=F32)
    rel = idx[:, None] - idx[None, :]
    dmat = jnp.where(rel >= 0, jnp.exp(log_gamma[:, None, None] * jnp.maximum(rel, 0.0)), 0.0)
    qdec = jnp.broadcast_to(jnp.exp(log_gamma[:, None] * (idx + 1.0))[:, :, None], (heads, blk, dk))
    kdec = jnp.broadcast_to(jnp.exp(log_gamma[:, None] * (blk - 1.0 - idx))[:, :, None], (heads, blk, dk))
    cdec = jnp.broadcast_to(jnp.exp(log_gamma * blk)[:, None, None], (heads, 1, dk))
    hp = MIXER_HEADS_PER_STEP
    groups = heads // hp
    assert (2 * heads * dk) % (hp * dv) == 0
    v_off = 2 * heads * dk // (hp * dv)
    g_off = v_off + groups
    row = lambda b, h, t: b * nblk + t
    return pl.pallas_call(
        _retention_kernel,
        grid=(batch, groups, nblk),
        in_specs=[pl.BlockSpec((blk, hp * dk), lambda b, h, t: (row(b, h, t), h)),
                  pl.BlockSpec((blk, hp * dk), lambda b, h, t: (row(b, h, t), groups + h)),
                  pl.BlockSpec((blk, hp * dv), lambda b, h, t: (row(b, h, t), v_off + h)),
                  pl.BlockSpec((blk, hp * dv), lambda b, h, t: (row(b, h, t), g_off + h)),
                  pl.BlockSpec((blk, hp * dv), lambda b, h, t: (row(b, h, t), h)),
                  pl.BlockSpec((blk, dk), lambda b, h, t: (row(b, h, t), 0)),
                  pl.BlockSpec((blk, dk), lambda b, h, t: (row(b, h, t), 0)),
                  pl.BlockSpec((hp, blk, blk), lambda b, h, t: (h, 0, 0)),
                  pl.BlockSpec((hp, blk, dk), lambda b, h, t: (h, 0, 0)),
                  pl.BlockSpec((hp, blk, dk), lambda b, h, t: (h, 0, 0)),
                  pl.BlockSpec((hp, 1, dk), lambda b, h, t: (h, 0, 0)),
                  pl.BlockSpec((1, hp * dv), lambda b, h, t: (0, h))],
        out_specs=pl.BlockSpec((blk, hp * dv), lambda b, h, t: (row(b, h, t), h)),
        out_shape=jax.ShapeDtypeStruct((batch * seq, heads * dv), BF16),
        scratch_shapes=[pltpu.VMEM((hp, dv, dk), F32)],
        compiler_params=_cparams("parallel", "parallel", "arbitrary"),
        name="retention",
    )(r, r, r, r, gates, cos2, sin2, dmat, qdec, kdec, cdec, norm_g.reshape(1, -1).astype(F32))


def _hgrn_kernel(f_ref, q_ref, i_ref, og_ref, mg_ref, y_ref, la_ref, l1_ref, oml_ref, gain_ref, o_ref,
                 st_ref):
    @pl.when(pl.program_id(2) == 0)
    def _():
        st_ref[...] = jnp.zeros_like(st_ref)

    blk = f_ref.shape[0]
    hp, dv, e = st_ref.shape
    c = HGRN_CHUNK
    nch = blk // c
    row = lax.broadcasted_iota(jnp.int32, (blk, e), 0)
    row_in_chunk = row % c
    chunk_of_row = row // c
    r2 = lax.broadcasted_iota(jnp.int32, (blk, blk), 0)
    c2 = lax.broadcasted_iota(jnp.int32, (blk, blk), 1)
    keep = (r2 // c == c2 // c) & (c2 <= r2)

    prep = []
    for hh in range(hp):
        e_cols = slice(hh * e, (hh + 1) * e)
        z = f_ref[:, e_cols]
        log_sig = jnp.minimum(z, 0.0) - jnp.log1p(jnp.exp(-jnp.abs(z)))
        la = la_ref[:, e_cols]
        u = l1_ref[:, e_cols] + log_sig
        log_f = jnp.maximum(la, u) + jnp.log1p(jnp.exp(-jnp.abs(la - u)))
        key = oml_ref[:, e_cols] * (1.0 / (1.0 + jnp.exp(z)))
        b = log_f
        shift = 1
        while shift < c:
            b = b + jnp.where(row_in_chunk >= shift, pltpu.roll(b, shift, 0), 0.0)
            shift *= 2
        b3 = b.reshape(nch, c, e)
        b_mid = jnp.broadcast_to(b3[:, c // 2 - 1:c // 2, :], (nch, c, e)).reshape(blk, e)
        b_end_rows = b3[:, c - 1:c, :]
        b_end = jnp.broadcast_to(b_end_rows, (nch, c, e)).reshape(blk, e)
        q = q_ref[:, e_cols].astype(F32)
        v = i_ref[:, hh * dv:(hh + 1) * dv]
        qf = (q * jnp.exp(b - b_mid)).astype(BF16)
        kf = (key * jnp.exp(b_mid - b)).astype(BF16)
        q_in = (q * jnp.exp(b)).astype(BF16)
        k_out = (key * jnp.exp(b_end - b)).astype(BF16)
        zero = jnp.zeros_like(q_in)
        k_wide = jnp.concatenate([jnp.where(chunk_of_row == j, k_out, zero) for j in range(nch)], axis=1)
        q_wide = jnp.concatenate([jnp.where(chunk_of_row == j, q_in, zero) for j in range(nch)], axis=1)
        v_t = v.astype(F32).T.astype(BF16)
        prep.append((qf, kf, k_wide, q_wide, v, v_t, jnp.exp(b_end_rows)))

    scores = [_dot_nt(p[0], p[1]) for p in prep]
    upds = [_dot(p[5], p[2]) for p in prep]
    for hh in range(hp):
        _, _, _, q_wide, v, _, dec = prep[hh]
        intra = _dot(jnp.where(keep, scores[hh], 0.0).astype(BF16), v)
        st = st_ref[hh]
        states = []
        for j in range(nch):
            states.append(st.astype(BF16))
            st = st * dec[j] + upds[hh][:, j * e:(j + 1) * e]
        st_ref[hh] = st
        inter = _dot_nt(q_wide, jnp.concatenate(states, axis=1))
        v_cols = slice(hh * dv, (hh + 1) * dv)
        y = _head_norm_gate(intra + inter, gain_ref[:, v_cols], og_ref[:, v_cols])
        o_ref[:, v_cols] = (y_ref[:, v_cols].astype(F32) + mg_ref[:, v_cols].astype(F32) * y).astype(o_ref.dtype)


def hgrn2(hf, hr, gates, y_prev, lb, norm_g, batch, seq):
    heads, e = HGRN_HEADS, HGRN_EXPAND
    dv = (hr.shape[1] - heads * e) // (2 * heads)
    blk = min(HGRN_BLOCK, seq)
    nblk = seq // blk
    lb = lb.astype(F32).reshape(1, -1)
    la, l1, oml = jnp.log(lb), jnp.log1p(-lb), 1.0 - lb
    hp = MIXER_HEADS_PER_STEP
    groups = heads // hp
    assert (heads * e) % (hp * dv) == 0
    i_off = heads * e // (hp * dv)
    og_off = i_off + groups
    row = lambda b, h, t: b * nblk + t
    vec = pl.BlockSpec((1, hp * e), lambda b, h, t: (0, h))
    return pl.pallas_call(
        _hgrn_kernel,
        grid=(batch, groups, nblk),
        in_specs=[pl.BlockSpec((blk, hp * e), lambda b, h, t: (row(b, h, t), h)),
                  pl.BlockSpec((blk, hp * e), lambda b, h, t: (row(b, h, t), h)),
                  pl.BlockSpec((blk, hp * dv), lambda b, h, t: (row(b, h, t), i_off + h)),
                  pl.BlockSpec((blk, hp * dv), lambda b, h, t: (row(b, h, t), og_off + h)),
                  pl.BlockSpec((blk, hp * dv), lambda b, h, t: (row(b, h, t), groups + h)),
                  pl.BlockSpec((blk, hp * dv), lambda b, h, t: (row(b, h, t), h)),
                  vec, vec, vec,
                  pl.BlockSpec((1, hp * dv), lambda b, h, t: (0, h))],
        out_specs=pl.BlockSpec((blk, hp * dv), lambda b, h, t: (row(b, h, t), h)),
        out_shape=jax.ShapeDtypeStruct((batch * seq, heads * dv), BF16),
        scratch_shapes=[pltpu.VMEM((hp, dv, e), F32)],
        compiler_params=_cparams("parallel", "parallel", "arbitrary"),
        name="hgrn2",
    )(hf, hr, hr, hr, gates, y_prev, la, l1, oml, norm_g.reshape(1, -1).astype(F32))


def _mla_prep_kernel(p_ref, qg_ref, kvg_ref, cm_ref, sm_ref, cq_ref, ckv_ref, kr_ref, *, q_rank, kv_rank):
    p = p_ref[...]

    def norm(x, g):
        return x * lax.rsqrt(jnp.mean(x * x, axis=-1, keepdims=True) + EPS) * g

    cq_ref[...] = norm(p[:, :q_rank], qg_ref[...]).astype(cq_ref.dtype)
    ckv_ref[...] = norm(p[:, q_rank:q_rank + kv_rank], kvg_ref[...]).astype(ckv_ref.dtype)
    kr = p[:, q_rank + kv_rank:q_rank + kv_rank + LANES]
    lane = lax.broadcasted_iota(jnp.int32, kr.shape, 1)
    half = MLA_ROPE_DIM // 2
    swapped = jnp.where(lane < MLA_ROPE_DIM + half, pltpu.roll(kr, half, 1),
                        pltpu.roll(kr, MLA_ROPE_DIM + half, 1))
    pair = jnp.where(lane < MLA_ROPE_DIM, kr, swapped)
    kr_ref[...] = (pair * cm_ref[...] + pltpu.roll(pair, MLA_ROPE_DIM, 1) * sm_ref[...]).astype(kr_ref.dtype)


def mla_prep(pm, q_norm, kv_norm, cm, sm):
    m = pm.shape[0]
    q_rank, kv_rank = q_norm.shape[0], kv_norm.shape[0]
    rows = min(ROW_TILE, m)
    full = lambda w: pl.BlockSpec((rows, w), lambda i: (i, 0))
    return pl.pallas_call(
        functools.partial(_mla_prep_kernel, q_rank=q_rank, kv_rank=kv_rank),
        grid=(m // rows,),
        in_specs=[full(pm.shape[1]),
                  pl.BlockSpec((1, q_rank), lambda i: (0, 0)),
                  pl.BlockSpec((1, kv_rank), lambda i: (0, 0)),
                  full(LANES), full(LANES)],
        out_specs=[full(q_rank), full(kv_rank), full(LANES)],
        out_shape=[jax.ShapeDtypeStruct((m, q_rank), BF16),
                   jax.ShapeDtypeStruct((m, kv_rank), BF16),
                   jax.ShapeDtypeStruct((m, LANES), BF16)],
        compiler_params=_cparams("parallel"),
        name="mla_prep",
    )(pm, q_norm.reshape(1, -1).astype(F32), kv_norm.reshape(1, -1).astype(F32), cm, sm)


def _attn_kernel(q_ref, kv_ref, kr_ref, mg_ref, y_ref, o_ref, kcat_ref, vt_ref, *, heads_per_step, tk):
    i = pl.program_id(2)
    tq = q_ref.shape[0]
    kv_w = MLA_NOPE_DIM + MLA_V_DIM
    q_w = MLA_NOPE_DIM + 2 * MLA_ROPE_DIM

    @pl.when(i == 0)
    def _():
        for hh in range(heads_per_step):
            kcat_ref[hh, :, :MLA_NOPE_DIM] = kv_ref[:, hh * kv_w:hh * kv_w + MLA_NOPE_DIM]
            kcat_ref[hh, :, MLA_NOPE_DIM:] = kr_ref[...]
            for jj in range(vt_ref.shape[1]):
                v = kv_ref[jj * tk:(jj + 1) * tk, hh * kv_w + MLA_NOPE_DIM:(hh + 1) * kv_w]
                vt_ref[hh, jj] = v.astype(F32).T.astype(BF16)

    qfs = [q_ref[:, hh * q_w:(hh + 1) * q_w] for hh in range(heads_per_step)]

    def step(j, carry, keep):
        rows = pl.ds(pl.multiple_of(j * tk, tk), tk)
        out = []
        scores = [_dot_nt(kcat_ref[hh, rows, :], qfs[hh]) for hh in range(heads_per_step)]
        for hh in range(heads_per_step):
            m, l, acc = carry[hh]
            s = scores[hh]
            if keep is not None:
                s = jnp.where(keep, s, -jnp.inf)
            m_new = jnp.maximum(m, jnp.max(s, axis=0, keepdims=True))
            p = jnp.exp2(s - m_new)
            alpha = jnp.exp2(m - m_new)
            l = alpha * l + jnp.sum(p, axis=0, keepdims=True)
            acc = alpha * acc + _dot(vt_ref[hh, j], p.astype(BF16))
            out.append((m_new, l, acc))
        return tuple(out)

    init = tuple((jnp.full((1, tq), -jnp.inf, F32), jnp.zeros((1, tq), F32), jnp.zeros((MLA_V_DIM, tq), F32))
                 for _ in range(heads_per_step))
    per_q = tq // tk
    carry = lax.fori_loop(0, i * per_q, lambda j, cr: step(j, cr, None), init)
    k_in_tile = lax.broadcasted_iota(jnp.int32, (tk, tq), 0)
    q_in_tile = lax.broadcasted_iota(jnp.int32, (tk, tq), 1)
    for d in range(per_q):
        carry = step(i * per_q + d, carry, k_in_tile + d * tk <= q_in_tile)
    for hh in range(heads_per_step):
        _, l, acc = carry[hh]
        cols = slice(hh * MLA_V_DIM, (hh + 1) * MLA_V_DIM)
        y = y_ref[:, cols].astype(F32) + mg_ref[:, cols].astype(F32) * (acc / l).T
        o_ref[:, cols] = y.astype(o_ref.dtype)


def mla_attention(q, kv, kr, gates, y_prev, batch, seq):
    heads = MLA_HEADS
    hp = ATTN_HEADS_PER_STEP
    tq = min(ATTN_BLOCK, seq)
    tk = min(ATTN_KV_BLOCK, tq)
    nq = seq // tq
    qw = MLA_NOPE_DIM + 2 * MLA_ROPE_DIM
    kvw = MLA_NOPE_DIM + MLA_V_DIM
    return pl.pallas_call(
        functools.partial(_attn_kernel, heads_per_step=hp, tk=tk),
        grid=(batch, heads // hp, nq),
        in_specs=[pl.BlockSpec((tq, hp * qw), lambda b, h, i: (b * nq + i, h)),
                  pl.BlockSpec((seq, hp * kvw), lambda b, h, i: (b, h)),
                  pl.BlockSpec((seq, LANES), lambda b, h, i: (b, 0)),
                  pl.BlockSpec((tq, hp * MLA_V_DIM), lambda b, h, i: (b * nq + i, 2 * (heads // hp) + h)),
                  pl.BlockSpec((tq, hp * MLA_V_DIM), lambda b, h, i: (b * nq + i, h))],
        out_specs=pl.BlockSpec((tq, hp * MLA_V_DIM), lambda b, h, i: (b * nq + i, h)),
        out_shape=jax.ShapeDtypeStruct((batch * seq, heads * MLA_V_DIM), BF16),
        scratch_shapes=[pltpu.VMEM((hp, seq, MLA_NOPE_DIM + LANES), BF16),
                        pltpu.VMEM((hp, seq // tk, MLA_V_DIM, tk), BF16)],
        compiler_params=_cparams("parallel", "parallel", "arbitrary"),
        name="mla_attention",
    )(q, kv, kr, gates, y_prev)


def _router_kernel(x_ref, g_ref, r_ref, h_ref, idx_ref, w_ref):
    x = x_ref[...]
    h = x * lax.rsqrt(jnp.mean(x * x, axis=-1, keepdims=True) + EPS) * g_ref[...]
    h_ref[...] = _pack_bf16_pairs(h)
    logits = jnp.dot(h, r_ref[...], preferred_element_type=F32, precision=lax.Precision.HIGHEST)
    lane = lax.broadcasted_iota(jnp.int32, logits.shape, 1)
    lg = jnp.where(lane < N_EXPERTS, logits, -jnp.inf)
    m1 = jnp.max(lg, axis=-1, keepdims=True)
    i1 = jnp.min(jnp.where(lg == m1, lane, LANES), axis=-1, keepdims=True)
    lg2 = jnp.where(lane == i1, -jnp.inf, lg)
    m2 = jnp.max(lg2, axis=-1, keepdims=True)
    i2 = jnp.min(jnp.where(lg2 == m2, lane, LANES), axis=-1, keepdims=True)
    e2 = jnp.exp(m2 - m1)
    w1 = 1.0 / (1.0 + e2)
    w2 = e2 / (1.0 + e2)
    idx_ref[...] = jnp.where(lane == 0, i1, i2)
    w_ref[...] = jnp.where(lane == 0, w1, w2)


def moe_router(x, g, router):
    m, d = x.shape
    rows = min(ROW_TILE, m)
    r_pad = jnp.zeros((d, LANES), F32).at[:, :N_EXPERTS].set(router.astype(F32))
    h, idx, w = pl.pallas_call(
        _router_kernel,
        grid=(m // rows,),
        in_specs=[pl.BlockSpec((rows, d), lambda i: (i, 0)),
                  pl.BlockSpec((1, d), lambda i: (0, 0)),
                  pl.BlockSpec((d, LANES), lambda i: (0, 0))],
        out_specs=[pl.BlockSpec((rows, d // 2), lambda i: (i, 0)),
                   pl.BlockSpec((rows, LANES), lambda i: (i, 0)),
                   pl.BlockSpec((rows, LANES), lambda i: (i, 0))],
        out_shape=[jax.ShapeDtypeStruct((m, d // 2), jnp.int32),
                   jax.ShapeDtypeStruct((m, LANES), jnp.int32),
                   jax.ShapeDtypeStruct((m, LANES), F32)],
        compiler_params=_cparams("parallel"),
        name="moe_router",
    )(x, g.reshape(1, d).astype(F32), r_pad)
    return h, idx[:, :TOP_K], w[:, :TOP_K]


def _routing_tables(idx, wts, tile):
    t = idx.shape[0]
    pairs = t * TOP_K
    e = idx.reshape(pairs)
    onehot = (e[:, None] == jnp.arange(N_EXPERTS, dtype=jnp.int32)[None, :]).astype(jnp.int32)
    csum = jnp.cumsum(onehot, axis=0)
    rank = jnp.sum(csum * onehot, axis=1) - 1
    counts = csum[-1]
    padded = ((counts + tile - 1) // tile) * tile
    ends = jnp.cumsum(padded)
    starts = ends - padded
    pos = (jnp.sum(onehot * starts[None, :], axis=1) + rank).astype(jnp.int32)
    rows = pairs + N_EXPERTS * tile
    row_tok = jnp.zeros((rows,), jnp.int32).at[pos].set(jnp.arange(pairs, dtype=jnp.int32) // TOP_K)
    row_w = jnp.zeros((rows,), F32).at[pos].set(wts.reshape(pairs))
    tile_start = jnp.arange(rows // tile, dtype=jnp.int32) * tile
    tile_e = jnp.minimum(jnp.sum((tile_start[:, None] >= ends[None, :]).astype(jnp.int32), axis=1),
                         N_EXPERTS - 1).astype(jnp.int32)
    n_used = (ends[-1] // tile).astype(jnp.int32).reshape(1)
    n_tiles = rows // tile
    t_idx = jnp.arange(n_tiles, dtype=jnp.int32)
    prev_e = jnp.concatenate([jnp.full((1,), -1, jnp.int32), tile_e[:-1]])
    first = ((tile_e != prev_e) & (t_idx < n_used[0])).astype(jnp.int32)
    later_first = (t_idx[None, :] > t_idx[:, None]) & (first[None, :] == 1)
    next_idx = jnp.min(jnp.where(later_first, t_idx[None, :], n_tiles), axis=1)
    nxt = jnp.where(next_idx < n_tiles, tile_e[jnp.minimum(next_idx, n_tiles - 1)], -1).astype(jnp.int32)
    valid = jnp.clip((starts + counts)[tile_e] - tile_start, 0, tile).astype(jnp.int32)
    used_rows = (n_used * tile).astype(jnp.int32)
    return pos, row_tok, row_w.reshape(rows, 1), used_rows, (tile_e, first, nxt, valid)


def _row_copy(src_hbm, src_row, dst, dst_row, sem):
    return pltpu.make_async_copy(src_hbm.at[pl.ds(src_row, 1), :], dst.at[pl.ds(dst_row, 1), :], sem)


def _gather_kernel(tok_ref, used_ref, h_hbm, o_ref, buf, sem):
    tg = buf.shape[1]
    i = pl.program_id(0)
    slot = i % 2
    used_rows = used_ref[0]

    def issue(tile, dst_slot):
        def start(r, carry):
            _row_copy(h_hbm, tok_ref[tile * tg + r], buf.at[dst_slot], r, sem.at[dst_slot]).start()
            return carry
        lax.fori_loop(0, tg, start, 0, unroll=8)

    @pl.when((i == 0) & (used_rows > 0))
    def _():
        issue(0, 0)

    @pl.when((i + 1 < pl.num_programs(0)) & ((i + 1) * tg < used_rows))
    def _():
        issue(i + 1, 1 - slot)

    @pl.when(i * tg < used_rows)
    def _():
        def wait(r, carry):
            _row_copy(h_hbm, 0, buf.at[slot], r, sem.at[slot]).wait()
            return carry

        lax.fori_loop(0, tg, wait, 0, unroll=8)
        o_ref[...] = _unpack_bf16_pairs(buf[slot]).astype(o_ref.dtype)

    @pl.when(i * tg >= used_rows)
    def _():
        o_ref[...] = jnp.zeros_like(o_ref)


def moe_gather(h, row_tok, used_rows):
    rows = row_tok.shape[0]
    d = 2 * h.shape[1]
    tg = GATHER_TILE
    return pl.pallas_call(
        _gather_kernel,
        grid_spec=pltpu.PrefetchScalarGridSpec(
            num_scalar_prefetch=2,
            grid=(rows // tg,),
            in_specs=[pl.BlockSpec(memory_space=pl.ANY)],
            out_specs=pl.BlockSpec((tg, d), lambda i, tok, used: (i, 0)),
            scratch_shapes=[pltpu.VMEM((2, tg, d // 2), h.dtype), pltpu.SemaphoreType.DMA((2,))]),
        out_shape=jax.ShapeDtypeStruct((rows, d), BF16),
        compiler_params=_cparams("arbitrary"),
        name="moe_gather",
    )(row_tok, used_rows, h)


def _moe_mm_kernel(*refs, n_w, layer, bn, has_scale):
    te_ref, first_ref, nxt_ref, valid_ref, a_ref = refs[:5]
    w_hbm = refs[5:5 + n_w]
    rw_ref = refs[5 + n_w] if has_scale else None
    o_ref = refs[5 + n_w + int(has_scale)]
    stage = refs[-1 - 2 * n_w:-1 - n_w]
    wb = refs[-1 - n_w:-1]
    sem = refs[-1]
    j = pl.program_id(0)
    i = pl.program_id(1)

    def copy(e, jj, t):
        cols = pl.ds(pl.multiple_of(jj * bn, LANES), bn)
        return pltpu.make_async_copy(w_hbm[t].at[layer, e, :, cols], stage[t], sem.at[t])

    @pl.when(first_ref[i] == 1)
    def _():
        @pl.when((j == 0) & (i == 0))
        def _():
            for t in range(n_w):
                copy(te_ref[0], 0, t).start()

        for t in range(n_w):
            copy(te_ref[i], j, t).wait()
            wb[t][...] = stage[t][...].astype(BF16)

        nxt = nxt_ref[i]

        @pl.when(nxt >= 0)
        def _():
            for t in range(n_w):
                copy(nxt, j, t).start()

        @pl.when((nxt < 0) & (j + 1 < pl.num_programs(0)))
        def _():
            for t in range(n_w):
                copy(te_ref[0], j + 1, t).start()

    tm = a_ref.shape[0]
    half = tm // 2
    valid = valid_ref[i]

    def rows_out(rows):
        a = a_ref[rows, :]
        acc = _dot(a, wb[0][...])
        if n_w == 2:
            acc = _silu(acc) * _dot(a, wb[1][...])
        if has_scale:
            o_ref[rows, :] = _pack_bf16_pairs(rw_ref[rows, :] * acc)
        else:
            o_ref[rows, :] = acc.astype(o_ref.dtype)

    @pl.when(valid > half)
    def _():
        rows_out(slice(0, tm))

    @pl.when((valid > 0) & (valid <= half))
    def _():
        rows_out(slice(0, half))
        o_ref[half:, :] = jnp.zeros((tm - half, o_ref.shape[1]), o_ref.dtype)

    @pl.when(valid == 0)
    def _():
        o_ref[...] = jnp.zeros_like(o_ref)


def moe_matmul(a, ws, layer, tables, bn, row_scale=None):
    tile_e, first, nxt, valid = tables
    rows, k = a.shape
    n = ws[0].shape[-1]
    n_w = len(ws)
    tm = MOE_TILE
    bn = _tile(n, bn)
    packed = row_scale is not None
    out_bn, out_n, out_dtype = (bn // 2, n // 2, jnp.int32) if packed else (bn, n, BF16)
    idx = lambda j, i, *_: (i, 0)
    in_specs = [pl.BlockSpec((tm, k), idx)] + [pl.BlockSpec(memory_space=pl.ANY)] * n_w
    args = [a, *ws]
    if row_scale is not None:
        in_specs.append(pl.BlockSpec((tm, 1), idx))
        args.append(row_scale)
    scratch = ([pltpu.VMEM((k, bn), F32)] * n_w + [pltpu.VMEM((k, bn), BF16)] * n_w
               + [pltpu.SemaphoreType.DMA((n_w,))])
    return pl.pallas_call(
        functools.partial(_moe_mm_kernel, n_w=n_w, layer=layer, bn=bn, has_scale=row_scale is not None),
        grid_spec=pltpu.PrefetchScalarGridSpec(
            num_scalar_prefetch=4,
            grid=(n // bn, rows // tm),
            in_specs=in_specs,
            out_specs=pl.BlockSpec((tm, out_bn), lambda j, i, *_: (i, j)),
            scratch_shapes=scratch),
        out_shape=jax.ShapeDtypeStruct((rows, out_n), out_dtype),
        compiler_params=_cparams("arbitrary", "arbitrary"),
        name="moe_matmul",
    )(tile_e, first, nxt, valid, *args)


def _combine_kernel(pos_ref, x_ref, y_hbm, g_ref, o_ref, buf, sem, *, apply_norm, pack_tile):
    tb = x_ref.shape[0]
    i = pl.program_id(0)
    slot = i % 2

    def issue(tile, dst_slot):
        def start(r, carry):
            for s in range(TOP_K):
                _row_copy(y_hbm, pos_ref[(tile * tb + r) * TOP_K + s], buf.at[dst_slot, s], r,
                          sem.at[dst_slot]).start()
            return carry
        lax.fori_loop(0, tb, start, 0, unroll=4)

    @pl.when(i == 0)
    def _():
        issue(0, 0)

    @pl.when(i + 1 < pl.num_programs(0))
    def _():
        issue(i + 1, 1 - slot)

    def wait(r, carry):
        for s in range(TOP_K):
            _row_copy(y_hbm, 0, buf.at[slot, s], r, sem.at[slot]).wait()
        return carry

    lax.fori_loop(0, tb, wait, 0, unroll=4)
    x = x_ref[...]
    half_tile = pack_tile // 2
    for s in range(TOP_K):
        w = buf[slot, s]
        x = x + jnp.concatenate([_unpack_bf16_pairs(w[:, c:c + half_tile])
                                 for c in range(0, w.shape[1], half_tile)], axis=1)
    if apply_norm:
        x = x * lax.rsqrt(jnp.mean(x * x, axis=-1, keepdims=True) + EPS) * g_ref[...]
    o_ref[...] = x.astype(o_ref.dtype)


def moe_combine(x, y, pos, final_g, apply_norm, pack_tile):
    m, d = x.shape
    tb = GATHER_TILE
    return pl.pallas_call(
        functools.partial(_combine_kernel, apply_norm=apply_norm, pack_tile=pack_tile),
        grid_spec=pltpu.PrefetchScalarGridSpec(
            num_scalar_prefetch=1,
            grid=(m // tb,),
            in_specs=[pl.BlockSpec((tb, d), lambda i, p: (i, 0)),
                      pl.BlockSpec(memory_space=pl.ANY),
                      pl.BlockSpec((1, d), lambda i, p: (0, 0))],
            out_specs=pl.BlockSpec((tb, d), lambda i, p: (i, 0)),
            scratch_shapes=[pltpu.VMEM((2, TOP_K, tb, d // 2), y.dtype), pltpu.SemaphoreType.DMA((2,))]),
        out_shape=jax.ShapeDtypeStruct((m, d), x.dtype),
        compiler_params=_cparams("arbitrary"),
        name="moe_combine_norm",
    )(pos, x, y, final_g.reshape(1, d).astype(F32))


def _rope_tables(positions, dim, pad_to):
    inv = ROPE_THETA ** (-jnp.arange(0, dim, 2, dtype=F32) / dim)
    ang = positions.astype(F32).reshape(-1)[:, None] * inv
    cos, sin = jnp.cos(ang), jnp.sin(ang)
    pad = jnp.zeros((ang.shape[0], pad_to - dim), F32)
    return (jnp.concatenate([cos, cos, pad], axis=1), jnp.concatenate([-sin, sin, pad], axis=1))


def _swap_halves(w):
    half = w.shape[-1] // 2
    return jnp.concatenate([w[..., half:], w[..., :half]], axis=-1)


def kernel(x, positions, ln_mix, w_in, ret_norm, hgrn_norm, hgrn_lb_logits, mla_q_norm, mla_w_uq,
           mla_kv_norm, mla_w_ukv, w_out, ln_ffn, ffn_w1, ffn_w3, ffn_w2, moe_router_w, moe_w1,
           moe_w3, moe_w2, final_norm):
    batch, seq, d_model = x.shape
    depth = w_in.shape[0]
    q_rank, kv_rank = mla_q_norm.shape[1], mla_kv_norm.shape[1]
    ret_w = 2 * RET_HEADS * RET_QK_DIM + 2 * d_model
    hg_e = HGRN_HEADS * HGRN_EXPAND
    hg_w = 2 * hg_e + 2 * d_model
    mla_w = q_rank + kv_rank + MLA_ROPE_DIM
    mla_pad = -(-(q_rank + kv_rank + LANES) // (2 * LANES)) * (2 * LANES)
    w_in_t = jnp.swapaxes(w_in, 1, 2)

    cos_r, sin_r = _rope_tables(positions, RET_QK_DIM, RET_QK_DIM)
    cm, sm = _rope_tables(positions, MLA_ROPE_DIM, LANES)
    lb_all = jnp.cumsum(jax.nn.softmax(hgrn_lb_logits.astype(F32), axis=0), axis=0)
    lb_all = lb_all - lb_all[:1]

    xf = x.reshape(batch * seq, d_model)
    for l in range(depth):
        last = l == depth - 1
        o = ret_w + hg_w
        h = rmsnorm(xf, ln_mix[l], BF16)
        r = staged_matmul(h, [w_in_t], l, 0, ret_w, BF16, transposed=True, bm=IN_PROJ_BM)
        hf = staged_matmul(h, [w_in_t], l, ret_w, hg_e, F32, transposed=True, bm=IN_PROJ_BM)
        hr = staged_matmul(h, [w_in_t], l, ret_w + hg_e, hg_w - hg_e, BF16, transposed=True, bm=IN_PROJ_BM)
        pm = staged_matmul(h, [w_in_t], l, o, mla_pad, F32, transposed=True, bn=mla_pad // 2)
        gates = staged_matmul(h, [w_in_t], l, o + mla_w, N_BRANCHES * d_model, BF16, epilogue="sigmoid",
                              transposed=True, bm=IN_PROJ_BM)

        y = retention(r, gates, cos_r, sin_r, ret_norm[l], batch, seq)
        y = hgrn2(hf, hr, gates, y, lb_all[l], hgrn_norm[l], batch, seq)

        cqn, ckvn, kr = mla_prep(pm, mla_q_norm[l], mla_kv_norm[l], cm, sm)
        wq = mla_w_uq[l].reshape(q_rank, MLA_HEADS, MLA_NOPE_DIM + MLA_ROPE_DIM)
        wq_rope = wq[..., MLA_NOPE_DIM:]
        wq = jnp.concatenate([wq, _swap_halves(wq_rope)], axis=-1).reshape(q_rank, -1).astype(BF16)
        q_scale = (MLA_NOPE_DIM + MLA_ROPE_DIM) ** -0.5 * LOG2_E
        q = matmul(cqn, wq, BF16, bn=MLA_UP_BN, q_rope=(cm, sm, q_scale))
        kv = matmul(ckvn, mla_w_ukv[l].astype(BF16), BF16, bn=MLA_UP_BN)
        y = mla_attention(q, kv, kr, gates, y, batch, seq)
        xf = staged_matmul(y, [w_out], l, 0, d_model, F32, res=xf)

        if l % 2 == 0:
            j = l // 2
            h2 = rmsnorm(xf, ln_ffn[l], BF16)
            g = staged_matmul(h2, [ffn_w1, ffn_w3], j, 0, ffn_w1.shape[-1], BF16, epilogue="swiglu")
            xf = matmul_acc_res(g, ffn_w2[j].astype(BF16), xf, _tile(g.shape[1], FFN_DOWN_BK), bn=FFN_DOWN_BN)
            if last:
                xf = rmsnorm(xf, final_norm, x.dtype)
        else:
            j = l // 2
            h2, idx, wts = moe_router(xf, ln_ffn[l], moe_router_w[j])
            pos, row_tok, row_w, used_rows, tables = _routing_tables(idx, wts, MOE_TILE)
            xs = moe_gather(h2, row_tok, used_rows)
            g = moe_matmul(xs, [moe_w1, moe_w3], j, tables, MOE_UP_BN)
            down_bn = _tile(d_model, MOE_DOWN_BN)
            yrows = moe_matmul(g, [moe_w2], j, tables, down_bn, row_scale=row_w)
            xf = moe_combine(xf, yrows, pos, final_norm, apply_norm=last, pack_tile=down_bn)
    return xf.reshape(batch, seq, d_model)
```
